```python
import jax, jax.numpy as jnp
from jax import lax
import numpy as np

D_MODEL = 4096
BATCH = 2
SEQ = 4096
DEPTH = 2

CHUNK = 64
Q_BLOCK = 128
MAX_POS_OFFSET = 8192

MLA_HEADS = 16
MLA_Q_RANK = 896
MLA_KV_RANK = 512
MLA_NOPE = 128
MLA_ROPE = 64
MLA_V = 128
ROPE_THETA = 10000.0
LRU_WIDTH = 1024
LRU_BLOCKS = 8
LRU_BLOCK = LRU_WIDTH // LRU_BLOCKS
LRU_CONV = 4
LRU_C = 8.0
RW_WIDTH = 1024
RW_HEAD = 64
RW_HEADS = RW_WIDTH // RW_HEAD
RW_DECAY_LORA = 64
RW_AAA_LORA = 64
RW_GATE_LORA = 160
N_BRANCH = 3
D_FF = 11008
FFN_CONV = 3
ALPHA = (2 * DEPTH) ** 0.25
BETA = (8 * DEPTH) ** -0.25
LN_EPS = 1e-5
RMS_EPS = 1e-6
GN_EPS = 64e-5

MLA_COLS = MLA_Q_RANK + MLA_KV_RANK + MLA_ROPE
LRU_COLS = 2 * LRU_WIDTH
RW_COLS = 3 * RW_WIDTH + RW_DECAY_LORA + RW_AAA_LORA + RW_GATE_LORA
GATE_COLS = N_BRANCH * D_MODEL
IN_COLS = MLA_COLS + LRU_COLS + RW_COLS + GATE_COLS

kernel_name = 'hybrid_mla_rglru_rwkv7_deepnorm_encoder'


def _split(x, sizes):
    idx = [int(v) for v in np.cumsum(sizes)[:-1]]
    return jnp.split(x, idx, axis=-1)


def layer_norm(x, g, b):
    xf = x.astype(jnp.float32)
    mu = jnp.mean(xf, -1, keepdims=True)
    var = jnp.mean(jnp.square(xf - mu), -1, keepdims=True)
    return ((xf - mu) * lax.rsqrt(var + LN_EPS) * g + b).astype(x.dtype)


def rms_norm(x, g):
    xf = x.astype(jnp.float32)
    return (xf * lax.rsqrt(jnp.mean(jnp.square(xf), -1, keepdims=True) + RMS_EPS) * g).astype(x.dtype)


def causal_dwconv(x, w, b):
    K, S = w.shape[0], x.shape[1]
    xp = jnp.pad(x, ((0, 0), (K - 1, 0), (0, 0)))
    return sum(xp[:, j:j + S] * w[j] for j in range(K)) + b


def token_shift(x):
    return jnp.pad(x, ((0, 0), (1, 0), (0, 0)))[:, :-1]


def rope_tables(positions):
    inv_freq = ROPE_THETA ** (-jnp.arange(0, MLA_ROPE, 2, dtype=jnp.float32) / MLA_ROPE)
    ang = positions.astype(jnp.float32)[..., None] * inv_freq
    return jnp.cos(ang), jnp.sin(ang)


def apply_rope(x, cos, sin):
    half = MLA_ROPE // 2
    x1, x2 = x[..., :half].astype(jnp.float32), x[..., half:].astype(jnp.float32)
    return jnp.concatenate([x1 * cos - x2 * sin, x1 * sin + x2 * cos], -1).astype(x.dtype)


def mla_branch(c_q, c_kv, k_r, positions, q_norm, w_uq, kv_norm, w_ukv):
    B, S, _ = c_q.shape
    q = jnp.einsum('bsr,rhd->bshd', rms_norm(c_q, q_norm), w_uq)
    kv = jnp.einsum('bsr,rhd->bshd', rms_norm(c_kv, kv_norm), w_ukv)
    q_nope, q_rope = q[..., :MLA_NOPE], q[..., MLA_NOPE:]
    k_nope, v = kv[..., :MLA_NOPE], kv[..., MLA_NOPE:]
    cos, sin = rope_tables(positions)
    q_rope = apply_rope(q_rope, cos[:, :, None], sin[:, :, None])
    k_rope = apply_rope(k_r, cos, sin)
    scale = (MLA_NOPE + MLA_ROPE) ** -0.5
    chunk_id = jnp.arange(S) // CHUNK
    outs = []
    for q0 in range(0, S, Q_BLOCK):
        q1 = q0 + Q_BLOCK
        s = (jnp.einsum('bqhd,bkhd->bhqk', q_nope[:, q0:q1], k_nope[:, :q1])
             + jnp.einsum('bqhd,bkd->bhqk', q_rope[:, q0:q1], k_rope[:, :q1]))
        mask = chunk_id[None, :q1] <= chunk_id[q0:q1, None]
        s = jnp.where(mask, s.astype(jnp.float32) * scale, -jnp.inf)
        p = jax.nn.softmax(s, axis=-1).astype(v.dtype)
        outs.append(jnp.einsum('bhqk,bkhd->bqhd', p, v[:, :q1]))
    return jnp.concatenate(outs, 1).reshape(B, S, MLA_HEADS * MLA_V)


def linear_scan(a, u):
    def combine(left, right):
        a_l, u_l = left
        a_r, u_r = right
        return a_l * a_r, a_r * u_l + u_r
    _, h = lax.associative_scan(combine, (a, u), axis=1)
    return h


def rglru_branch(x_in, gate_in, conv_w, conv_b, w_a, b_a, w_x, b_x, lam):
    B, S, W = x_in.shape
    xc = causal_dwconv(x_in, conv_w, conv_b)
    xb = xc.reshape(B, S, LRU_BLOCKS, LRU_BLOCK)
    r = jax.nn.sigmoid(jnp.einsum('bsnc,ncd->bsnd', xb, w_a) + b_a).reshape(B, S, W)
    i = jax.nn.sigmoid(jnp.einsum('bsnc,ncd->bsnd', xb, w_x) + b_x).reshape(B, S, W)
    log_a = (-LRU_C * r * jax.nn.softplus(-lam)).astype(jnp.float32)
    a = jnp.exp(log_a)
    u = jnp.sqrt(-jnp.expm1(2.0 * log_a)) * (i * xc).astype(jnp.float32)
    h = linear_scan(a, u)
    return h.astype(x_in.dtype) * jax.nn.gelu(gate_in)


def rwkv7_branch(p, mu, w0, w2, a0, a2, g2, k_k, k_a, r_k, gn_g, gn_b):
    B, S, _ = p.shape
    p = p + mu * (token_shift(p) - p)
    r, k, v, xw, xa, xg = _split(p, [RW_WIDTH] * 3 + [RW_DECAY_LORA, RW_AAA_LORA, RW_GATE_LORA])
    w = -jax.nn.softplus(-(w0 + jnp.tanh(xw) @ w2)) - 0.5
    a = jax.nn.sigmoid(a0 + xa @ a2)
    g = jax.nn.sigmoid(xg) @ g2
    heads = lambda t: t.reshape(B, S, RW_HEADS, RW_HEAD).astype(jnp.float32)
    kk = heads(k * k_k)
    kk = kk / jnp.maximum(jnp.linalg.norm(kk, axis=-1, keepdims=True), 1e-12)
    k = k * (1 + (a - 1) * k_a)
    rh, kh, vh, ah = heads(r), heads(k), heads(v), heads(a)
    decay = jnp.exp(-jnp.exp(heads(w)))

    def step(state, inp):
        r_t, d_t, k_t, v_t, kk_t, b_t = inp
        sa = jnp.einsum('bhvk,bhk->bhv', state, -kk_t)
        state = (state * d_t[:, :, None, :] + sa[..., None] * b_t[:, :, None, :]
                 + v_t[..., None] * k_t[:, :, None, :])
        return state, jnp.einsum('bhvk,bhk->bhv', state, r_t)

    tm = lambda t: jnp.moveaxis(t, 1, 0)
    s0 = jnp.zeros((B, RW_HEADS, RW_HEAD, RW_HEAD), jnp.float32)
    _, o = lax.scan(step, s0, (tm(rh), tm(decay), tm(kh), tm(vh), tm(kk), tm(kk * ah)))
    o = jnp.moveaxis(o, 0, 1)
    mean = jnp.mean(o, -1, keepdims=True)
    var = jnp.mean(jnp.square(o - mean), -1, keepdims=True)
    o = ((o - mean) * lax.rsqrt(var + GN_EPS)).reshape(B, S, RW_WIDTH) * gn_g + gn_b
    bonus = jnp.sum(rh * kh * r_k, -1, keepdims=True) * vh
    o = o + bonus.reshape(B, S, RW_WIDTH)
    return (o * g).astype(p.dtype)


def mixer_sublayer(x, positions, w_in, mla_q_norm, mla_w_uq, mla_kv_norm, mla_w_ukv,
                   lru_conv_w, lru_conv_b, lru_w_a, lru_b_a, lru_w_x, lru_b_x, lru_lambda,
                   rw_mu, rw_w0, rw_w2, rw_a0, rw_a2, rw_g2, rw_k_k, rw_k_a, rw_r_k, rw_gn_g, rw_gn_b,
                   w_o_mla, w_o_lru, w_o_rwkv, w_out, ln_g, ln_b):
    B, S, _ = x.shape
    proj = x @ w_in
    c_q, c_kv, k_r, lru_x, lru_g, rw_p, gate_logits = _split(
        proj, [MLA_Q_RANK, MLA_KV_RANK, MLA_ROPE, LRU_WIDTH, LRU_WIDTH, RW_COLS, GATE_COLS])
    y_a = mla_branch(c_q, c_kv, k_r, positions, mla_q_norm, mla_w_uq, mla_kv_norm, mla_w_ukv) @ w_o_mla
    y_b = rglru_branch(lru_x, lru_g, lru_conv_w, lru_conv_b, lru_w_a, lru_b_a,
                       lru_w_x, lru_b_x, lru_lambda) @ w_o_lru
    y_c = rwkv7_branch(rw_p, rw_mu, rw_w0, rw_w2, rw_a0, rw_a2, rw_g2, rw_k_k, rw_k_a,
                       rw_r_k, rw_gn_g, rw_gn_b) @ w_o_rwkv
    gates = jax.nn.sigmoid(gate_logits).reshape(B, S, N_BRANCH, D_MODEL)
    merged = gates[:, :, 0] * y_a + gates[:, :, 1] * y_b + gates[:, :, 2] * y_c
    return layer_norm(ALPHA * x + merged @ w_out, ln_g, ln_b)


def ffn_sublayer(x, w_up, conv_w, conv_b, w_down, ln_g, ln_b):
    gate, val = jnp.split(x @ w_up, 2, axis=-1)
    h = jax.nn.gelu(causal_dwconv(gate, conv_w, conv_b)) * val
    return layer_norm(ALPHA * x + h @ w_down, ln_g, ln_b)


def setup_inputs(seed: int = 0) -> dict:
    key = jax.random.key(seed)
    ks = iter(jax.random.split(key, 64))
    L = DEPTH
    f32 = jnp.float32

    def nrm(shape, scale):
        return jax.random.normal(next(ks), shape, f32) * scale

    def gain(shape):
        return 1.0 + nrm(shape, 0.02)

    x = nrm((BATCH, SEQ, D_MODEL), 1.0)
    offset = jax.random.randint(next(ks), (BATCH, 1), 0, MAX_POS_OFFSET, jnp.int32)
    positions = (offset + jnp.arange(SEQ, dtype=jnp.int32)[None, :]).astype(jnp.int32)

    u = jax.random.uniform(next(ks), (L, LRU_WIDTH), f32, 0.9, 0.999)
    a_base = u ** (1.0 / LRU_C)
    lru_lambda = jnp.log(a_base) - jnp.log1p(-a_base)

    ratio = jnp.arange(RW_WIDTH, dtype=f32) / (RW_WIDTH - 1)
    layer_ratio = (jnp.arange(L, dtype=f32) / max(L - 1, 1))[:, None]
    decay_speed = -7.0 + 5.0 * ratio[None, :] ** (0.85 + layer_ratio ** 0.5)
    rw_w0 = decay_speed + 0.5 + nrm((L, RW_WIDTH), 0.05)

    return {
        'x': x,
        'positions': positions,
        'w_in': nrm((L, D_MODEL, IN_COLS), D_MODEL ** -0.5),
        'mla_q_norm': gain((L, MLA_Q_RANK)),
        'mla_w_uq': nrm((L, MLA_Q_RANK, MLA_HEADS, MLA_NOPE + MLA_ROPE), MLA_Q_RANK ** -0.5),
        'mla_kv_norm': gain((L, MLA_KV_RANK)),
        'mla_w_ukv': nrm((L, MLA_KV_RANK, MLA_HEADS, MLA_NOPE + MLA_V), MLA_KV_RANK ** -0.5),
        'lru_conv_w': nrm((L, LRU_CONV, LRU_WIDTH), LRU_CONV ** -0.5),
        'lru_conv_b': nrm((L, LRU_WIDTH), 0.01),
        'lru_w_a': nrm((L, LRU_BLOCKS, LRU_BLOCK, LRU_BLOCK), LRU_BLOCK ** -0.5),
        'lru_b_a': nrm((L, LRU_BLOCKS, LRU_BLOCK), 0.01),
        'lru_w_x': nrm((L, LRU_BLOCKS, LRU_BLOCK, LRU_BLOCK), LRU_BLOCK ** -0.5),
        'lru_b_x': nrm((L, LRU_BLOCKS, LRU_BLOCK), 0.01),
        'lru_lambda': lru_lambda,
        'rw_mu': jax.random.uniform(next(ks), (L, RW_COLS), f32),
        'rw_w0': rw_w0,
        'rw_w2': nrm((L, RW_DECAY_LORA, RW_WIDTH), 0.5 * RW_DECAY_LORA ** -0.5),
        'rw_a0': nrm((L, RW_WIDTH), 0.1),
        'rw_a2': nrm((L, RW_AAA_LORA, RW_WIDTH), RW_AAA_LORA ** -0.5),
        'rw_g2': nrm((L, RW_GATE_LORA, RW_WIDTH), RW_GATE_LORA ** -0.5),
        'rw_k_k': 0.85 + nrm((L, RW_WIDTH), 0.02),
        'rw_k_a': gain((L, RW_WIDTH)),
        'rw_r_k': nrm((L, RW_HEADS, RW_HEAD), 0.1),
        'rw_gn_g': gain((L, RW_WIDTH)),
        'rw_gn_b': nrm((L, RW_WIDTH), 0.01),
        'w_o_mla': nrm((L, MLA_HEADS * MLA_V, D_MODEL), BETA * (MLA_HEADS * MLA_V) ** -0.5),
        'w_o_lru': nrm((L, LRU_WIDTH, D_MODEL), BETA * LRU_WIDTH ** -0.5),
        'w_o_rwkv': nrm((L, RW_WIDTH, D_MODEL), BETA * RW_WIDTH ** -0.5),
        'w_out': nrm((L, D_MODEL, D_MODEL), BETA * D_MODEL ** -0.5),
        'ln1_g': gain((L, D_MODEL)),
        'ln1_b': nrm((L, D_MODEL), 0.01),
        'ffn_w_up': nrm((L, D_MODEL, 2 * D_FF), D_MODEL ** -0.5),
        'ffn_conv_w': nrm((L, FFN_CONV, D_FF), FFN_CONV ** -0.5),
        'ffn_conv_b': nrm((L, D_FF), 0.01),
        'ffn_w_down': nrm((L, D_FF, D_MODEL), BETA * D_FF ** -0.5),
        'ln2_g': gain((L, D_MODEL)),
        'ln2_b': nrm((L, D_MODEL), 0.01),
    }


def reference(x, positions, w_in, mla_q_norm, mla_w_uq, mla_kv_norm, mla_w_ukv,
              lru_conv_w, lru_conv_b, lru_w_a, lru_b_a, lru_w_x, lru_b_x, lru_lambda,
              rw_mu, rw_w0, rw_w2, rw_a0, rw_a2, rw_g2, rw_k_k, rw_k_a, rw_r_k, rw_gn_g, rw_gn_b,
              w_o_mla, w_o_lru, w_o_rwkv, w_out, ln1_g, ln1_b,
              ffn_w_up, ffn_conv_w, ffn_conv_b, ffn_w_down, ln2_g, ln2_b):
    for l in range(DEPTH):
        x = mixer_sublayer(x, positions, w_in[l], mla_q_norm[l], mla_w_uq[l], mla_kv_norm[l], mla_w_ukv[l],
                           lru_conv_w[l], lru_conv_b[l], lru_w_a[l], lru_b_a[l], lru_w_x[l], lru_b_x[l],
                           lru_lambda[l], rw_mu[l], rw_w0[l], rw_w2[l], rw_a0[l], rw_a2[l], rw_g2[l],
                           rw_k_k[l], rw_k_a[l], rw_r_k[l], rw_gn_g[l], rw_gn_b[l],
                           w_o_mla[l], w_o_lru[l], w_o_rwkv[l], w_out[l], ln1_g[l], ln1_b[l])
        x = ffn_sublayer(x, ffn_w_up[l], ffn_conv_w[l], ffn_conv_b[l], ffn_w_down[l], ln2_g[l], ln2_b[l])
    return x
```

```python
import functools

import jax
import jax.numpy as jnp
from jax import lax
from jax.experimental import pallas as pl
from jax.experimental.pallas import tpu as pltpu

F32 = jnp.float32
BF16 = jnp.bfloat16

D_MODEL = 4096
DEPTH = 2
CHUNK = 64
MLA_HEADS = 16
MLA_Q_RANK = 896
MLA_KV_RANK = 512
MLA_NOPE = 128
MLA_ROPE = 64
MLA_V = 128
ROPE_THETA = 10000.0
LRU_WIDTH = 1024
LRU_BLOCKS = 8
LRU_BLOCK = LRU_WIDTH // LRU_BLOCKS
LRU_CONV = 4
LRU_C = 8.0
RW_WIDTH = 1024
RW_HEAD = 64
RW_HEADS = RW_WIDTH // RW_HEAD
RW_DECAY_LORA = 64
RW_AAA_LORA = 64
RW_GATE_LORA = 160
D_FF = 11008
FFN_CONV = 3
ALPHA = (2 * DEPTH) ** 0.25
LN_EPS = 1e-5
RMS_EPS = 1e-6
GN_EPS = 64e-5

LANES = 128
SUBLANES = 8
VMEM_LIMIT_BYTES = 56 * 1024 * 1024

Q_PAD = 1024
D_FF_PAD = 11264
OFF_GATES = 0
OFF_LRU_X = 3 * D_MODEL
OFF_LRU_G = OFF_LRU_X + LRU_WIDTH
OFF_RW_R = OFF_LRU_G + LRU_WIDTH
OFF_RW_K = OFF_RW_R + RW_WIDTH
OFF_RW_V = OFF_RW_K + RW_WIDTH
OFF_CQ = OFF_RW_V + RW_WIDTH
OFF_CKV = OFF_CQ + Q_PAD
OFF_KR = OFF_CKV + MLA_KV_RANK
OFF_XWA = OFF_KR + 2 * MLA_ROPE
OFF_XG = OFF_XWA + RW_DECAY_LORA + RW_AAA_LORA
IN_COLS_PAD = OFF_XG + 256

_SQRT_2_OVER_PI = 0.7978845608028654


def _cparams(*sem):
    return pltpu.CompilerParams(dimension_semantics=sem, vmem_limit_bytes=VMEM_LIMIT_BYTES)


def _gelu_tanh(x):
    return 0.5 * x * (1.0 + jnp.tanh(_SQRT_2_OVER_PI * (x + 0.044715 * (x * x * x))))


def _shift_rows(x, shift, tail):
    row = lax.broadcasted_iota(jnp.int32, x.shape, 0)
    xs = pltpu.roll(x, shift, axis=0)
    for j in range(shift):
        src = SUBLANES - shift + j
        xs = jnp.where(row == j, tail[src:src + 1, :], xs)
    return xs


def _mm_kernel(a_ref, b_ref, o_ref):
    o_ref[...] = jnp.dot(a_ref[...], b_ref[...], preferred_element_type=F32).astype(o_ref.dtype)


def matmul(a, b, *, tm, tn, out_dtype=F32):
    m, k = a.shape
    _, n = b.shape
    return pl.pallas_call(
        _mm_kernel,
        grid=(m // tm, n // tn),
        in_specs=[pl.BlockSpec((tm, k), lambda i, j: (i, 0)),
                  pl.BlockSpec((k, tn), lambda i, j: (0, j))],
        out_specs=pl.BlockSpec((tm, tn), lambda i, j: (i, j)),
        out_shape=jax.ShapeDtypeStruct((m, n), out_dtype),
        compiler_params=_cparams("parallel", "parallel"),
        name="in_proj",
    )(a, b)


def _mm_ln_kernel(a_ref, w_ref, x_ref, g_ref, b_ref, o_ref, ob_ref, acc_ref):
    k = pl.program_id(1)

    @pl.when(k == 0)
    def _():
        acc_ref[...] = jnp.zeros_like(acc_ref)

    acc_ref[...] += jnp.dot(a_ref[...], w_ref[...], preferred_element_type=F32)

    @pl.when(k == pl.num_programs(1) - 1)
    def _():
        y = ALPHA * x_ref[...] + acc_ref[...]
        mu = jnp.mean(y, axis=-1, keepdims=True)
        yc = y - mu
        var = jnp.mean(yc * yc, axis=-1, keepdims=True)
        o = yc * lax.rsqrt(var + LN_EPS) * g_ref[...] + b_ref[...]
        o_ref[...] = o
        ob_ref[...] = o.astype(BF16)


def matmul_residual_layernorm(a, w, x, g, b, *, tm, tk, name):
    m, k = a.shape
    n = w.shape[1]
    return pl.pallas_call(
        _mm_ln_kernel,
        grid=(m // tm, k // tk),
        in_specs=[pl.BlockSpec((tm, tk), lambda i, kk: (i, kk)),
                  pl.BlockSpec((tk, n), lambda i, kk: (kk, 0)),
                  pl.BlockSpec((tm, n), lambda i, kk: (i, 0)),
                  pl.BlockSpec((1, n), lambda i, kk: (0, 0)),
                  pl.BlockSpec((1, n), lambda i, kk: (0, 0))],
        out_specs=[pl.BlockSpec((tm, n), lambda i, kk: (i, 0)),
                   pl.BlockSpec((tm, n), lambda i, kk: (i, 0))],
        out_shape=[jax.ShapeDtypeStruct((m, n), F32), jax.ShapeDtypeStruct((m, n), BF16)],
        scratch_shapes=[pltpu.VMEM((tm, n), F32)],
        compiler_params=_cparams("parallel", "arbitrary"),
        name=name,
    )(a, w, x, g.reshape(1, n), b.reshape(1, n))


def _rope_half(block, cs):
    p = block * cs
    return p + pltpu.roll(p, MLA_ROPE, axis=1)


def _q_up_kernel(p_ref, g_ref, w_ref, cs_ref, q_ref, *, rank):
    x = p_ref[...]
    ms = jnp.sum(x * x, axis=-1, keepdims=True) * (1.0 / rank)
    xn = (x * lax.rsqrt(ms + RMS_EPS) * g_ref[...]).astype(BF16)
    cs = cs_ref[...]
    hw = 2 * LANES
    for h in range(q_ref.shape[1]):
        acc = jnp.dot(xn, w_ref[:, h * hw:(h + 1) * hw], preferred_element_type=F32)
        rot = _rope_half(acc[:, LANES:hw], cs)
        q_ref[0, h, :, 0:MLA_NOPE] = acc[:, 0:MLA_NOPE].astype(BF16)
        q_ref[0, h, :, MLA_NOPE:MLA_NOPE + MLA_ROPE] = rot[:, 0:MLA_ROPE].astype(BF16)


def mla_q_up(proj, gain, w, cs, *, batch, seq, tm):
    nh = MLA_HEADS
    spt = seq // tm
    return pl.pallas_call(
        functools.partial(_q_up_kernel, rank=MLA_Q_RANK),
        grid=(batch * spt,),
        in_specs=[pl.BlockSpec((tm, Q_PAD), lambda i: (i, OFF_CQ // Q_PAD)),
                  pl.BlockSpec((1, Q_PAD), lambda i: (0, 0)),
                  pl.BlockSpec(w.shape, lambda i: (0, 0)),
                  pl.BlockSpec((tm, LANES), lambda i: (i, 0))],
        out_specs=pl.BlockSpec((1, nh, tm, MLA_NOPE + MLA_ROPE), lambda i: (i // spt, 0, i % spt, 0)),
        out_shape=jax.ShapeDtypeStruct((batch, nh, seq, MLA_NOPE + MLA_ROPE), BF16),
        compiler_params=_cparams("parallel"),
        name="mla_q_up",
    )(proj, gain, w, cs)


def _kv_up_kernel(p_ref, kr_ref, g_ref, w_ref, cs_ref, k_ref, v_ref, *, rank):
    x = p_ref[...]
    ms = jnp.sum(x * x, axis=-1, keepdims=True) * (1.0 / rank)
    xn = (x * lax.rsqrt(ms + RMS_EPS) * g_ref[...]).astype(BF16)
    krope = _rope_half(kr_ref[...], cs_ref[...])[:, 0:MLA_ROPE].astype(BF16)
    hw = MLA_NOPE + MLA_V
    for h in range(k_ref.shape[1]):
        acc = jnp.dot(xn, w_ref[:, h * hw:(h + 1) * hw], preferred_element_type=F32)
        k_ref[0, h, :, 0:MLA_NOPE] = acc[:, 0:MLA_NOPE].astype(BF16)
        k_ref[0, h, :, MLA_NOPE:MLA_NOPE + MLA_ROPE] = krope
        v_ref[0, h] = acc[:, MLA_NOPE:hw].astype(BF16)


def mla_kv_up(proj, gain, w, cs, *, batch, seq, tm):
    nh = MLA_HEADS
    spt = seq // tm
    dk = MLA_NOPE + MLA_ROPE
    return pl.pallas_call(
        functools.partial(_kv_up_kernel, rank=MLA_KV_RANK),
        grid=(batch * spt,),
        in_specs=[pl.BlockSpec((tm, MLA_KV_RANK), lambda i: (i, OFF_CKV // MLA_KV_RANK)),
                  pl.BlockSpec((tm, LANES), lambda i: (i, OFF_KR // LANES)),
                  pl.BlockSpec((1, MLA_KV_RANK), lambda i: (0, 0)),
                  pl.BlockSpec(w.shape, lambda i: (0, 0)),
                  pl.BlockSpec((tm, LANES), lambda i: (i, 0))],
        out_specs=[pl.BlockSpec((1, nh, tm, dk), lambda i: (i // spt, 0, i % spt, 0)),
                   pl.BlockSpec((1, nh, tm, MLA_V), lambda i: (i // spt, 0, i % spt, 0))],
        out_shape=[jax.ShapeDtypeStruct((batch, nh, seq, dk), BF16),
                   jax.ShapeDtypeStruct((batch, nh, seq, MLA_V), BF16)],
        compiler_params=_cparams("parallel"),
        name="mla_kv_up",
    )(proj, proj, gain, w, cs)


def _flash_kernel(q_ref, k_ref, v_ref, o_ref, *, tq, scale):
    qi = pl.program_id(2)
    q = q_ref[0, 0]

    def block(j, m, l, acc, diagonal):
        start = pl.multiple_of(j * tq, tq)
        k = k_ref[0, 0, pl.ds(start, tq), :]
        v = v_ref[0, 0, pl.ds(start, tq), :]
        s = lax.dot_general(q, k, (((1,), (1,)), ((), ())), preferred_element_type=F32) * scale
        if diagonal:
            shift = CHUNK.bit_length() - 1
            qc = lax.broadcasted_iota(jnp.int32, (tq, tq), 0) >> shift
            kc = lax.broadcasted_iota(jnp.int32, (tq, tq), 1) >> shift
            s = jnp.where(kc <= qc, s, -jnp.inf)
        m_new = jnp.maximum(m, jnp.max(s, axis=-1, keepdims=True))
        alpha = jnp.exp(m - m_new)
        p = jnp.exp(s - m_new)
        l = alpha * l + jnp.sum(p, axis=-1, keepdims=True)
        acc = alpha * acc + jnp.dot(p.astype(BF16), v, preferred_element_type=F32)
        return m_new, l, acc

    init = (jnp.full((tq, 1), -1e30, F32), jnp.zeros((tq, 1), F32), jnp.zeros((tq, MLA_V), F32))
    m, l, acc = lax.fori_loop(0, qi, lambda j, c: block(j, *c, False), init)
    m, l, acc = block(qi, m, l, acc, True)
    o_ref[0] = (acc / l).astype(o_ref.dtype)


def mla_attention(q, k, v, *, tq):
    batch, nh, seq, dk = q.shape
    scale = (MLA_NOPE + MLA_ROPE) ** -0.5
    return pl.pallas_call(
        functools.partial(_flash_kernel, tq=tq, scale=scale),
        grid=(batch, nh, seq // tq),
        in_specs=[pl.BlockSpec((1, 1, tq, dk), lambda b, h, i: (b, h, i, 0)),
                  pl.BlockSpec((1, 1, seq, dk), lambda b, h, i: (b, h, 0, 0)),
                  pl.BlockSpec((1, 1, seq, MLA_V), lambda b, h, i: (b, h, 0, 0))],
        out_specs=pl.BlockSpec((1, tq, MLA_V), lambda b, h, i: (b, i, h)),
        out_shape=jax.ShapeDtypeStruct((batch, seq, nh * MLA_V), BF16),
        compiler_params=_cparams("parallel", "parallel", "arbitrary"),
        name="mla_attention",
    )(q, k, v)


def _lru_kernel(x_ref, g_ref, cw_ref, cb_ref, wa_ref, ba_ref, wx_ref, bx_ref, lam_ref, o_ref,
                tail_ref, h_ref, a_scr, u_scr):
    @pl.when(pl.program_id(1) == 0)
    def _():
        tail_ref[...] = jnp.zeros_like(tail_ref)
        h_ref[...] = jnp.zeros_like(h_ref)

    x = x_ref[...]
    ts = x.shape[0]
    tail = tail_ref[...]
    xc = x * cw_ref[LRU_CONV - 1:LRU_CONV, :] + cb_ref[...]
    for sh in range(1, LRU_CONV):
        xc = xc + _shift_rows(x, sh, tail) * cw_ref[LRU_CONV - 1 - sh:LRU_CONV - sh, :]
    tail_ref[...] = x[ts - SUBLANES:ts, :]

    xb = xc.astype(BF16)
    ra, rx = [], []
    for n in range(LRU_BLOCKS):
        blk = xb[:, n * LRU_BLOCK:(n + 1) * LRU_BLOCK]
        ra.append(jnp.dot(blk, wa_ref[n], preferred_element_type=F32))
        rx.append(jnp.dot(blk, wx_ref[n], preferred_element_type=F32))
    r = jax.nn.sigmoid(jnp.concatenate(ra, axis=1) + ba_ref[...])
    gi = jax.nn.sigmoid(jnp.concatenate(rx, axis=1) + bx_ref[...])
    z = -lam_ref[...]
    softplus = jnp.maximum(z, 0.0) + jnp.log1p(jnp.exp(-jnp.abs(z)))
    log_a = -LRU_C * r * softplus
    a_scr[...] = jnp.exp(log_a)
    u_scr[...] = jnp.sqrt(1.0 - jnp.exp(2.0 * log_a)) * (gi * xc)

    row8 = lax.broadcasted_iota(jnp.int32, (SUBLANES, x.shape[1]), 0)

    def body(grp, h):
        base = pl.multiple_of(grp * SUBLANES, SUBLANES)
        a8 = a_scr[pl.ds(base, SUBLANES), :]
        u8 = u_scr[pl.ds(base, SUBLANES), :]
        hs = u8
        for s in range(SUBLANES):
            h = a8[s:s + 1, :] * h + u8[s:s + 1, :]
            hs = jnp.where(row8 == s, h, hs)
        u_scr[pl.ds(base, SUBLANES), :] = hs
        return h

    h_ref[...] = lax.fori_loop(0, ts // SUBLANES, body, h_ref[...])
    o_ref[...] = (u_scr[...] * _gelu_tanh(g_ref[...])).astype(o_ref.dtype)


def rglru_branch(proj, conv_w, conv_b, w_a, b_a, w_x, b_x, lam, *, batch, seq, ts):
    w = LRU_WIDTH
    spt = seq // ts
    vec = lambda: pl.BlockSpec((1, w), lambda b, s: (0, 0))
    blockdiag = lambda: pl.BlockSpec((LRU_BLOCKS, LRU_BLOCK, LRU_BLOCK), lambda b, s: (0, 0, 0))
    return pl.pallas_call(
        _lru_kernel,
        grid=(batch, spt),
        in_specs=[pl.BlockSpec((ts, w), lambda b, s: (b * spt + s, OFF_LRU_X // w)),
                  pl.BlockSpec((ts, w), lambda b, s: (b * spt + s, OFF_LRU_G // w)),
                  pl.BlockSpec((LRU_CONV, w), lambda b, s: (0, 0)),
                  vec(), blockdiag(), vec(), blockdiag(), vec(), vec()],
        out_specs=pl.BlockSpec((ts, w), lambda b, s: (b * spt + s, 0)),
        out_shape=jax.ShapeDtypeStruct((batch * seq, w), BF16),
        scratch_shapes=[pltpu.VMEM((SUBLANES, w), F32), pltpu.VMEM((1, w), F32),
                        pltpu.VMEM((ts, w), F32), pltpu.VMEM((ts, w), F32)],
        compiler_params=_cparams("parallel", "arbitrary"),
        name="rglru",
    )(proj, proj, conv_w, conv_b.reshape(1, w), w_a.astype(BF16), b_a.reshape(1, w),
      w_x.astype(BF16), b_x.reshape(1, w), lam.reshape(1, w))


def _rwkv_kernel(r_ref, k_ref, v_ref, kk_ref, b_ref, d_ref, ot_ref, st_ref):
    nb, tau, npair, _ = r_ref.shape
    hd = RW_HEAD

    @pl.when(pl.program_id(0) == 0)
    def _():
        st_ref[...] = jnp.zeros_like(st_ref)

    ot_ref[...] = jnp.zeros_like(ot_ref)

    row = lax.broadcasted_iota(jnp.int32, (hd, LANES), 0)
    lane = lax.broadcasted_iota(jnp.int32, (hd, LANES), 1)
    shift = hd.bit_length() - 1
    diag = (lane & (hd - 1)) == row
    kr = lax.broadcasted_iota(jnp.int32, (2 * LANES, LANES), 0)
    kc = lax.broadcasted_iota(jnp.int32, (2 * LANES, LANES), 1)
    ones_bd = jnp.where(((kr & (LANES - 1)) >> shift) == (kc >> shift), 1.0, 0.0).astype(BF16)

    def segsum(x):
        hi = x.astype(BF16)
        lo = (x - hi.astype(F32)).astype(BF16)
        return jnp.dot(jnp.concatenate([hi, lo], axis=1), ones_bd, preferred_element_type=F32)

    def step(t, carry):
        sel = (lane & (hd - 1)) == t
        for bi in range(nb):
            rt, kt, vt = r_ref[bi, t], k_ref[bi, t], v_ref[bi, t]
            kkt, bt, dt = kk_ref[bi, t], b_ref[bi, t], d_ref[bi, t]
            vec = lambda tile, p: tile[p:p + 1, :]
            sa = -segsum(jnp.concatenate(
                [st_ref[bi, p] * vec(kkt, p) for p in range(npair)], axis=0))
            vcb = segsum(jnp.concatenate(
                [jnp.where(diag, vec(vt, p), 0.0) for p in range(npair)], axis=0))
            outs = []
            for p in range(npair):
                rows = slice(p * hd, (p + 1) * hd)
                new = (st_ref[bi, p] * vec(dt, p) + sa[rows] * vec(bt, p)
                       + vcb[rows] * vec(kt, p))
                st_ref[bi, p] = new
                outs.append(new * vec(rt, p))
            o = segsum(jnp.concatenate(outs, axis=0))
            for p in range(npair):
                ot_ref[0, bi, p] = jnp.where(sel, o[p * hd:(p + 1) * hd], ot_ref[0, bi, p])
        return carry

    lax.fori_loop(0, tau, step, 0)


def rwkv_recurrence(r, k, v, kk, b, d):
    nb, seq, width = r.shape
    tau = RW_HEAD
    npair = width // LANES
    nblk = seq // tau
    spec = pl.BlockSpec((nb, tau, npair, LANES), lambda g: (0, g, 0, 0))
    tiles = [a.reshape(nb, seq, npair, LANES) for a in (r, k, v, kk, b, d)]
    ot = pl.pallas_call(
        _rwkv_kernel,
        grid=(nblk,),
        in_specs=[spec] * 6,
        out_specs=pl.BlockSpec((1, nb, npair, RW_HEAD, LANES), lambda g: (g, 0, 0, 0, 0)),
        out_shape=jax.ShapeDtypeStruct((nblk, nb, npair, RW_HEAD, LANES), F32),
        scratch_shapes=[pltpu.VMEM((nb, npair, RW_HEAD, LANES), F32)],
        compiler_params=_cparams("arbitrary"),
        name="rwkv_recurrence",
    )(*tiles)
    ot = ot.reshape(nblk, nb, npair, RW_HEAD, 2, tau)
    return jnp.transpose(ot, (1, 0, 5, 2, 4, 3)).reshape(nb, seq, width)


def _merge_kernel(a_ref, l_ref, r_ref, wa_ref, wl_ref, wr_ref, ga_ref, gl_ref, gr_ref, o_ref):
    acc = jax.nn.sigmoid(ga_ref[...]) * jnp.dot(a_ref[...], wa_ref[...], preferred_element_type=F32)
    acc += jax.nn.sigmoid(gl_ref[...]) * jnp.dot(l_ref[...], wl_ref[...], preferred_element_type=F32)
    acc += jax.nn.sigmoid(gr_ref[...]) * jnp.dot(r_ref[...], wr_ref[...], preferred_element_type=F32)
    o_ref[...] = acc.astype(o_ref.dtype)


def merge_branches(y_a, y_l, y_r, w_a, w_l, w_r, proj, *, tm, tn):
    m = y_a.shape[0]
    n = w_a.shape[1]
    nj = n // tn
    act = lambda width: pl.BlockSpec((tm, width), lambda i, j: (i, 0))
    wgt = lambda width: pl.BlockSpec((width, tn), lambda i, j: (0, j))
    gate = lambda g: pl.BlockSpec((tm, tn), lambda i, j: (i, g * nj + j))
    return pl.pallas_call(
        _merge_kernel,
        grid=(m // tm, nj),
        in_specs=[act(y_a.shape[1]), act(y_l.shape[1]), act(y_r.shape[1]),
                  wgt(w_a.shape[0]), wgt(w_l.shape[0]), wgt(w_r.shape[0]),
                  gate(0), gate(1), gate(2)],
        out_specs=pl.BlockSpec((tm, tn), lambda i, j: (i, j)),
        out_shape=jax.ShapeDtypeStruct((m, n), BF16),
        compiler_params=_cparams("parallel", "parallel"),
        name="merge_branches",
    )(y_a, y_l, y_r, w_a, w_l, w_r, proj, proj, proj)


def _ffn_up_kernel(x_ref, wg_ref, wv_ref, cw_ref, cb_ref, o_ref, halo_ref, *, tiles_per_seq):
    i = pl.program_id(0)
    j = pl.program_id(1)
    g = jnp.dot(x_ref[...], wg_ref[...], preferred_element_type=F32)
    v = jnp.dot(x_ref[...], wv_ref[...], preferred_element_type=F32)
    tm = g.shape[0]

    @pl.when(i % tiles_per_seq == 0)
    def _():
        halo_ref[j] = jnp.zeros(halo_ref.shape[1:], F32)

    tail = halo_ref[j]
    c = g * cw_ref[FFN_CONV - 1:FFN_CONV, :] + cb_ref[...]
    for sh in range(1, FFN_CONV):
        c = c + _shift_rows(g, sh, tail) * cw_ref[FFN_CONV - 1 - sh:FFN_CONV - sh, :]
    halo_ref[j] = g[tm - SUBLANES:tm, :]
    o_ref[...] = (_gelu_tanh(c) * v).astype(o_ref.dtype)


def ffn_up(x, w, conv_w, conv_b, *, seq, tm, tn):
    m, k = x.shape
    dff = conv_w.shape[1]
    nj = dff // tn
    return pl.pallas_call(
        functools.partial(_ffn_up_kernel, tiles_per_seq=seq // tm),
        grid=(m // tm, nj),
        in_specs=[pl.BlockSpec((tm, k), lambda i, j: (i, 0)),
                  pl.BlockSpec((k, tn), lambda i, j: (0, j)),
                  pl.BlockSpec((k, tn), lambda i, j: (0, nj + j)),
                  pl.BlockSpec((FFN_CONV, tn), lambda i, j: (0, j)),
                  pl.BlockSpec((1, tn), lambda i, j: (0, j))],
        out_specs=pl.BlockSpec((tm, tn), lambda i, j: (i, j)),
        out_shape=jax.ShapeDtypeStruct((m, dff), BF16),
        scratch_shapes=[pltpu.VMEM((nj, SUBLANES, tn), F32)],
        compiler_params=_cparams("arbitrary", "arbitrary"),
        name="ffn_up",
    )(x, w, w, conv_w, conv_b.reshape(1, dff))


def _rotate_half_cols(w):
    half = MLA_ROPE // 2
    return jnp.concatenate([-w[..., half:], w[..., :half]], axis=-1)


def _prep_w_in(w):
    o_ckv = MLA_Q_RANK
    o_kr = o_ckv + MLA_KV_RANK
    o_lx = o_kr + MLA_ROPE
    o_lg = o_lx + LRU_WIDTH
    o_rw = o_lg + LRU_WIDTH
    o_xw = o_rw + 3 * RW_WIDTH
    o_xg = o_xw + RW_DECAY_LORA + RW_AAA_LORA
    o_gt = o_xg + RW_GATE_LORA
    d = w.shape[0]
    kr = w[:, o_kr:o_lx]
    cols = [w[:, o_gt:], w[:, o_lx:o_rw], w[:, o_rw:o_xw],
            w[:, :o_ckv], jnp.zeros((d, Q_PAD - MLA_Q_RANK), w.dtype),
            w[:, o_ckv:o_kr], kr, _rotate_half_cols(kr),
            w[:, o_xw:o_xg], w[:, o_xg:o_gt],
            jnp.zeros((d, IN_COLS_PAD - OFF_XG - RW_GATE_LORA), w.dtype)]
    return jnp.concatenate(cols, axis=1).astype(BF16)


def _prep_w_uq(w):
    rope = w[..., MLA_NOPE:]
    w = jnp.concatenate([w, _rotate_half_cols(rope)], axis=-1)
    w = w.reshape(MLA_Q_RANK, MLA_HEADS * 2 * LANES)
    return jnp.pad(w, ((0, Q_PAD - MLA_Q_RANK), (0, 0))).astype(BF16)


def _prep_ffn(w_up, conv_w, conv_b, w_down):
    pad = D_FF_PAD - D_FF
    d = w_up.shape[0]
    z = jnp.zeros((d, pad), w_up.dtype)
    w_up = jnp.concatenate([w_up[:, :D_FF], z, w_up[:, D_FF:], z], axis=1).astype(BF16)
    return (w_up, jnp.pad(conv_w, ((0, 0), (0, pad))), jnp.pad(conv_b, (0, pad)),
            jnp.pad(w_down, ((0, pad), (0, 0))).astype(BF16))


def _rope_table(positions):
    inv_freq = ROPE_THETA ** (-jnp.arange(0, MLA_ROPE, 2, dtype=F32) / MLA_ROPE)
    ang = positions.astype(F32)[..., None] * inv_freq
    cos, sin = jnp.cos(ang), jnp.sin(ang)
    return jnp.concatenate([cos, cos, sin, sin], axis=-1).reshape(-1, 2 * MLA_ROPE)


def _token_shift(x):
    return jnp.pad(x, ((0, 0), (1, 0), (0, 0)))[:, :-1]


def _rwkv_branch(proj, batch, seq, mu, w0, w2, a0, a2, g2, k_k, k_a, r_k, gn_g, gn_b):
    p3 = proj.reshape(batch, seq, -1)
    segs = [(OFF_RW_R, 3 * RW_WIDTH, 0),
            (OFF_XWA, RW_DECAY_LORA + RW_AAA_LORA, 3 * RW_WIDTH),
            (OFF_XG, RW_GATE_LORA, 3 * RW_WIDTH + RW_DECAY_LORA + RW_AAA_LORA)]
    mixed = []
    for off, width, mu_off in segs:
        p = p3[:, :, off:off + width]
        mixed.append(p + mu[mu_off:mu_off + width] * (_token_shift(p) - p))
    rkv, xwa, xg = mixed
    r, k, v = rkv[..., :RW_WIDTH], rkv[..., RW_WIDTH:2 * RW_WIDTH], rkv[..., 2 * RW_WIDTH:]
    xw, xa = xwa[..., :RW_DECAY_LORA], xwa[..., RW_DECAY_LORA:]
    w = -jax.nn.softplus(-(w0 + jnp.tanh(xw) @ w2)) - 0.5
    a = jax.nn.sigmoid(a0 + xa @ a2)
    g = jax.nn.sigmoid(xg) @ g2
    heads = lambda t: t.reshape(batch, seq, RW_HEADS, RW_HEAD)
    kk = heads(k * k_k)
    kk = kk / jnp.maximum(jnp.linalg.norm(kk, axis=-1, keepdims=True), 1e-12)
    kk = kk.reshape(batch, seq, RW_WIDTH)
    k = k * (1 + (a - 1) * k_a)
    decay = jnp.exp(-jnp.exp(w))
    o = heads(rwkv_recurrence(r, k, v, kk, kk * a, decay))
    mean = jnp.mean(o, -1, keepdims=True)
    var = jnp.mean(jnp.square(o - mean), -1, keepdims=True)
    o = ((o - mean) * lax.rsqrt(var + GN_EPS)).reshape(batch, seq, RW_WIDTH) * gn_g + gn_b
    bonus = jnp.sum(heads(r) * heads(k) * r_k, -1, keepdims=True) * heads(v)
    o = o + bonus.reshape(batch, seq, RW_WIDTH)
    return (o * g).reshape(batch * seq, RW_WIDTH).astype(BF16)


class _Tiles:
    in_proj = (1024, 1024)
    mla_up_rows = 512
    attn_q = 512
    lru_rows = 512
    merge = (512, 1024)
    ln_rows = 256
    ln_k = 1024
    ffn_up = (1024, 512)


def kernel(x, positions, w_in, mla_q_norm, mla_w_uq, mla_kv_norm, mla_w_ukv, lru_conv_w, lru_conv_b, lru_w_a, lru_b_a, lru_w_x, lru_b_x, lru_lambda, rw_mu, rw_w0, rw_w2, rw_a0, rw_a2, rw_g2, rw_k_k, rw_k_a, rw_r_k, rw_gn_g, rw_gn_b, w_o_mla, w_o_lru, w_o_rwkv, w_out, ln1_g, ln1_b, ffn_w_up, ffn_conv_w, ffn_conv_b, ffn_w_down, ln2_g, ln2_b):
    batch, seq, d = x.shape
    m = batch * seq
    t = _Tiles
    cs = _rope_table(positions)
    xf = x.reshape(m, d)
    xb = xf.astype(BF16)
    for l in range(DEPTH):
        proj = matmul(xb, _prep_w_in(w_in[l]), tm=t.in_proj[0], tn=t.in_proj[1])

        q_gain = jnp.pad(mla_q_norm[l], (0, Q_PAD - MLA_Q_RANK)).reshape(1, Q_PAD)
        q = mla_q_up(proj, q_gain, _prep_w_uq(mla_w_uq[l]), cs, batch=batch, seq=seq, tm=t.mla_up_rows)
        w_ukv = mla_w_ukv[l].reshape(MLA_KV_RANK, -1).astype(BF16)
        k, v = mla_kv_up(proj, mla_kv_norm[l].reshape(1, -1), w_ukv, cs, batch=batch, seq=seq,
                         tm=t.mla_up_rows)
        y_a = mla_attention(q, k, v, tq=t.attn_q).reshape(m, -1)

        y_l = rglru_branch(proj, lru_conv_w[l], lru_conv_b[l], lru_w_a[l], lru_b_a[l],
                           lru_w_x[l], lru_b_x[l], lru_lambda[l], batch=batch, seq=seq, ts=t.lru_rows)

        y_r = _rwkv_branch(proj, batch, seq, rw_mu[l], rw_w0[l], rw_w2[l], rw_a0[l], rw_a2[l],
                           rw_g2[l], rw_k_k[l], rw_k_a[l], rw_r_k[l], rw_gn_g[l], rw_gn_b[l])

        merged = merge_branches(y_a, y_l, y_r, w_o_mla[l].astype(BF16), w_o_lru[l].astype(BF16),
                                w_o_rwkv[l].astype(BF16), proj, tm=t.merge[0], tn=t.merge[1])
        xf, xb = matmul_residual_layernorm(merged, w_out[l].astype(BF16), xf, ln1_g[l], ln1_b[l],
                                           tm=t.ln_rows, tk=t.ln_k, name="mixer_out_ln")

        w_up, conv_w, conv_b, w_down = _prep_ffn(ffn_w_up[l], ffn_conv_w[l], ffn_conv_b[l], ffn_w_down[l])
        h = ffn_up(xb, w_up, conv_w, conv_b, seq=seq, tm=t.ffn_up[0], tn=t.ffn_up[1])
        xf, xb = matmul_residual_layernorm(h, w_down, xf, ln2_g[l], ln2_b[l],
                                           tm=t.ln_rows, tk=t.ln_k, name="ffn_down_ln")
    return xf.reshape(batch, seq, d)
```

```python
import functools

import jax
import jax.numpy as jnp
from jax import lax
from jax.experimental import pallas as pl
from jax.experimental.pallas import tpu as pltpu

F32 = jnp.float32
BF16 = jnp.bfloat16

D_MODEL = 4096
DEPTH = 2
CHUNK = 64
MLA_HEADS = 16
MLA_Q_RANK = 896
MLA_KV_RANK = 512
MLA_NOPE = 128
MLA_ROPE = 64
MLA_V = 128
ROPE_THETA = 10000.0
LRU_WIDTH = 1024
LRU_BLOCKS = 8
LRU_BLOCK = LRU_WIDTH // LRU_BLOCKS
LRU_CONV = 4
LRU_C = 8.0
RW_WIDTH = 1024
RW_HEAD = 64
RW_HEADS = RW_WIDTH // RW_HEAD
RW_DECAY_LORA = 64
RW_AAA_LORA = 64
RW_GATE_LORA = 160
D_FF = 11008
FFN_CONV = 3
ALPHA = (2 * DEPTH) ** 0.25
LN_EPS = 1e-5
RMS_EPS = 1e-6
GN_EPS = 64e-5

LANES = 128
SUBLANES = 8
VMEM_LIMIT_BYTES = 56 * 1024 * 1024

Q_PAD = 1024
D_FF_PAD = 11264
XG_PAD = 256
OFF_GATES = 0
OFF_LRU_X = 3 * D_MODEL
OFF_LRU_G = OFF_LRU_X + LRU_WIDTH
OFF_RW_R = OFF_LRU_G + LRU_WIDTH
OFF_RW_K = OFF_RW_R + RW_WIDTH
OFF_RW_V = OFF_RW_K + RW_WIDTH
OFF_CQ = OFF_RW_V + RW_WIDTH
OFF_CKV = OFF_CQ + Q_PAD
OFF_KR = OFF_CKV + MLA_KV_RANK
OFF_XWA = OFF_KR + 2 * MLA_ROPE
OFF_XG = OFF_XWA + RW_DECAY_LORA + RW_AAA_LORA
IN_COLS_PAD = OFF_XG + XG_PAD

RW_GROUP_PAIRS = 4

_SQRT_2_OVER_PI = 0.7978845608028654


def _cparams(*sem):
    return pltpu.CompilerParams(dimension_semantics=sem, vmem_limit_bytes=VMEM_LIMIT_BYTES)


def _gelu_tanh(x):
    return 0.5 * x * (1.0 + jnp.tanh(_SQRT_2_OVER_PI * (x + 0.044715 * (x * x * x))))


def _softplus(z):
    return jnp.maximum(z, 0.0) + jnp.log1p(jnp.exp(-jnp.abs(z)))


def _shift_rows(x, shift, tail):
    row = lax.broadcasted_iota(jnp.int32, x.shape, 0)
    xs = pltpu.roll(x, shift, axis=0)
    for j in range(shift):
        src = SUBLANES - shift + j
        xs = jnp.where(row == j, tail[src:src + 1, :], xs)
    return xs


def _group_ones(rows, cols, group):
    shift = group.bit_length() - 1
    r = lax.broadcasted_iota(jnp.int32, (rows, cols), 0)
    c = lax.broadcasted_iota(jnp.int32, (rows, cols), 1)
    return jnp.where(((r & (cols - 1)) >> shift) == (c >> shift), 1.0, 0.0).astype(BF16)


def _head_sum(x, ones_hl):
    hi = x.astype(BF16)
    lo = (x - hi.astype(F32)).astype(BF16)
    return jnp.dot(jnp.concatenate([hi, lo], axis=1), ones_hl, preferred_element_type=F32)


def _mm_kernel(a_ref, b_ref, o_ref):
    o_ref[...] = jnp.dot(a_ref[...], b_ref[...], preferred_element_type=F32).astype(o_ref.dtype)


def matmul(a, b, *, tm, tn, out_dtype=F32):
    m, k = a.shape
    _, n = b.shape
    return pl.pallas_call(
        _mm_kernel,
        grid=(m // tm, n // tn),
        in_specs=[pl.BlockSpec((tm, k), lambda i, j: (i, 0)),
                  pl.BlockSpec((k, tn), lambda i, j: (0, j))],
        out_specs=pl.BlockSpec((tm, tn), lambda i, j: (i, j)),
        out_shape=jax.ShapeDtypeStruct((m, n), out_dtype),
        compiler_params=_cparams("parallel", "parallel"),
        name="in_proj",
    )(a, b)


def _mm_ln_kernel(a_ref, w_ref, x_ref, g_ref, b_ref, o_ref, ob_ref):
    k = pl.program_id(1)

    @pl.when(k == 0)
    def _():
        o_ref[...] = jnp.zeros_like(o_ref)

    o_ref[...] += jnp.dot(a_ref[...], w_ref[...], preferred_element_type=F32)

    @pl.when(k == pl.num_programs(1) - 1)
    def _():
        y = ALPHA * x_ref[...] + o_ref[...]
        mu = jnp.mean(y, axis=-1, keepdims=True)
        yc = y - mu
        var = jnp.mean(yc * yc, axis=-1, keepdims=True)
        o = yc * lax.rsqrt(var + LN_EPS) * g_ref[...] + b_ref[...]
        o_ref[...] = o
        ob_ref[...] = o.astype(BF16)


def matmul_residual_layernorm(a, w, x, g, b, *, tm, tk, name):
    m, k = a.shape
    n = w.shape[1]
    return pl.pallas_call(
        _mm_ln_kernel,
        grid=(m // tm, k // tk),
        in_specs=[pl.BlockSpec((tm, tk), lambda i, kk: (i, kk)),
                  pl.BlockSpec((tk, n), lambda i, kk: (kk, 0)),
                  pl.BlockSpec((tm, n), lambda i, kk: (i, 0), pipeline_mode=pl.Buffered(1)),
                  pl.BlockSpec((1, n), lambda i, kk: (0, 0)),
                  pl.BlockSpec((1, n), lambda i, kk: (0, 0))],
        out_specs=[pl.BlockSpec((tm, n), lambda i, kk: (i, 0)),
                   pl.BlockSpec((tm, n), lambda i, kk: (i, 0))],
        out_shape=[jax.ShapeDtypeStruct((m, n), F32), jax.ShapeDtypeStruct((m, n), BF16)],
        compiler_params=_cparams("parallel", "arbitrary"),
        name=name,
    )(a, w, x, g.reshape(1, n), b.reshape(1, n))


def _rope_half(block, cs):
    p = block * cs
    return p + pltpu.roll(p, MLA_ROPE, axis=1)


def _q_up_kernel(p_ref, g_ref, w_ref, cs_ref, q_ref, *, rank):
    x = p_ref[...]
    ms = jnp.sum(x * x, axis=-1, keepdims=True) * (1.0 / rank)
    xn = (x * lax.rsqrt(ms + RMS_EPS) * g_ref[...]).astype(BF16)
    cs = cs_ref[...]
    hw = 2 * LANES
    for h in range(q_ref.shape[1]):
        acc = jnp.dot(xn, w_ref[:, h * hw:(h + 1) * hw], preferred_element_type=F32)
        rot = _rope_half(acc[:, LANES:hw], cs)
        q_ref[0, h, :, 0:MLA_NOPE] = acc[:, 0:MLA_NOPE].astype(BF16)
        q_ref[0, h, :, MLA_NOPE:MLA_NOPE + MLA_ROPE] = rot[:, 0:MLA_ROPE].astype(BF16)


def mla_q_up(proj, gain, w, cs, *, batch, seq, tm):
    nh = MLA_HEADS
    spt = seq // tm
    return pl.pallas_call(
        functools.partial(_q_up_kernel, rank=MLA_Q_RANK),
        grid=(batch * spt,),
        in_specs=[pl.BlockSpec((tm, Q_PAD), lambda i: (i, OFF_CQ // Q_PAD)),
                  pl.BlockSpec((1, Q_PAD), lambda i: (0, 0)),
                  pl.BlockSpec(w.shape, lambda i: (0, 0)),
                  pl.BlockSpec((tm, LANES), lambda i: (i, 0))],
        out_specs=pl.BlockSpec((1, nh, tm, MLA_NOPE + MLA_ROPE), lambda i: (i // spt, 0, i % spt, 0)),
        out_shape=jax.ShapeDtypeStruct((batch, nh, seq, MLA_NOPE + MLA_ROPE), BF16),
        compiler_params=_cparams("parallel"),
        name="mla_q_up",
    )(proj, gain, w, cs)


def _kv_up_kernel(p_ref, kr_ref, g_ref, w_ref, cs_ref, k_ref, v_ref, *, rank):
    x = p_ref[...]
    ms = jnp.sum(x * x, axis=-1, keepdims=True) * (1.0 / rank)
    xn = (x * lax.rsqrt(ms + RMS_EPS) * g_ref[...]).astype(BF16)
    krope = _rope_half(kr_ref[...], cs_ref[...])[:, 0:MLA_ROPE].astype(BF16)
    hw = MLA_NOPE + MLA_V
    for h in range(k_ref.shape[1]):
        acc = jnp.dot(xn, w_ref[:, h * hw:(h + 1) * hw], preferred_element_type=F32)
        k_ref[0, h, :, 0:MLA_NOPE] = acc[:, 0:MLA_NOPE].astype(BF16)
        k_ref[0, h, :, MLA_NOPE:MLA_NOPE + MLA_ROPE] = krope
        v_ref[0, h] = acc[:, MLA_NOPE:hw].astype(BF16)


def mla_kv_up(proj, gain, w, cs, *, batch, seq, tm):
    nh = MLA_HEADS
    spt = seq // tm
    dk = MLA_NOPE + MLA_ROPE
    return pl.pallas_call(
        functools.partial(_kv_up_kernel, rank=MLA_KV_RANK),
        grid=(batch * spt,),
        in_specs=[pl.BlockSpec((tm, MLA_KV_RANK), lambda i: (i, OFF_CKV // MLA_KV_RANK)),
                  pl.BlockSpec((tm, LANES), lambda i: (i, OFF_KR // LANES)),
                  pl.BlockSpec((1, MLA_KV_RANK), lambda i: (0, 0)),
                  pl.BlockSpec(w.shape, lambda i: (0, 0)),
                  pl.BlockSpec((tm, LANES), lambda i: (i, 0))],
        out_specs=[pl.BlockSpec((1, nh, tm, dk), lambda i: (i // spt, 0, i % spt, 0)),
                   pl.BlockSpec((1, nh, tm, MLA_V), lambda i: (i // spt, 0, i % spt, 0))],
        out_shape=[jax.ShapeDtypeStruct((batch, nh, seq, dk), BF16),
                   jax.ShapeDtypeStruct((batch, nh, seq, MLA_V), BF16)],
        compiler_params=_cparams("parallel"),
        name="mla_kv_up",
    )(proj, proj, gain, w, cs)


def _flash_kernel(q_ref, k_ref, v_ref, o_ref, *, tq, scale):
    qi = pl.program_id(2)
    q = q_ref[0, 0]

    def block(j, m, l, acc, diagonal):
        start = pl.multiple_of(j * tq, tq)
        k = k_ref[0, 0, pl.ds(start, tq), :]
        v = v_ref[0, 0, pl.ds(start, tq), :]
        s = lax.dot_general(q, k, (((1,), (1,)), ((), ())), preferred_element_type=F32) * scale
        if diagonal:
            shift = CHUNK.bit_length() - 1
            qc = lax.broadcasted_iota(jnp.int32, (tq, tq), 0) >> shift
            kc = lax.broadcasted_iota(jnp.int32, (tq, tq), 1) >> shift
            s = jnp.where(kc <= qc, s, -jnp.inf)
        m_new = jnp.maximum(m, jnp.max(s, axis=-1, keepdims=True))
        alpha = jnp.exp(m - m_new)
        p = jnp.exp(s - m_new)
        l = alpha * l + jnp.sum(p, axis=-1, keepdims=True)
        acc = alpha * acc + jnp.dot(p.astype(BF16), v, preferred_element_type=F32)
        return m_new, l, acc

    init = (jnp.full((tq, 1), -1e30, F32), jnp.zeros((tq, 1), F32), jnp.zeros((tq, MLA_V), F32))
    m, l, acc = lax.fori_loop(0, qi, lambda j, c: block(j, *c, False), init)
    m, l, acc = block(qi, m, l, acc, True)
    o_ref[0] = (acc / l).astype(o_ref.dtype)


def mla_attention(q, k, v, *, tq):
    batch, nh, seq, dk = q.shape
    scale = (MLA_NOPE + MLA_ROPE) ** -0.5
    return pl.pallas_call(
        functools.partial(_flash_kernel, tq=tq, scale=scale),
        grid=(batch, nh, seq // tq),
        in_specs=[pl.BlockSpec((1, 1, tq, dk), lambda b, h, i: (b, h, i, 0)),
                  pl.BlockSpec((1, 1, seq, dk), lambda b, h, i: (b, h, 0, 0)),
                  pl.BlockSpec((1, 1, seq, MLA_V), lambda b, h, i: (b, h, 0, 0))],
        out_specs=pl.BlockSpec((1, tq, MLA_V), lambda b, h, i: (b, i, h)),
        out_shape=jax.ShapeDtypeStruct((batch, seq, nh * MLA_V), BF16),
        compiler_params=_cparams("parallel", "parallel", "arbitrary"),
        name="mla_attention",
    )(q, k, v)


def _lru_kernel(x_ref, g_ref, cw_ref, cb_ref, wa_ref, ba_ref, wx_ref, bx_ref, lam_ref, o_ref,
                tail_ref, h_ref, a_scr, u_scr):
    @pl.when(pl.program_id(1) == 0)
    def _():
        tail_ref[...] = jnp.zeros_like(tail_ref)
        h_ref[...] = jnp.zeros_like(h_ref)

    x = x_ref[...]
    ts = x.shape[0]
    tail = tail_ref[...]
    xc = x * cw_ref[LRU_CONV - 1:LRU_CONV, :] + cb_ref[...]
    for sh in range(1, LRU_CONV):
        xc = xc + _shift_rows(x, sh, tail) * cw_ref[LRU_CONV - 1 - sh:LRU_CONV - sh, :]
    tail_ref[...] = x[ts - SUBLANES:ts, :]

    xb = xc.astype(BF16)
    ra, rx = [], []
    for n in range(LRU_BLOCKS):
        blk = xb[:, n * LRU_BLOCK:(n + 1) * LRU_BLOCK]
        ra.append(jnp.dot(blk, wa_ref[n], preferred_element_type=F32))
        rx.append(jnp.dot(blk, wx_ref[n], preferred_element_type=F32))
    r = jax.nn.sigmoid(jnp.concatenate(ra, axis=1) + ba_ref[...])
    gi = jax.nn.sigmoid(jnp.concatenate(rx, axis=1) + bx_ref[...])
    log_a = -LRU_C * r * _softplus(-lam_ref[...])
    a_scr[...] = jnp.exp(log_a)
    u_scr[...] = jnp.sqrt(1.0 - jnp.exp(2.0 * log_a)) * (gi * xc)

    row8 = lax.broadcasted_iota(jnp.int32, (SUBLANES, x.shape[1]), 0)

    def body(grp, h):
        base = pl.multiple_of(grp * SUBLANES, SUBLANES)
        a8 = a_scr[pl.ds(base, SUBLANES), :]
        u8 = u_scr[pl.ds(base, SUBLANES), :]
        hs = u8
        for s in range(SUBLANES):
            h = a8[s:s + 1, :] * h + u8[s:s + 1, :]
            hs = jnp.where(row8 == s, h, hs)
        u_scr[pl.ds(base, SUBLANES), :] = hs
        return h

    h_ref[...] = lax.fori_loop(0, ts // SUBLANES, body, h_ref[...])
    o_ref[...] = (u_scr[...] * _gelu_tanh(g_ref[...])).astype(o_ref.dtype)


def rglru_branch(proj, conv_w, conv_b, w_a, b_a, w_x, b_x, lam, *, batch, seq, ts):
    w = LRU_WIDTH
    spt = seq // ts
    vec = lambda: pl.BlockSpec((1, w), lambda b, s: (0, 0))
    blockdiag = lambda: pl.BlockSpec((LRU_BLOCKS, LRU_BLOCK, LRU_BLOCK), lambda b, s: (0, 0, 0))
    return pl.pallas_call(
        _lru_kernel,
        grid=(batch, spt),
        in_specs=[pl.BlockSpec((ts, w), lambda b, s: (b * spt + s, OFF_LRU_X // w)),
                  pl.BlockSpec((ts, w), lambda b, s: (b * spt + s, OFF_LRU_G // w)),
                  pl.BlockSpec((LRU_CONV, w), lambda b, s: (0, 0)),
                  vec(), blockdiag(), vec(), blockdiag(), vec(), vec()],
        out_specs=pl.BlockSpec((ts, w), lambda b, s: (b * spt + s, 0)),
        out_shape=jax.ShapeDtypeStruct((batch * seq, w), BF16),
        scratch_shapes=[pltpu.VMEM((SUBLANES, w), F32), pltpu.VMEM((1, w), F32),
                        pltpu.VMEM((ts, w), F32), pltpu.VMEM((ts, w), F32)],
        compiler_params=_cparams("parallel", "arbitrary"),
        name="rglru",
    )(proj, proj, conv_w, conv_b.reshape(1, w), w_a.astype(BF16), b_a.reshape(1, w),
      w_x.astype(BF16), b_x.reshape(1, w), lam.reshape(1, w))


def _rwkv_prep_kernel(r_ref, k_ref, v_ref, wa_ref, xg_ref,
                      mur_ref, muk_ref, muv_ref, muwa_ref, mug_ref,
                      w0_ref, w2_ref, a0_ref, a2_ref, g2_ref, kkw_ref, ka_ref,
                      ro_ref, ko_ref, vo_ref, nkk_ref, b_ref, d_ref, g_ref,
                      tr_ref, tk_ref, tv_ref, twa_ref, tg_ref):
    @pl.when(pl.program_id(1) == 0)
    def _():
        for t in (tr_ref, tk_ref, tv_ref, twa_ref, tg_ref):
            t[...] = jnp.zeros_like(t)

    def mix(x_ref, mu_ref, tail_ref):
        x = x_ref[...]
        prev = _shift_rows(x, 1, tail_ref[...])
        tail_ref[...] = x[x.shape[0] - SUBLANES:, :]
        return x + mu_ref[...] * (prev - x)

    r = mix(r_ref, mur_ref, tr_ref)
    k = mix(k_ref, muk_ref, tk_ref)
    v = mix(v_ref, muv_ref, tv_ref)
    xwa = mix(wa_ref, muwa_ref, twa_ref)
    xg = mix(xg_ref, mug_ref, tg_ref)
    w = -_softplus(-(w0_ref[...] + jnp.dot(jnp.tanh(xwa).astype(BF16), w2_ref[...],
                                           preferred_element_type=F32))) - 0.5
    a = jax.nn.sigmoid(a0_ref[...] + jnp.dot(xwa.astype(BF16), a2_ref[...], preferred_element_type=F32))
    g = jnp.dot(jax.nn.sigmoid(xg).astype(BF16), g2_ref[...], preferred_element_type=F32)

    kk = k * kkw_ref[...]
    ones_hl = _group_ones(2 * LANES, LANES, RW_HEAD)
    sq = kk * kk
    ss = jnp.concatenate([_head_sum(sq[:, c * LANES:(c + 1) * LANES], ones_hl)
                          for c in range(RW_WIDTH // LANES)], axis=1)
    kk = kk / jnp.maximum(jnp.sqrt(ss), 1e-12)

    ro_ref[...] = r
    ko_ref[...] = k * (1.0 + (a - 1.0) * ka_ref[...])
    vo_ref[...] = v
    nkk_ref[...] = -kk
    b_ref[...] = kk * a
    d_ref[...] = jnp.exp(-jnp.exp(w))
    g_ref[...] = g


def rwkv_prep(proj, mu, w0, w2, a0, a2, g2, k_k, k_a, *, batch, seq, ts):
    w = RW_WIDTH
    spt = seq // ts
    lora = RW_DECAY_LORA + RW_AAA_LORA
    row = lambda width, col: pl.BlockSpec((ts, width), lambda b, s: (b * spt + s, col))
    vec = lambda width: pl.BlockSpec((1, width), lambda b, s: (0, 0))
    mat = lambda rows: pl.BlockSpec((rows, w), lambda b, s: (0, 0))
    mu_r, mu_k, mu_v = (mu[i * w:(i + 1) * w].reshape(1, w) for i in range(3))
    mu_wa = mu[3 * w:3 * w + lora].reshape(1, lora)
    mu_g = jnp.pad(mu[3 * w + lora:], (0, XG_PAD - RW_GATE_LORA)).reshape(1, XG_PAD)
    zeros = jnp.zeros((RW_DECAY_LORA, w), w2.dtype)
    w2p = jnp.concatenate([w2, zeros], axis=0).astype(BF16)
    a2p = jnp.concatenate([zeros, a2], axis=0).astype(BF16)
    g2p = jnp.pad(g2, ((0, XG_PAD - RW_GATE_LORA), (0, 0))).astype(BF16)
    out = jax.ShapeDtypeStruct((batch * seq, w), F32)
    return pl.pallas_call(
        _rwkv_prep_kernel,
        grid=(batch, spt),
        in_specs=[row(w, OFF_RW_R // w), row(w, OFF_RW_K // w), row(w, OFF_RW_V // w),
                  row(lora, OFF_XWA // lora), row(XG_PAD, OFF_XG // XG_PAD),
                  vec(w), vec(w), vec(w), vec(lora), vec(XG_PAD),
                  vec(w), mat(lora), vec(w), mat(lora), mat(XG_PAD), vec(w), vec(w)],
        out_specs=[pl.BlockSpec((ts, w), lambda b, s: (b * spt + s, 0))] * 7,
        out_shape=[out] * 7,
        scratch_shapes=[pltpu.VMEM((SUBLANES, w), F32)] * 3
                       + [pltpu.VMEM((SUBLANES, lora), F32), pltpu.VMEM((SUBLANES, XG_PAD), F32)],
        compiler_params=_cparams("parallel", "arbitrary"),
        name="rwkv_prep",
    )(proj, proj, proj, proj, proj, mu_r, mu_k, mu_v, mu_wa, mu_g,
      w0.reshape(1, w), w2p, a0.reshape(1, w), a2p, g2p, k_k.reshape(1, w), k_a.reshape(1, w))


def _rwkv_rec_kernel(r_ref, k_ref, v_ref, nkk_ref, b_ref, d_ref, g_ref, rk_ref, gng_ref, gnb_ref,
                     y_ref, st_ref, q_ref, ot_ref):
    nb, tau, width = r_ref.shape
    npair = width // LANES
    hd = RW_HEAD
    pairs = [(bi, p) for bi in range(nb) for p in range(npair)]
    flat = [(i, bi, p) for i, (bi, p) in enumerate(pairs)]
    groups = [flat[j:j + RW_GROUP_PAIRS] for j in range(0, len(flat), RW_GROUP_PAIRS)]

    @pl.when(pl.program_id(0) == 0)
    def _():
        st_ref[...] = jnp.zeros_like(st_ref)

    q_ref[...] = jnp.zeros_like(q_ref)
    ot_ref[...] = jnp.zeros_like(ot_ref)

    row = lax.broadcasted_iota(jnp.int32, (hd, LANES), 0)
    lane = lax.broadcasted_iota(jnp.int32, (hd, LANES), 1)
    lane_in_head = lane & (hd - 1)
    diag = lane_in_head == row
    ones2 = _group_ones(2 * LANES, 2 * LANES, hd)

    def group(grp, carry):
        base = pl.multiple_of(grp * SUBLANES, SUBLANES)
        tile = lambda ref, bi, p: ref[bi, pl.ds(base, SUBLANES), pl.ds(p * LANES, LANES)]
        vcb = [None] * len(groups)
        for s in range(SUBLANES):
            sel = lane_in_head == (base + s - 1)
            for gi, grp_pairs in enumerate(groups):
                lhs = []
                for i, bi, p in grp_pairs:
                    pm = (st_ref[bi, p] * tile(nkk_ref, bi, p)[s:s + 1, :]).astype(BF16)
                    lhs.append(jnp.concatenate([pm, q_ref[pl.ds(i * hd, hd), :]], axis=1))
                if s % 2 == 0:
                    for i, bi, p in grp_pairs:
                        v8 = tile(v_ref, bi, p)
                        lhs.append(jnp.concatenate(
                            [jnp.where(diag, v8[s:s + 1, :], 0.0).astype(BF16),
                             jnp.where(diag, v8[s + 1:s + 2, :], 0.0).astype(BF16)], axis=1))
                out = jnp.dot(jnp.concatenate(lhs, axis=0), ones2, preferred_element_type=F32)
                if s % 2 == 0:
                    vcb[gi] = out[len(grp_pairs) * hd:, :]
                for j, (i, bi, p) in enumerate(grp_pairs):
                    rows = slice(j * hd, (j + 1) * hd)
                    ot_ref[bi, p] = jnp.where(sel, out[rows, LANES:], ot_ref[bi, p])
                    vc = vcb[gi][rows, 0:LANES] if s % 2 == 0 else vcb[gi][rows, LANES:]
                    new = (st_ref[bi, p] * tile(d_ref, bi, p)[s:s + 1, :]
                           + out[rows, 0:LANES] * tile(b_ref, bi, p)[s:s + 1, :]
                           + vc * tile(k_ref, bi, p)[s:s + 1, :])
                    st_ref[bi, p] = new
                    q_ref[pl.ds(i * hd, hd), :] = (new * tile(r_ref, bi, p)[s:s + 1, :]).astype(BF16)
        return carry

    lax.fori_loop(0, tau // SUBLANES, group, 0)

    o_last = jnp.dot(q_ref[...], ones2[0:LANES, 0:LANES], preferred_element_type=F32)
    sel = lane_in_head == (tau - 1)
    for i, (bi, p) in enumerate(pairs):
        ot_ref[bi, p] = jnp.where(sel, o_last[i * hd:(i + 1) * hd, :], ot_ref[bi, p])

    ones_hl = _group_ones(2 * LANES, LANES, hd)
    first_head = lane < hd
    inv_hd = 1.0 / hd
    for bi in range(nb):
        for p0 in range(0, npair, 2):
            tr = jnp.concatenate([ot_ref[bi, p0], ot_ref[bi, p0 + 1]], axis=0).T
            top, bot = tr[0:hd, :], tr[hd:2 * hd, :]
            nat = (jnp.where(first_head, top, pltpu.roll(bot, hd, axis=1)),
                   jnp.where(first_head, pltpu.roll(top, hd, axis=1), bot))
            for p, o in zip((p0, p0 + 1), nat):
                cols = pl.ds(p * LANES, LANES)
                mean = _head_sum(o, ones_hl) * inv_hd
                oc = o - mean
                var = _head_sum(oc * oc, ones_hl) * inv_hd
                on = oc * lax.rsqrt(var + GN_EPS) * gng_ref[:, cols] + gnb_ref[:, cols]
                bonus = _head_sum(r_ref[bi, :, cols] * k_ref[bi, :, cols] * rk_ref[:, cols], ones_hl)
                y = (on + bonus * v_ref[bi, :, cols]) * g_ref[bi, :, cols]
                y_ref[bi, :, cols] = y.astype(y_ref.dtype)


def rwkv_recurrence(r, k, v, nkk, b, d, g, r_k, gn_g, gn_b, *, batch, seq):
    width = RW_WIDTH
    tau = RW_HEAD
    npair = width // LANES
    spec = pl.BlockSpec((batch, tau, width), lambda i: (0, i, 0))
    vec = pl.BlockSpec((1, width), lambda i: (0, 0))
    acts = [a.reshape(batch, seq, width) for a in (r, k, v, nkk, b, d, g)]
    return pl.pallas_call(
        _rwkv_rec_kernel,
        grid=(seq // tau,),
        in_specs=[spec] * 7 + [vec] * 3,
        out_specs=spec,
        out_shape=jax.ShapeDtypeStruct((batch, seq, width), BF16),
        scratch_shapes=[pltpu.VMEM((batch, npair, RW_HEAD, LANES), F32),
                        pltpu.VMEM((batch * npair * RW_HEAD, LANES), BF16),
                        pltpu.VMEM((batch, npair, RW_HEAD, LANES), F32)],
        compiler_params=_cparams("arbitrary"),
        name="rwkv_recurrence",
    )(*acts, r_k.reshape(1, width), gn_g.reshape(1, width), gn_b.reshape(1, width))


def _merge_kernel(a_ref, l_ref, r_ref, wa_ref, wl_ref, wr_ref, ga_ref, gl_ref, gr_ref, o_ref):
    acc = jax.nn.sigmoid(ga_ref[...]) * jnp.dot(a_ref[...], wa_ref[...], preferred_element_type=F32)
    acc += jax.nn.sigmoid(gl_ref[...]) * jnp.dot(l_ref[...], wl_ref[...], preferred_element_type=F32)
    acc += jax.nn.sigmoid(gr_ref[...]) * jnp.dot(r_ref[...], wr_ref[...], preferred_element_type=F32)
    o_ref[...] = acc.astype(o_ref.dtype)


def merge_branches(y_a, y_l, y_r, w_a, w_l, w_r, proj, *, tm, tn):
    m = y_a.shape[0]
    n = w_a.shape[1]
    nj = n // tn
    act = lambda width: pl.BlockSpec((tm, width), lambda i, j: (i, 0))
    wgt = lambda width: pl.BlockSpec((width, tn), lambda i, j: (0, j))
    gate = lambda g: pl.BlockSpec((tm, tn), lambda i, j: (i, g * nj + j))
    return pl.pallas_call(
        _merge_kernel,
        grid=(m // tm, nj),
        in_specs=[act(y_a.shape[1]), act(y_l.shape[1]), act(y_r.shape[1]),
                  wgt(w_a.shape[0]), wgt(w_l.shape[0]), wgt(w_r.shape[0]),
                  gate(0), gate(1), gate(2)],
        out_specs=pl.BlockSpec((tm, tn), lambda i, j: (i, j)),
        out_shape=jax.ShapeDtypeStruct((m, n), BF16),
        compiler_params=_cparams("parallel", "parallel"),
        name="merge_branches",
    )(y_a, y_l, y_r, w_a, w_l, w_r, proj, proj, proj)


def _ffn_up_kernel(x_ref, wg_ref, wv_ref, cw_ref, cb_ref, o_ref, halo_ref, *, tiles_per_seq):
    i = pl.program_id(0)
    j = pl.program_id(1)
    g = jnp.dot(x_ref[...], wg_ref[...], preferred_element_type=F32)
    v = jnp.dot(x_ref[...], wv_ref[...], preferred_element_type=F32)
    tm = g.shape[0]

    @pl.when(i % tiles_per_seq == 0)
    def _():
        halo_ref[j] = jnp.zeros(halo_ref.shape[1:], F32)

    tail = halo_ref[j]
    c = g * cw_ref[FFN_CONV - 1:FFN_CONV, :] + cb_ref[...]
    for sh in range(1, FFN_CONV):
        c = c + _shift_rows(g, sh, tail) * cw_ref[FFN_CONV - 1 - sh:FFN_CONV - sh, :]
    halo_ref[j] = g[tm - SUBLANES:tm, :]
    o_ref[...] = (_gelu_tanh(c) * v).astype(o_ref.dtype)


def ffn_up(x, w, conv_w, conv_b, *, seq, tm, tn):
    m, k = x.shape
    dff = conv_w.shape[1]
    nj = dff // tn
    return pl.pallas_call(
        functools.partial(_ffn_up_kernel, tiles_per_seq=seq // tm),
        grid=(m // tm, nj),
        in_specs=[pl.BlockSpec((tm, k), lambda i, j: (i, 0)),
                  pl.BlockSpec((k, tn), lambda i, j: (0, j)),
                  pl.BlockSpec((k, tn), lambda i, j: (0, nj + j)),
                  pl.BlockSpec((FFN_CONV, tn), lambda i, j: (0, j)),
                  pl.BlockSpec((1, tn), lambda i, j: (0, j))],
        out_specs=pl.BlockSpec((tm, tn), lambda i, j: (i, j)),
        out_shape=jax.ShapeDtypeStruct((m, dff), BF16),
        scratch_shapes=[pltpu.VMEM((nj, SUBLANES, tn), F32)],
        compiler_params=_cparams("arbitrary", "arbitrary"),
        name="ffn_up",
    )(x, w, w, conv_w, conv_b.reshape(1, dff))


def _rotate_half_cols(w):
    half = MLA_ROPE // 2
    return jnp.concatenate([-w[..., half:], w[..., :half]], axis=-1)


def _prep_w_in(w):
    o_ckv = MLA_Q_RANK
    o_kr = o_ckv + MLA_KV_RANK
    o_lx = o_kr + MLA_ROPE
    o_lg = o_lx + LRU_WIDTH
    o_rw = o_lg + LRU_WIDTH
    o_xw = o_rw + 3 * RW_WIDTH
    o_xg = o_xw + RW_DECAY_LORA + RW_AAA_LORA
    o_gt = o_xg + RW_GATE_LORA
    d = w.shape[0]
    kr = w[:, o_kr:o_lx]
    cols = [w[:, o_gt:], w[:, o_lx:o_rw], w[:, o_rw:o_xw],
            w[:, :o_ckv], jnp.zeros((d, Q_PAD - MLA_Q_RANK), w.dtype),
            w[:, o_ckv:o_kr], kr, _rotate_half_cols(kr),
            w[:, o_xw:o_xg], w[:, o_xg:o_gt],
            jnp.zeros((d, XG_PAD - RW_GATE_LORA), w.dtype)]
    return jnp.concatenate(cols, axis=1).astype(BF16)


def _prep_w_uq(w):
    rope = w[..., MLA_NOPE:]
    w = jnp.concatenate([w, _rotate_half_cols(rope)], axis=-1)
    w = w.reshape(MLA_Q_RANK, MLA_HEADS * 2 * LANES)
    return jnp.pad(w, ((0, Q_PAD - MLA_Q_RANK), (0, 0))).astype(BF16)


def _prep_ffn(w_up, conv_w, conv_b, w_down):
    pad = D_FF_PAD - D_FF
    d = w_up.shape[0]
    z = jnp.zeros((d, pad), w_up.dtype)
    w_up = jnp.concatenate([w_up[:, :D_FF], z, w_up[:, D_FF:], z], axis=1).astype(BF16)
    return (w_up, jnp.pad(conv_w, ((0, 0), (0, pad))), jnp.pad(conv_b, (0, pad)),
            jnp.pad(w_down, ((0, pad), (0, 0))).astype(BF16))


def _rope_table(positions):
    inv_freq = ROPE_THETA ** (-jnp.arange(0, MLA_ROPE, 2, dtype=F32) / MLA_ROPE)
    ang = positions.astype(F32)[..., None] * inv_freq
    cos, sin = jnp.cos(ang), jnp.sin(ang)
    return jnp.concatenate([cos, cos, sin, sin], axis=-1).reshape(-1, 2 * MLA_ROPE)


class _Tiles:
    in_proj = (1024, 1024)
    mla_up_rows = 512
    attn_q = 512
    lru_rows = 512
    rwkv_prep_rows = 256
    merge = (512, 1024)
    ln_rows = 512
    ln_k = 512
    ffn_up = (1024, 512)


def kernel(x, positions, w_in, mla_q_norm, mla_w_uq, mla_kv_norm, mla_w_ukv, lru_conv_w, lru_conv_b, lru_w_a, lru_b_a, lru_w_x, lru_b_x, lru_lambda, rw_mu, rw_w0, rw_w2, rw_a0, rw_a2, rw_g2, rw_k_k, rw_k_a, rw_r_k, rw_gn_g, rw_gn_b, w_o_mla, w_o_lru, w_o_rwkv, w_out, ln1_g, ln1_b, ffn_w_up, ffn_conv_w, ffn_conv_b, ffn_w_down, ln2_g, ln2_b):
    batch, seq, d = x.shape
    m = batch * seq
    t = _Tiles
    cs = _rope_table(positions)
    xf = x.reshape(m, d)
    xb = xf.astype(BF16)
    for l in range(DEPTH):
        proj = matmul(xb, _prep_w_in(w_in[l]), tm=t.in_proj[0], tn=t.in_proj[1])

        q_gain = jnp.pad(mla_q_norm[l], (0, Q_PAD - MLA_Q_RANK)).reshape(1, Q_PAD)
        q = mla_q_up(proj, q_gain, _prep_w_uq(mla_w_uq[l]), cs, batch=batch, seq=seq, tm=t.mla_up_rows)
        w_ukv = mla_w_ukv[l].reshape(MLA_KV_RANK, -1).astype(BF16)
        k, v = mla_kv_up(proj, mla_kv_norm[l].reshape(1, -1), w_ukv, cs, batch=batch, seq=seq,
                         tm=t.mla_up_rows)
        y_a = mla_attention(q, k, v, tq=t.attn_q).reshape(m, -1)

        y_l = rglru_branch(proj, lru_conv_w[l], lru_conv_b[l], lru_w_a[l], lru_b_a[l],
                           lru_w_x[l], lru_b_x[l], lru_lambda[l], batch=batch, seq=seq, ts=t.lru_rows)

        rw = rwkv_prep(proj, rw_mu[l], rw_w0[l], rw_w2[l], rw_a0[l], rw_a2[l], rw_g2[l],
                       rw_k_k[l], rw_k_a[l], batch=batch, seq=seq, ts=t.rwkv_prep_rows)
        y_r = rwkv_recurrence(*rw, rw_r_k[l], rw_gn_g[l], rw_gn_b[l], batch=batch, seq=seq)
        y_r = y_r.reshape(m, RW_WIDTH)

        merged = merge_branches(y_a, y_l, y_r, w_o_mla[l].astype(BF16), w_o_lru[l].astype(BF16),
                                w_o_rwkv[l].astype(BF16), proj, tm=t.merge[0], tn=t.merge[1])
        xf, xb = matmul_residual_layernorm(merged, w_out[l].astype(BF16), xf, ln1_g[l], ln1_b[l],
                                           tm=t.ln_rows, tk=t.ln_k, name="mixer_out_ln")

        w_up, conv_w, conv_b, w_down = _prep_ffn(ffn_w_up[l], ffn_conv_w[l], ffn_conv_b[l], ffn_w_down[l])
        h = ffn_up(xb, w_up, conv_w, conv_b, seq=seq, tm=t.ffn_up[0], tn=t.ffn_up[1])
        xf, xb = matmul_residual_layernorm(h, w_down, xf, ln2_g[l], ln2_b[l],
                                           tm=t.ln_rows, tk=t.ln_k, name="ffn_down_ln")
    return xf.reshape(batch, seq, d)
```

```python
import functools

import jax
import jax.numpy as jnp
from jax import lax
from jax.experimental import pallas as pl
from jax.experimental.pallas import tpu as pltpu

F32 = jnp.float32
BF16 = jnp.bfloat16

D_MODEL = 4096
DEPTH = 2
CHUNK = 64
MLA_HEADS = 16
MLA_Q_RANK = 896
MLA_KV_RANK = 512
MLA_NOPE = 128
MLA_ROPE = 64
MLA_V = 128
ROPE_THETA = 10000.0
LRU_WIDTH = 1024
LRU_BLOCKS = 8
LRU_BLOCK = LRU_WIDTH // LRU_BLOCKS
LRU_CONV = 4
LRU_C = 8.0
RW_WIDTH = 1024
RW_HEAD = 64
RW_HEADS = RW_WIDTH // RW_HEAD
RW_DECAY_LORA = 64
RW_AAA_LORA = 64
RW_GATE_LORA = 160
D_FF = 11008
FFN_CONV = 3
ALPHA = (2 * DEPTH) ** 0.25
LN_EPS = 1e-5
RMS_EPS = 1e-6
GN_EPS = 64e-5

LANES = 128
SUBLANES = 8
VMEM_LIMIT_BYTES = 56 * 1024 * 1024

Q_PAD = 1024
D_FF_PAD = 11264
XG_PAD = 256
OFF_GATES = 0
OFF_LRU_X = 3 * D_MODEL
OFF_LRU_G = OFF_LRU_X + LRU_WIDTH
OFF_RW_R = OFF_LRU_G + LRU_WIDTH
OFF_RW_K = OFF_RW_R + RW_WIDTH
OFF_RW_V = OFF_RW_K + RW_WIDTH
OFF_CQ = OFF_RW_V + RW_WIDTH
OFF_CKV = OFF_CQ + Q_PAD
OFF_KR = OFF_CKV + MLA_KV_RANK
OFF_XWA = OFF_KR + 2 * MLA_ROPE
OFF_XG = OFF_XWA + RW_DECAY_LORA + RW_AAA_LORA
IN_COLS_PAD = OFF_XG + XG_PAD

RW_GROUP_PAIRS = 4
_SQRT_2_OVER_PI = 0.7978845608028654
_LOG2_E = 1.4426950408889634


def _cparams(*sem):
    return pltpu.CompilerParams(dimension_semantics=sem, vmem_limit_bytes=VMEM_LIMIT_BYTES)


def _gelu_tanh(x):
    return 0.5 * x * (1.0 + jnp.tanh(_SQRT_2_OVER_PI * (x + 0.044715 * (x * x * x))))


def _softplus(z):
    return jnp.maximum(z, 0.0) + jnp.log1p(jnp.exp(-jnp.abs(z)))


def _shift_rows(x, shift, tail):
    row = lax.broadcasted_iota(jnp.int32, x.shape, 0)
    xs = pltpu.roll(x, shift, axis=0)
    for j in range(shift):
        src = SUBLANES - shift + j
        xs = jnp.where(row == j, tail[src:src + 1, :], xs)
    return xs


def _group_ones(rows, cols, group):
    shift = group.bit_length() - 1
    r = lax.broadcasted_iota(jnp.int32, (rows, cols), 0)
    c = lax.broadcasted_iota(jnp.int32, (rows, cols), 1)
    return jnp.where(((r & (cols - 1)) >> shift) == (c >> shift), 1.0, 0.0).astype(BF16)


def _head_sum(x, ones_hl):
    hi = x.astype(BF16)
    lo = (x - hi.astype(F32)).astype(BF16)
    return jnp.dot(jnp.concatenate([hi, lo], axis=1), ones_hl, preferred_element_type=F32)


def _mm_kernel(a_ref, b_ref, o_ref):
    o_ref[...] = jnp.dot(a_ref[...], b_ref[...], preferred_element_type=F32).astype(o_ref.dtype)


def matmul(a, b, *, tm, tn, out_dtype=F32):
    m, k = a.shape
    _, n = b.shape
    return pl.pallas_call(
        _mm_kernel,
        grid=(m // tm, n // tn),
        in_specs=[pl.BlockSpec((tm, k), lambda i, j: (i, 0)),
                  pl.BlockSpec((k, tn), lambda i, j: (0, j))],
        out_specs=pl.BlockSpec((tm, tn), lambda i, j: (i, j)),
        out_shape=jax.ShapeDtypeStruct((m, n), out_dtype),
        compiler_params=_cparams("parallel", "parallel"),
        name="in_proj",
    )(a, b)


def _mm_res_kernel(a_ref, w_ref, x_ref, o_ref, acc_ref):
    k = pl.program_id(2)

    @pl.when(k == 0)
    def _():
        acc_ref[...] = ALPHA * x_ref[...]

    acc_ref[...] += jnp.dot(a_ref[...], w_ref[...], preferred_element_type=F32)

    @pl.when(k == pl.num_programs(2) - 1)
    def _():
        o_ref[...] = acc_ref[...]


def _ln_kernel(y_ref, g_ref, b_ref, o_ref, ob_ref):
    y = y_ref[...]
    mu = jnp.mean(y, axis=-1, keepdims=True)
    yc = y - mu
    var = jnp.mean(yc * yc, axis=-1, keepdims=True)
    o = yc * lax.rsqrt(var + LN_EPS) * g_ref[...] + b_ref[...]
    o_ref[...] = o
    ob_ref[...] = o.astype(BF16)


def matmul_residual_layernorm(a, w, x, g, b, *, tm, tn, tk, ln_rows, name):
    m, k = a.shape
    n = w.shape[1]
    y = pl.pallas_call(
        _mm_res_kernel,
        grid=(m // tm, n // tn, k // tk),
        in_specs=[pl.BlockSpec((tm, tk), lambda i, j, kk: (i, kk)),
                  pl.BlockSpec((tk, tn), lambda i, j, kk: (kk, j)),
                  pl.BlockSpec((tm, tn), lambda i, j, kk: (i, j))],
        out_specs=pl.BlockSpec((tm, tn), lambda i, j, kk: (i, j)),
        out_shape=jax.ShapeDtypeStruct((m, n), F32),
        scratch_shapes=[pltpu.VMEM((tm, tn), F32)],
        compiler_params=_cparams("parallel", "parallel", "arbitrary"),
        name=name,
    )(a, w, x)
    return pl.pallas_call(
        _ln_kernel,
        grid=(m // ln_rows,),
        in_specs=[pl.BlockSpec((ln_rows, n), lambda i: (i, 0)),
                  pl.BlockSpec((1, n), lambda i: (0, 0)),
                  pl.BlockSpec((1, n), lambda i: (0, 0))],
        out_specs=[pl.BlockSpec((ln_rows, n), lambda i: (i, 0)),
                   pl.BlockSpec((ln_rows, n), lambda i: (i, 0))],
        out_shape=[jax.ShapeDtypeStruct((m, n), F32), jax.ShapeDtypeStruct((m, n), BF16)],
        compiler_params=_cparams("parallel"),
        name="layernorm",
    )(y, g.reshape(1, n), b.reshape(1, n))


def _rope_half(block, cs):
    p = block * cs
    return p + pltpu.roll(p, MLA_ROPE, axis=1)


def _q_up_kernel(p_ref, g_ref, w_ref, cs_ref, q_ref, *, rank):
    x = p_ref[...]
    ms = jnp.sum(x * x, axis=-1, keepdims=True) * (1.0 / rank)
    xn = (x * lax.rsqrt(ms + RMS_EPS) * g_ref[...]).astype(BF16)
    cs = cs_ref[...]
    hw = 2 * LANES
    for h in range(q_ref.shape[1]):
        acc = jnp.dot(xn, w_ref[:, h * hw:(h + 1) * hw], preferred_element_type=F32)
        rot = _rope_half(acc[:, LANES:hw], cs)
        q_ref[0, h, :, 0:MLA_NOPE] = acc[:, 0:MLA_NOPE].astype(BF16)
        q_ref[0, h, :, MLA_NOPE:MLA_NOPE + MLA_ROPE] = rot[:, 0:MLA_ROPE].astype(BF16)


def mla_q_up(proj, gain, w, cs, *, batch, seq, tm):
    nh = MLA_HEADS
    spt = seq // tm
    return pl.pallas_call(
        functools.partial(_q_up_kernel, rank=MLA_Q_RANK),
        grid=(batch * spt,),
        in_specs=[pl.BlockSpec((tm, Q_PAD), lambda i: (i, OFF_CQ // Q_PAD)),
                  pl.BlockSpec((1, Q_PAD), lambda i: (0, 0)),
                  pl.BlockSpec(w.shape, lambda i: (0, 0)),
                  pl.BlockSpec((tm, LANES), lambda i: (i, 0))],
        out_specs=pl.BlockSpec((1, nh, tm, MLA_NOPE + MLA_ROPE), lambda i: (i // spt, 0, i % spt, 0)),
        out_shape=jax.ShapeDtypeStruct((batch, nh, seq, MLA_NOPE + MLA_ROPE), BF16),
        compiler_params=_cparams("parallel"),
        name="mla_q_up",
    )(proj, gain, w, cs)


def _kv_up_kernel(p_ref, kr_ref, g_ref, w_ref, cs_ref, k_ref, v_ref, *, rank):
    x = p_ref[...]
    ms = jnp.sum(x * x, axis=-1, keepdims=True) * (1.0 / rank)
    xn = (x * lax.rsqrt(ms + RMS_EPS) * g_ref[...]).astype(BF16)
    krope = _rope_half(kr_ref[...], cs_ref[...])[:, 0:MLA_ROPE].astype(BF16)
    hw = MLA_NOPE + MLA_V
    for h in range(k_ref.shape[1]):
        acc = jnp.dot(xn, w_ref[:, h * hw:(h + 1) * hw], preferred_element_type=F32)
        k_ref[0, h, :, 0:MLA_NOPE] = acc[:, 0:MLA_NOPE].astype(BF16)
        k_ref[0, h, :, MLA_NOPE:MLA_NOPE + MLA_ROPE] = krope
        v_ref[0, h] = acc[:, MLA_NOPE:hw].astype(BF16)


def mla_kv_up(proj, gain, w, cs, *, batch, seq, tm):
    nh = MLA_HEADS
    spt = seq // tm
    dk = MLA_NOPE + MLA_ROPE
    return pl.pallas_call(
        functools.partial(_kv_up_kernel, rank=MLA_KV_RANK),
        grid=(batch * spt,),
        in_specs=[pl.BlockSpec((tm, MLA_KV_RANK), lambda i: (i, OFF_CKV // MLA_KV_RANK)),
                  pl.BlockSpec((tm, LANES), lambda i: (i, OFF_KR // LANES)),
                  pl.BlockSpec((1, MLA_KV_RANK), lambda i: (0, 0)),
                  pl.BlockSpec(w.shape, lambda i: (0, 0)),
                  pl.BlockSpec((tm, LANES), lambda i: (i, 0))],
        out_specs=[pl.BlockSpec((1, nh, tm, dk), lambda i: (i // spt, 0, i % spt, 0)),
                   pl.BlockSpec((1, nh, tm, MLA_V), lambda i: (i // spt, 0, i % spt, 0))],
        out_shape=[jax.ShapeDtypeStruct((batch, nh, seq, dk), BF16),
                   jax.ShapeDtypeStruct((batch, nh, seq, MLA_V), BF16)],
        compiler_params=_cparams("parallel"),
        name="mla_kv_up",
    )(proj, proj, gain, w, cs)


def _flash_kernel(q_ref, k_ref, v_ref, o_ref, *, tq, scale):
    qi = pl.program_id(2)
    q = q_ref[0, 0]

    def scores(j):
        k = k_ref[0, 0, pl.ds(pl.multiple_of(j * tq, tq), tq), :]
        return lax.dot_general(q, k, (((1,), (1,)), ((), ())),
                               preferred_element_type=F32) * (scale * _LOG2_E)

    def update(j, s, m, l, acc):
        v = v_ref[0, 0, pl.ds(pl.multiple_of(j * tq, tq), tq), :]
        m_new = jnp.maximum(m, jnp.max(s, axis=-1, keepdims=True))
        alpha = jnp.exp2(m - m_new)
        p = jnp.exp2(s - m_new)
        l = alpha * l + jnp.sum(p, axis=-1, keepdims=True)
        acc = alpha * acc + jnp.dot(p.astype(BF16), v, preferred_element_type=F32)
        return m_new, l, acc

    def pair(jj, carry):
        s_a, s_b = scores(2 * jj), scores(2 * jj + 1)
        return update(2 * jj + 1, s_b, *update(2 * jj, s_a, *carry))

    def single(_, carry):
        return update(qi - 1, scores(qi - 1), *carry)

    carry = (jnp.full((tq, 1), -1e30, F32), jnp.zeros((tq, 1), F32), jnp.zeros((tq, MLA_V), F32))
    carry = lax.fori_loop(0, qi >> 1, pair, carry)
    carry = lax.fori_loop(0, qi & 1, single, carry)
    shift = CHUNK.bit_length() - 1
    qc = lax.broadcasted_iota(jnp.int32, (tq, tq), 0) >> shift
    kc = lax.broadcasted_iota(jnp.int32, (tq, tq), 1) >> shift
    m, l, acc = update(qi, jnp.where(kc <= qc, scores(qi), -jnp.inf), *carry)
    o_ref[0] = (acc / l).astype(o_ref.dtype)


def mla_attention(q, k, v, *, tq):
    batch, nh, seq, dk = q.shape
    scale = (MLA_NOPE + MLA_ROPE) ** -0.5
    return pl.pallas_call(
        functools.partial(_flash_kernel, tq=tq, scale=scale),
        grid=(batch, nh, seq // tq),
        in_specs=[pl.BlockSpec((1, 1, tq, dk), lambda b, h, i: (b, h, i, 0)),
                  pl.BlockSpec((1, 1, seq, dk), lambda b, h, i: (b, h, 0, 0)),
                  pl.BlockSpec((1, 1, seq, MLA_V), lambda b, h, i: (b, h, 0, 0))],
        out_specs=pl.BlockSpec((1, tq, MLA_V), lambda b, h, i: (b, i, h)),
        out_shape=jax.ShapeDtypeStruct((batch, seq, nh * MLA_V), BF16),
        compiler_params=_cparams("parallel", "parallel", "arbitrary"),
        name="mla_attention",
    )(q, k, v)


def _lru_kernel(x_ref, g_ref, cw_ref, cb_ref, wa_ref, ba_ref, wx_ref, bx_ref, lam_ref, o_ref,
                tail_ref, h_ref, a_scr, u_scr):
    @pl.when(pl.program_id(1) == 0)
    def _():
        tail_ref[...] = jnp.zeros_like(tail_ref)
        h_ref[...] = jnp.zeros_like(h_ref)

    x = x_ref[...]
    ts = x.shape[0]
    tail = tail_ref[...]
    xc = x * cw_ref[LRU_CONV - 1:LRU_CONV, :] + cb_ref[...]
    for sh in range(1, LRU_CONV):
        xc = xc + _shift_rows(x, sh, tail) * cw_ref[LRU_CONV - 1 - sh:LRU_CONV - sh, :]
    tail_ref[...] = x[ts - SUBLANES:ts, :]

    xb = xc.astype(BF16)
    ra, rx = [], []
    for n in range(LRU_BLOCKS):
        blk = xb[:, n * LRU_BLOCK:(n + 1) * LRU_BLOCK]
        ra.append(jnp.dot(blk, wa_ref[n], preferred_element_type=F32))
        rx.append(jnp.dot(blk, wx_ref[n], preferred_element_type=F32))
    r = jax.nn.sigmoid(jnp.concatenate(ra, axis=1) + ba_ref[...])
    gi = jax.nn.sigmoid(jnp.concatenate(rx, axis=1) + bx_ref[...])
    log_a = -LRU_C * r * _softplus(-lam_ref[...])
    a_scr[...] = jnp.exp(log_a)
    u_scr[...] = jnp.sqrt(1.0 - jnp.exp(2.0 * log_a)) * (gi * xc)

    row8 = lax.broadcasted_iota(jnp.int32, (SUBLANES, x.shape[1]), 0)

    def body(grp, h):
        base = pl.multiple_of(grp * SUBLANES, SUBLANES)
        a8 = a_scr[pl.ds(base, SUBLANES), :]
        u8 = u_scr[pl.ds(base, SUBLANES), :]
        hs = u8
        for s in range(SUBLANES):
            h = a8[s:s + 1, :] * h + u8[s:s + 1, :]
            hs = jnp.where(row8 == s, h, hs)
        u_scr[pl.ds(base, SUBLANES), :] = hs
        return h

    h_ref[...] = lax.fori_loop(0, ts // SUBLANES, body, h_ref[...])
    o_ref[...] = (u_scr[...] * _gelu_tanh(g_ref[...])).astype(o_ref.dtype)


def rglru_branch(proj, conv_w, conv_b, w_a, b_a, w_x, b_x, lam, *, batch, seq, ts):
    w = LRU_WIDTH
    spt = seq // ts
    vec = lambda: pl.BlockSpec((1, w), lambda b, s: (0, 0))
    blockdiag = lambda: pl.BlockSpec((LRU_BLOCKS, LRU_BLOCK, LRU_BLOCK), lambda b, s: (0, 0, 0))
    return pl.pallas_call(
        _lru_kernel,
        grid=(batch, spt),
        in_specs=[pl.BlockSpec((ts, w), lambda b, s: (b * spt + s, OFF_LRU_X // w)),
                  pl.BlockSpec((ts, w), lambda b, s: (b * spt + s, OFF_LRU_G // w)),
                  pl.BlockSpec((LRU_CONV, w), lambda b, s: (0, 0)),
                  vec(), blockdiag(), vec(), blockdiag(), vec(), vec()],
        out_specs=pl.BlockSpec((ts, w), lambda b, s: (b * spt + s, 0)),
        out_shape=jax.ShapeDtypeStruct((batch * seq, w), BF16),
        scratch_shapes=[pltpu.VMEM((SUBLANES, w), F32), pltpu.VMEM((1, w), F32),
                        pltpu.VMEM((ts, w), F32), pltpu.VMEM((ts, w), F32)],
        compiler_params=_cparams("parallel", "arbitrary"),
        name="rglru",
    )(proj, proj, conv_w, conv_b.reshape(1, w), w_a.astype(BF16), b_a.reshape(1, w),
      w_x.astype(BF16), b_x.reshape(1, w), lam.reshape(1, w))


def _rwkv_prep_kernel(r_ref, k_ref, v_ref, wa_ref, xg_ref,
                      mur_ref, muk_ref, muv_ref, muwa_ref, mug_ref,
                      w0_ref, w2_ref, a0_ref, a2_ref, g2_ref, kkw_ref, ka_ref,
                      ro_ref, ko_ref, vo_ref, nkk_ref, b_ref, d_ref, g_ref,
                      tr_ref, tk_ref, tv_ref, twa_ref, tg_ref):
    @pl.when(pl.program_id(1) == 0)
    def _():
        for t in (tr_ref, tk_ref, tv_ref, twa_ref, tg_ref):
            t[...] = jnp.zeros_like(t)

    def mix(x_ref, mu_ref, tail_ref):
        x = x_ref[...]
        prev = _shift_rows(x, 1, tail_ref[...])
        tail_ref[...] = x[x.shape[0] - SUBLANES:, :]
        return x + mu_ref[...] * (prev - x)

    r = mix(r_ref, mur_ref, tr_ref)
    k = mix(k_ref, muk_ref, tk_ref)
    v = mix(v_ref, muv_ref, tv_ref)
    xwa = mix(wa_ref, muwa_ref, twa_ref)
    xg = mix(xg_ref, mug_ref, tg_ref)
    w = -_softplus(-(w0_ref[...] + jnp.dot(jnp.tanh(xwa).astype(BF16), w2_ref[...],
                                           preferred_element_type=F32))) - 0.5
    a = jax.nn.sigmoid(a0_ref[...] + jnp.dot(xwa.astype(BF16), a2_ref[...], preferred_element_type=F32))
    g = jnp.dot(jax.nn.sigmoid(xg).astype(BF16), g2_ref[...], preferred_element_type=F32)

    kk = k * kkw_ref[...]
    ones_hl = _group_ones(2 * LANES, LANES, RW_HEAD)
    sq = kk * kk
    ss = jnp.concatenate([_head_sum(sq[:, c * LANES:(c + 1) * LANES], ones_hl)
                          for c in range(RW_WIDTH // LANES)], axis=1)
    kk = kk / jnp.maximum(jnp.sqrt(ss), 1e-12)

    ro_ref[...] = r
    ko_ref[...] = k * (1.0 + (a - 1.0) * ka_ref[...])
    vo_ref[...] = v
    nkk_ref[...] = -kk
    b_ref[...] = kk * a
    d_ref[...] = jnp.exp(-jnp.exp(w))
    g_ref[...] = g


def rwkv_prep(proj, mu, w0, w2, a0, a2, g2, k_k, k_a, *, batch, seq, ts):
    w = RW_WIDTH
    spt = seq // ts
    lora = RW_DECAY_LORA + RW_AAA_LORA
    row = lambda width, col: pl.BlockSpec((ts, width), lambda b, s: (b * spt + s, col))
    vec = lambda width: pl.BlockSpec((1, width), lambda b, s: (0, 0))
    mat = lambda rows: pl.BlockSpec((rows, w), lambda b, s: (0, 0))
    mu_r, mu_k, mu_v = (mu[i * w:(i + 1) * w].reshape(1, w) for i in range(3))
    mu_wa = mu[3 * w:3 * w + lora].reshape(1, lora)
    mu_g = jnp.pad(mu[3 * w + lora:], (0, XG_PAD - RW_GATE_LORA)).reshape(1, XG_PAD)
    zeros = jnp.zeros((RW_DECAY_LORA, w), w2.dtype)
    w2p = jnp.concatenate([w2, zeros], axis=0).astype(BF16)
    a2p = jnp.concatenate([zeros, a2], axis=0).astype(BF16)
    g2p = jnp.pad(g2, ((0, XG_PAD - RW_GATE_LORA), (0, 0))).astype(BF16)
    out = jax.ShapeDtypeStruct((batch * seq, w), F32)
    return pl.pallas_call(
        _rwkv_prep_kernel,
        grid=(batch, spt),
        in_specs=[row(w, OFF_RW_R // w), row(w, OFF_RW_K // w), row(w, OFF_RW_V // w),
                  row(lora, OFF_XWA // lora), row(XG_PAD, OFF_XG // XG_PAD),
                  vec(w), vec(w), vec(w), vec(lora), vec(XG_PAD),
                  vec(w), mat(lora), vec(w), mat(lora), mat(XG_PAD), vec(w), vec(w)],
        out_specs=[pl.BlockSpec((ts, w), lambda b, s: (b * spt + s, 0))] * 7,
        out_shape=[out] * 7,
        scratch_shapes=[pltpu.VMEM((SUBLANES, w), F32)] * 3
                       + [pltpu.VMEM((SUBLANES, lora), F32), pltpu.VMEM((SUBLANES, XG_PAD), F32)],
        compiler_params=_cparams("parallel", "arbitrary"),
        name="rwkv_prep",
    )(proj, proj, proj, proj, proj, mu_r, mu_k, mu_v, mu_wa, mu_g,
      w0.reshape(1, w), w2p, a0.reshape(1, w), a2p, g2p, k_k.reshape(1, w), k_a.reshape(1, w))


def _rwkv_rec_kernel(r_ref, k_ref, v_ref, nkk_ref, b_ref, d_ref, g_ref, rk_ref, gng_ref, gnb_ref,
                     y_ref, st_ref, q_ref, ot_ref):
    nb, tau, width = r_ref.shape
    npair = width // LANES
    hd = RW_HEAD
    pairs = [(bi, p) for bi in range(nb) for p in range(npair)]
    flat = [(i, bi, p) for i, (bi, p) in enumerate(pairs)]
    groups = [flat[j:j + RW_GROUP_PAIRS] for j in range(0, len(flat), RW_GROUP_PAIRS)]

    @pl.when(pl.program_id(0) == 0)
    def _():
        st_ref[...] = jnp.zeros_like(st_ref)

    q_ref[...] = jnp.zeros_like(q_ref)
    ot_ref[...] = jnp.zeros_like(ot_ref)

    row = lax.broadcasted_iota(jnp.int32, (hd, LANES), 0)
    lane = lax.broadcasted_iota(jnp.int32, (hd, LANES), 1)
    lane_in_head = lane & (hd - 1)
    diag = lane_in_head == row
    ones2 = _group_ones(2 * LANES, 2 * LANES, hd)

    def group(grp, carry):
        base = pl.multiple_of(grp * SUBLANES, SUBLANES)
        tile = lambda ref, bi, p: ref[bi, pl.ds(base, SUBLANES), pl.ds(p * LANES, LANES)]
        vcb = [None] * len(groups)
        for s in range(SUBLANES):
            sel = lane_in_head == (base + s - 1)
            for gi, grp_pairs in enumerate(groups):
                lhs = []
                for i, bi, p in grp_pairs:
                    pm = (st_ref[bi, p] * tile(nkk_ref, bi, p)[s:s + 1, :]).astype(BF16)
                    lhs.append(jnp.concatenate([pm, q_ref[pl.ds(i * hd, hd), :]], axis=1))
                if s % 2 == 0:
                    for i, bi, p in grp_pairs:
                        v8 = tile(v_ref, bi, p)
                        lhs.append(jnp.concatenate(
                            [jnp.where(diag, v8[s:s + 1, :], 0.0).astype(BF16),
                             jnp.where(diag, v8[s + 1:s + 2, :], 0.0).astype(BF16)], axis=1))
                out = jnp.dot(jnp.concatenate(lhs, axis=0), ones2, preferred_element_type=F32)
                if s % 2 == 0:
                    vcb[gi] = out[len(grp_pairs) * hd:, :]
                for j, (i, bi, p) in enumerate(grp_pairs):
                    rows = slice(j * hd, (j + 1) * hd)
                    ot_ref[bi, p] = jnp.where(sel, out[rows, LANES:], ot_ref[bi, p])
                    vc = vcb[gi][rows, 0:LANES] if s % 2 == 0 else vcb[gi][rows, LANES:]
                    new = (st_ref[bi, p] * tile(d_ref, bi, p)[s:s + 1, :]
                           + out[rows, 0:LANES] * tile(b_ref, bi, p)[s:s + 1, :]
                           + vc * tile(k_ref, bi, p)[s:s + 1, :])
                    st_ref[bi, p] = new
                    q_ref[pl.ds(i * hd, hd), :] = (new * tile(r_ref, bi, p)[s:s + 1, :]).astype(BF16)
        return carry

    lax.fori_loop(0, tau // SUBLANES, group, 0)

    o_last = jnp.dot(q_ref[...], ones2[0:LANES, 0:LANES], preferred_element_type=F32)
    sel = lane_in_head == (tau - 1)
    for i, (bi, p) in enumerate(pairs):
        ot_ref[bi, p] = jnp.where(sel, o_last[i * hd:(i + 1) * hd, :], ot_ref[bi, p])

    ones_hl = _group_ones(2 * LANES, LANES, hd)
    first_head = lane < hd
    inv_hd = 1.0 / hd
    for bi in range(nb):
        for p0 in range(0, npair, 2):
            tr = jnp.concatenate([ot_ref[bi, p0], ot_ref[bi, p0 + 1]], axis=0).T
            top, bot = tr[0:hd, :], tr[hd:2 * hd, :]
            nat = (jnp.where(first_head, top, pltpu.roll(bot, hd, axis=1)),
                   jnp.where(first_head, pltpu.roll(top, hd, axis=1), bot))
            for p, o in zip((p0, p0 + 1), nat):
                cols = pl.ds(p * LANES, LANES)
                mean = _head_sum(o, ones_hl) * inv_hd
                oc = o - mean
                var = _head_sum(oc * oc, ones_hl) * inv_hd
                on = oc * lax.rsqrt(var + GN_EPS) * gng_ref[:, cols] + gnb_ref[:, cols]
                bonus = _head_sum(r_ref[bi, :, cols] * k_ref[bi, :, cols] * rk_ref[:, cols], ones_hl)
                y = (on + bonus * v_ref[bi, :, cols]) * g_ref[bi, :, cols]
                y_ref[bi, :, cols] = y.astype(y_ref.dtype)


def rwkv_recurrence(r, k, v, nkk, b, d, g, r_k, gn_g, gn_b, *, batch, seq):
    width = RW_WIDTH
    tau = RW_HEAD
    npair = width // LANES
    spec = pl.BlockSpec((batch, tau, width), lambda i: (0, i, 0))
    vec = pl.BlockSpec((1, width), lambda i: (0, 0))
    acts = [a.reshape(batch, seq, width) for a in (r, k, v, nkk, b, d, g)]
    return pl.pallas_call(
        _rwkv_rec_kernel,
        grid=(seq // tau,),
        in_specs=[spec] * 7 + [vec] * 3,
        out_specs=spec,
        out_shape=jax.ShapeDtypeStruct((batch, seq, width), BF16),
        scratch_shapes=[pltpu.VMEM((batch, npair, RW_HEAD, LANES), F32),
                        pltpu.VMEM((batch * npair * RW_HEAD, LANES), BF16),
                        pltpu.VMEM((batch, npair, RW_HEAD, LANES), F32)],
        compiler_params=_cparams("arbitrary"),
        name="rwkv_recurrence",
    )(*acts, r_k.reshape(1, width), gn_g.reshape(1, width), gn_b.reshape(1, width))


def _merge_kernel(a_ref, l_ref, r_ref, wa_ref, wl_ref, wr_ref, ga_ref, gl_ref, gr_ref, o_ref):
    acc = jax.nn.sigmoid(ga_ref[...]) * jnp.dot(a_ref[...], wa_ref[...], preferred_element_type=F32)
    acc += jax.nn.sigmoid(gl_ref[...]) * jnp.dot(l_ref[...], wl_ref[...], preferred_element_type=F32)
    acc += jax.nn.sigmoid(gr_ref[...]) * jnp.dot(r_ref[...], wr_ref[...], preferred_element_type=F32)
    o_ref[...] = acc.astype(o_ref.dtype)


def merge_branches(y_a, y_l, y_r, w_a, w_l, w_r, proj, *, tm, tn):
    m = y_a.shape[0]
    n = w_a.shape[1]
    nj = n // tn
    act = lambda width: pl.BlockSpec((tm, width), lambda i, j: (i, 0))
    wgt = lambda width: pl.BlockSpec((width, tn), lambda i, j: (0, j))
    gate = lambda g: pl.BlockSpec((tm, tn), lambda i, j: (i, g * nj + j))
    return pl.pallas_call(
        _merge_kernel,
        grid=(m // tm, nj),
        in_specs=[act(y_a.shape[1]), act(y_l.shape[1]), act(y_r.shape[1]),
                  wgt(w_a.shape[0]), wgt(w_l.shape[0]), wgt(w_r.shape[0]),
                  gate(0), gate(1), gate(2)],
        out_specs=pl.BlockSpec((tm, tn), lambda i, j: (i, j)),
        out_shape=jax.ShapeDtypeStruct((m, n), BF16),
        compiler_params=_cparams("parallel", "parallel"),
        name="merge_branches",
    )(y_a, y_l, y_r, w_a, w_l, w_r, proj, proj, proj)


def _ffn_up_kernel(x_ref, wg_ref, wv_ref, cw_ref, cb_ref, o_ref, wgb_ref, wvb_ref, halo_ref,
                   *, tiles_per_seq, valid_tiles):
    j = pl.program_id(0)
    i = pl.program_id(1)

    @pl.when(i == 0)
    def _():
        wgb_ref[...] = wg_ref[...].astype(BF16)
        wvb_ref[...] = wv_ref[...].astype(BF16)

    @pl.when(i % tiles_per_seq == 0)
    def _():
        halo_ref[...] = jnp.zeros_like(halo_ref)

    @pl.when(j < valid_tiles)
    def _():
        x = x_ref[...]
        g = jnp.dot(x, wgb_ref[...], preferred_element_type=F32)
        tm = g.shape[0]
        tail = halo_ref[...]
        c = g * cw_ref[FFN_CONV - 1:FFN_CONV, :] + cb_ref[...]
        for sh in range(1, FFN_CONV):
            c = c + _shift_rows(g, sh, tail) * cw_ref[FFN_CONV - 1 - sh:FFN_CONV - sh, :]
        halo_ref[...] = g[tm - SUBLANES:tm, :]
        v = jnp.dot(x, wvb_ref[...], preferred_element_type=F32)
        o_ref[...] = (_gelu_tanh(c) * v).astype(o_ref.dtype)

    @pl.when(j >= valid_tiles)
    def _():
        o_ref[...] = jnp.zeros_like(o_ref)


def ffn_up(x, w, conv_w, conv_b, *, seq, tm, tn, dff_pad):
    m, k = x.shape
    dff = conv_w.shape[1]
    valid = dff // tn
    last = valid - 1
    return pl.pallas_call(
        functools.partial(_ffn_up_kernel, tiles_per_seq=seq // tm, valid_tiles=valid),
        grid=(dff_pad // tn, m // tm),
        in_specs=[pl.BlockSpec((tm, k), lambda j, i: (i, 0)),
                  pl.BlockSpec((k, tn), lambda j, i: (0, jnp.minimum(j, last))),
                  pl.BlockSpec((k, tn), lambda j, i: (0, valid + jnp.minimum(j, last))),
                  pl.BlockSpec((FFN_CONV, tn), lambda j, i: (0, jnp.minimum(j, last))),
                  pl.BlockSpec((1, tn), lambda j, i: (0, jnp.minimum(j, last)))],
        out_specs=pl.BlockSpec((tm, tn), lambda j, i: (i, j)),
        out_shape=jax.ShapeDtypeStruct((m, dff_pad), BF16),
        scratch_shapes=[pltpu.VMEM((k, tn), BF16), pltpu.VMEM((k, tn), BF16),
                        pltpu.VMEM((SUBLANES, tn), F32)],
        compiler_params=_cparams("arbitrary", "arbitrary"),
        name="ffn_up",
    )(x, w, w, conv_w, conv_b.reshape(1, dff))


def _rotate_half_cols(w):
    half = MLA_ROPE // 2
    return jnp.concatenate([-w[..., half:], w[..., :half]], axis=-1)


SRC_CKV = MLA_Q_RANK
SRC_KR = SRC_CKV + MLA_KV_RANK
SRC_LRU = SRC_KR + MLA_ROPE
SRC_RW = SRC_LRU + 2 * LRU_WIDTH
SRC_XWA = SRC_RW + 3 * RW_WIDTH
SRC_XG = SRC_XWA + RW_DECAY_LORA + RW_AAA_LORA
SRC_GATES = SRC_XG + RW_GATE_LORA


def _w_in_kernel(s_ref, o_ref):
    rows = s_ref.shape[0]

    def copy(dst, src, width):
        o_ref[:, dst:dst + width] = s_ref[:, src:src + width].astype(BF16)

    copy(OFF_GATES, SRC_GATES, 3 * D_MODEL)
    copy(OFF_LRU_X, SRC_LRU, 2 * LRU_WIDTH)
    copy(OFF_RW_R, SRC_RW, 3 * RW_WIDTH)
    copy(OFF_CQ, 0, MLA_Q_RANK)
    o_ref[:, OFF_CQ + MLA_Q_RANK:OFF_CQ + Q_PAD] = jnp.zeros((rows, Q_PAD - MLA_Q_RANK), BF16)
    copy(OFF_CKV, SRC_CKV, MLA_KV_RANK)
    half = MLA_ROPE // 2
    t = s_ref[:, SRC_KR:SRC_KR + LANES]
    lane = lax.broadcasted_iota(jnp.int32, t.shape, 1)
    rot = jnp.where(lane < MLA_ROPE + half, -pltpu.roll(t, half, axis=1),
                    pltpu.roll(t, MLA_ROPE + half, axis=1))
    o_ref[:, OFF_KR:OFF_KR + LANES] = jnp.where(lane < MLA_ROPE, t, rot).astype(BF16)
    copy(OFF_XWA, SRC_XWA, RW_DECAY_LORA + RW_AAA_LORA)
    xg = s_ref[:, SRC_XG:SRC_XG + XG_PAD]
    lane2 = lax.broadcasted_iota(jnp.int32, xg.shape, 1)
    o_ref[:, OFF_XG:OFF_XG + XG_PAD] = jnp.where(lane2 < RW_GATE_LORA, xg, 0.0).astype(BF16)


def relayout_w_in(w, *, tr):
    d, n = w.shape
    return pl.pallas_call(
        _w_in_kernel,
        grid=(d // tr,),
        in_specs=[pl.BlockSpec((tr, n), lambda i: (i, 0))],
        out_specs=pl.BlockSpec((tr, IN_COLS_PAD), lambda i: (i, 0)),
        out_shape=jax.ShapeDtypeStruct((d, IN_COLS_PAD), BF16),
        compiler_params=_cparams("parallel"),
        name="w_in_relayout",
    )(w)


def _cast_kernel(s_ref, o_ref, *, valid_blocks):
    i = pl.program_id(0)

    @pl.when(i < valid_blocks)
    def _():
        o_ref[...] = s_ref[...].astype(o_ref.dtype)

    @pl.when(i >= valid_blocks)
    def _():
        o_ref[...] = jnp.zeros_like(o_ref)


def cast_bf16(w, *, tr, rows_out=None):
    r, c = w.shape
    rows_out = r if rows_out is None else rows_out
    valid = r // tr
    return pl.pallas_call(
        functools.partial(_cast_kernel, valid_blocks=valid),
        grid=(rows_out // tr,),
        in_specs=[pl.BlockSpec((tr, c), lambda i: (jnp.minimum(i, valid - 1), 0))],
        out_specs=pl.BlockSpec((tr, c), lambda i: (i, 0)),
        out_shape=jax.ShapeDtypeStruct((rows_out, c), BF16),
        compiler_params=_cparams("parallel"),
        name="cast_bf16",
    )(w)


def _prep_w_uq(w):
    rope = w[..., MLA_NOPE:]
    w = jnp.concatenate([w, _rotate_half_cols(rope)], axis=-1)
    w = w.reshape(MLA_Q_RANK, MLA_HEADS * 2 * LANES)
    return jnp.pad(w, ((0, Q_PAD - MLA_Q_RANK), (0, 0))).astype(BF16)


def _rope_table(positions):
    inv_freq = ROPE_THETA ** (-jnp.arange(0, MLA_ROPE, 2, dtype=F32) / MLA_ROPE)
    ang = positions.astype(F32)[..., None] * inv_freq
    cos, sin = jnp.cos(ang), jnp.sin(ang)
    return jnp.concatenate([cos, cos, sin, sin], axis=-1).reshape(-1, 2 * MLA_ROPE)


class _Tiles:
    in_proj = (1024, 1024)
    mla_up_rows = 512
    attn_q = 512
    lru_rows = 512
    rwkv_prep_rows = 256
    merge = (512, 1024)
    out_mm = (1024, 1024, 1024)
    ln_rows = 256
    ffn_up = (1024, 256)
    w_in_rows = 64
    cast_rows = 256


def kernel(x, positions, w_in, mla_q_norm, mla_w_uq, mla_kv_norm, mla_w_ukv, lru_conv_w, lru_conv_b, lru_w_a, lru_b_a, lru_w_x, lru_b_x, lru_lambda, rw_mu, rw_w0, rw_w2, rw_a0, rw_a2, rw_g2, rw_k_k, rw_k_a, rw_r_k, rw_gn_g, rw_gn_b, w_o_mla, w_o_lru, w_o_rwkv, w_out, ln1_g, ln1_b, ffn_w_up, ffn_conv_w, ffn_conv_b, ffn_w_down, ln2_g, ln2_b):
    batch, seq, d = x.shape
    m = batch * seq
    t = _Tiles
    cs = _rope_table(positions)
    xf = x.reshape(m, d)
    xb = xf.astype(BF16)
    for l in range(DEPTH):
        cast = functools.partial(cast_bf16, tr=t.cast_rows)
        proj = matmul(xb, relayout_w_in(w_in[l], tr=t.w_in_rows), tm=t.in_proj[0], tn=t.in_proj[1])

        q_gain = jnp.pad(mla_q_norm[l], (0, Q_PAD - MLA_Q_RANK)).reshape(1, Q_PAD)
        q = mla_q_up(proj, q_gain, _prep_w_uq(mla_w_uq[l]), cs, batch=batch, seq=seq, tm=t.mla_up_rows)
        w_ukv = cast(mla_w_ukv[l].reshape(MLA_KV_RANK, -1))
        k, v = mla_kv_up(proj, mla_kv_norm[l].reshape(1, -1), w_ukv, cs, batch=batch, seq=seq,
                         tm=t.mla_up_rows)
        y_a = mla_attention(q, k, v, tq=t.attn_q).reshape(m, -1)

        y_l = rglru_branch(proj, lru_conv_w[l], lru_conv_b[l], lru_w_a[l], lru_b_a[l],
                           lru_w_x[l], lru_b_x[l], lru_lambda[l], batch=batch, seq=seq, ts=t.lru_rows)

        rw = rwkv_prep(proj, rw_mu[l], rw_w0[l], rw_w2[l], rw_a0[l], rw_a2[l], rw_g2[l],
                       rw_k_k[l], rw_k_a[l], batch=batch, seq=seq, ts=t.rwkv_prep_rows)
        y_r = rwkv_recurrence(*rw, rw_r_k[l], rw_gn_g[l], rw_gn_b[l], batch=batch, seq=seq)
        y_r = y_r.reshape(m, RW_WIDTH)

        merged = merge_branches(y_a, y_l, y_r, cast(w_o_mla[l]), cast(w_o_lru[l]), cast(w_o_rwkv[l]),
                                proj, tm=t.merge[0], tn=t.merge[1])
        mm = dict(tm=t.out_mm[0], tn=t.out_mm[1], tk=t.out_mm[2], ln_rows=t.ln_rows)
        xf, xb = matmul_residual_layernorm(merged, cast(w_out[l]), xf, ln1_g[l], ln1_b[l],
                                           name="mixer_out", **mm)

        h = ffn_up(xb, ffn_w_up[l], ffn_conv_w[l], ffn_conv_b[l], seq=seq, tm=t.ffn_up[0],
                   tn=t.ffn_up[1], dff_pad=D_FF_PAD)
        xf, xb = matmul_residual_layernorm(h, cast(ffn_w_down[l], rows_out=D_FF_PAD), xf, ln2_g[l],
                                           ln2_b[l], name="ffn_down", **mm)
    return xf.reshape(batch, seq, d)
```

```python
import functools

import jax
import jax.numpy as jnp
from jax import lax
from jax.experimental import pallas as pl
from jax.experimental.pallas import tpu as pltpu

F32 = jnp.float32
BF16 = jnp.bfloat16

D_MODEL = 4096
DEPTH = 2
CHUNK = 64
MLA_HEADS = 16
MLA_Q_RANK = 896
MLA_KV_RANK = 512
MLA_NOPE = 128
MLA_ROPE = 64
MLA_V = 128
ROPE_THETA = 10000.0
LRU_WIDTH = 1024
LRU_BLOCKS = 8
LRU_BLOCK = LRU_WIDTH // LRU_BLOCKS
LRU_CONV = 4
LRU_C = 8.0
RW_WIDTH = 1024
RW_HEAD = 64
RW_HEADS = RW_WIDTH // RW_HEAD
RW_DECAY_LORA = 64
RW_AAA_LORA = 64
RW_GATE_LORA = 160
D_FF = 11008
FFN_CONV = 3
ALPHA = (2 * DEPTH) ** 0.25
LN_EPS = 1e-5
RMS_EPS = 1e-6
GN_EPS = 64e-5

LANES = 128
SUBLANES = 8
VMEM_LIMIT_BYTES = 56 * 1024 * 1024

Q_PAD = 1024
D_FF_PAD = 11264
XG_PAD = 256
OFF_GATES = 0
OFF_LRU_X = 3 * D_MODEL
OFF_LRU_G = OFF_LRU_X + LRU_WIDTH
OFF_RW_R = OFF_LRU_G + LRU_WIDTH
OFF_RW_K = OFF_RW_R + RW_WIDTH
OFF_RW_V = OFF_RW_K + RW_WIDTH
OFF_CQ = OFF_RW_V + RW_WIDTH
OFF_CKV = OFF_CQ + Q_PAD
OFF_KR = OFF_CKV + MLA_KV_RANK
OFF_XWA = OFF_KR + 2 * MLA_ROPE
OFF_XG = OFF_XWA + RW_DECAY_LORA + RW_AAA_LORA
IN_COLS_PAD = OFF_XG + XG_PAD

RW_GROUP_PAIRS = 4
_SQRT_2_OVER_PI = 0.7978845608028654
_LOG2_E = 1.4426950408889634


def _cparams(*sem):
    return pltpu.CompilerParams(dimension_semantics=sem, vmem_limit_bytes=VMEM_LIMIT_BYTES)


def _gelu_tanh(x):
    return 0.5 * x * (1.0 + jnp.tanh(_SQRT_2_OVER_PI * (x + 0.044715 * (x * x * x))))


def _softplus(z):
    return jnp.maximum(z, 0.0) + jnp.log1p(jnp.exp(-jnp.abs(z)))


def _shift_rows(x, shift, tail):
    row = lax.broadcasted_iota(jnp.int32, x.shape, 0)
    xs = pltpu.roll(x, shift, axis=0)
    for j in range(shift):
        src = SUBLANES - shift + j
        xs = jnp.where(row == j, tail[src:src + 1, :], xs)
    return xs


def _group_ones(rows, cols, group):
    shift = group.bit_length() - 1
    r = lax.broadcasted_iota(jnp.int32, (rows, cols), 0)
    c = lax.broadcasted_iota(jnp.int32, (rows, cols), 1)
    return jnp.where(((r & (cols - 1)) >> shift) == (c >> shift), 1.0, 0.0).astype(BF16)


def _head_sum(x, ones_hl):
    hi = x.astype(BF16)
    lo = (x - hi.astype(F32)).astype(BF16)
    return jnp.dot(jnp.concatenate([hi, lo], axis=1), ones_hl, preferred_element_type=F32)


def _mm_kernel(a_ref, b_ref, o_ref):
    o_ref[...] = jnp.dot(a_ref[...], b_ref[...], preferred_element_type=F32).astype(o_ref.dtype)


def matmul(a, b, *, tm, tn, out_dtype=F32):
    m, k = a.shape
    _, n = b.shape
    return pl.pallas_call(
        _mm_kernel,
        grid=(m // tm, n // tn),
        in_specs=[pl.BlockSpec((tm, k), lambda i, j: (i, 0)),
                  pl.BlockSpec((k, tn), lambda i, j: (0, j))],
        out_specs=pl.BlockSpec((tm, tn), lambda i, j: (i, j)),
        out_shape=jax.ShapeDtypeStruct((m, n), out_dtype),
        compiler_params=_cparams("parallel", "parallel"),
        name="in_proj",
    )(a, b)


def _mm_res_kernel(a_ref, w_ref, x_ref, o_ref, acc_ref):
    k = pl.program_id(2)

    @pl.when(k == 0)
    def _():
        acc_ref[...] = ALPHA * x_ref[...]

    acc_ref[...] += jnp.dot(a_ref[...], w_ref[...], preferred_element_type=F32)

    @pl.when(k == pl.num_programs(2) - 1)
    def _():
        o_ref[...] = acc_ref[...]


def _ln_kernel(y_ref, g_ref, b_ref, o_ref, ob_ref):
    y = y_ref[...]
    mu = jnp.mean(y, axis=-1, keepdims=True)
    yc = y - mu
    var = jnp.mean(yc * yc, axis=-1, keepdims=True)
    o = yc * lax.rsqrt(var + LN_EPS) * g_ref[...] + b_ref[...]
    o_ref[...] = o
    ob_ref[...] = o.astype(BF16)


def matmul_residual_layernorm(a, w, x, g, b, *, tm, tn, tk, ln_rows, name):
    m, k = a.shape
    n = w.shape[1]
    y = pl.pallas_call(
        _mm_res_kernel,
        grid=(m // tm, n // tn, k // tk),
        in_specs=[pl.BlockSpec((tm, tk), lambda i, j, kk: (i, kk)),
                  pl.BlockSpec((tk, tn), lambda i, j, kk: (kk, j)),
                  pl.BlockSpec((tm, tn), lambda i, j, kk: (i, j))],
        out_specs=pl.BlockSpec((tm, tn), lambda i, j, kk: (i, j)),
        out_shape=jax.ShapeDtypeStruct((m, n), F32),
        scratch_shapes=[pltpu.VMEM((tm, tn), F32)],
        compiler_params=_cparams("parallel", "parallel", "arbitrary"),
        name=name,
    )(a, w, x)
    return pl.pallas_call(
        _ln_kernel,
        grid=(m // ln_rows,),
        in_specs=[pl.BlockSpec((ln_rows, n), lambda i: (i, 0)),
                  pl.BlockSpec((1, n), lambda i: (0, 0)),
                  pl.BlockSpec((1, n), lambda i: (0, 0))],
        out_specs=[pl.BlockSpec((ln_rows, n), lambda i: (i, 0)),
                   pl.BlockSpec((ln_rows, n), lambda i: (i, 0))],
        out_shape=[jax.ShapeDtypeStruct((m, n), F32), jax.ShapeDtypeStruct((m, n), BF16)],
        compiler_params=_cparams("parallel"),
        name="layernorm",
    )(y, g.reshape(1, n), b.reshape(1, n))


def _rope_half(block, cs):
    p = block * cs
    return p + pltpu.roll(p, MLA_ROPE, axis=1)


def _q_up_kernel(p_ref, g_ref, w_ref, cs_ref, q_ref, *, rank):
    x = p_ref[...]
    ms = jnp.sum(x * x, axis=-1, keepdims=True) * (1.0 / rank)
    xn = (x * lax.rsqrt(ms + RMS_EPS) * g_ref[...]).astype(BF16)
    cs = cs_ref[...]
    hw = 2 * LANES
    for h in range(q_ref.shape[1]):
        acc = jnp.dot(xn, w_ref[:, h * hw:(h + 1) * hw], preferred_element_type=F32)
        rot = _rope_half(acc[:, LANES:hw], cs)
        q_ref[0, h, :, 0:MLA_NOPE] = acc[:, 0:MLA_NOPE].astype(BF16)
        q_ref[0, h, :, MLA_NOPE:MLA_NOPE + MLA_ROPE] = rot[:, 0:MLA_ROPE].astype(BF16)


def mla_q_up(proj, gain, w, cs, *, batch, seq, tm):
    nh = MLA_HEADS
    spt = seq // tm
    return pl.pallas_call(
        functools.partial(_q_up_kernel, rank=MLA_Q_RANK),
        grid=(batch * spt,),
        in_specs=[pl.BlockSpec((tm, Q_PAD), lambda i: (i, OFF_CQ // Q_PAD)),
                  pl.BlockSpec((1, Q_PAD), lambda i: (0, 0)),
                  pl.BlockSpec(w.shape, lambda i: (0, 0)),
                  pl.BlockSpec((tm, LANES), lambda i: (i, 0))],
        out_specs=pl.BlockSpec((1, nh, tm, MLA_NOPE + MLA_ROPE), lambda i: (i // spt, 0, i % spt, 0)),
        out_shape=jax.ShapeDtypeStruct((batch, nh, seq, MLA_NOPE + MLA_ROPE), BF16),
        compiler_params=_cparams("parallel"),
        name="mla_q_up",
    )(proj, gain, w, cs)


def _kv_up_kernel(p_ref, kr_ref, g_ref, w_ref, cs_ref, k_ref, v_ref, *, rank):
    x = p_ref[...]
    ms = jnp.sum(x * x, axis=-1, keepdims=True) * (1.0 / rank)
    xn = (x * lax.rsqrt(ms + RMS_EPS) * g_ref[...]).astype(BF16)
    krope = _rope_half(kr_ref[...], cs_ref[...])[:, 0:MLA_ROPE].astype(BF16)
    hw = MLA_NOPE + MLA_V
    for h in range(k_ref.shape[1]):
        acc = jnp.dot(xn, w_ref[:, h * hw:(h + 1) * hw], preferred_element_type=F32)
        k_ref[0, h, :, 0:MLA_NOPE] = acc[:, 0:MLA_NOPE].astype(BF16)
        k_ref[0, h, :, MLA_NOPE:MLA_NOPE + MLA_ROPE] = krope
        v_ref[0, h] = acc[:, MLA_NOPE:hw].astype(BF16)


def mla_kv_up(proj, gain, w, cs, *, batch, seq, tm):
    nh = MLA_HEADS
    spt = seq // tm
    dk = MLA_NOPE + MLA_ROPE
    return pl.pallas_call(
        functools.partial(_kv_up_kernel, rank=MLA_KV_RANK),
        grid=(batch * spt,),
        in_specs=[pl.BlockSpec((tm, MLA_KV_RANK), lambda i: (i, OFF_CKV // MLA_KV_RANK)),
                  pl.BlockSpec((tm, LANES), lambda i: (i, OFF_KR // LANES)),
                  pl.BlockSpec((1, MLA_KV_RANK), lambda i: (0, 0)),
                  pl.BlockSpec(w.shape, lambda i: (0, 0)),
                  pl.BlockSpec((tm, LANES), lambda i: (i, 0))],
        out_specs=[pl.BlockSpec((1, nh, tm, dk), lambda i: (i // spt, 0, i % spt, 0)),
                   pl.BlockSpec((1, nh, tm, MLA_V), lambda i: (i // spt, 0, i % spt, 0))],
        out_shape=[jax.ShapeDtypeStruct((batch, nh, seq, dk), BF16),
                   jax.ShapeDtypeStruct((batch, nh, seq, MLA_V), BF16)],
        compiler_params=_cparams("parallel"),
        name="mla_kv_up",
    )(proj, proj, gain, w, cs)


def _flash_kernel(q_ref, k_ref, v_ref, o_ref, *, tq, scale):
    qi = pl.program_id(2)
    q = q_ref[0, 0]

    def scores(j):
        k = k_ref[0, 0, pl.ds(pl.multiple_of(j * tq, tq), tq), :]
        return lax.dot_general(q, k, (((1,), (1,)), ((), ())),
                               preferred_element_type=F32) * (scale * _LOG2_E)

    def update(j, s, m, l, acc):
        v = v_ref[0, 0, pl.ds(pl.multiple_of(j * tq, tq), tq), :]
        m_new = jnp.maximum(m, jnp.max(s, axis=-1, keepdims=True))
        alpha = jnp.exp2(m - m_new)
        p = jnp.exp2(s - m_new)
        l = alpha * l + jnp.sum(p, axis=-1, keepdims=True)
        acc = alpha * acc + jnp.dot(p.astype(BF16), v, preferred_element_type=F32)
        return m_new, l, acc

    def pair(jj, carry):
        s_a, s_b = scores(2 * jj), scores(2 * jj + 1)
        return update(2 * jj + 1, s_b, *update(2 * jj, s_a, *carry))

    def single(_, carry):
        return update(qi - 1, scores(qi - 1), *carry)

    carry = (jnp.full((tq, 1), -1e30, F32), jnp.zeros((tq, 1), F32), jnp.zeros((tq, MLA_V), F32))
    carry = lax.fori_loop(0, qi >> 1, pair, carry)
    carry = lax.fori_loop(0, qi & 1, single, carry)
    shift = CHUNK.bit_length() - 1
    qc = lax.broadcasted_iota(jnp.int32, (tq, tq), 0) >> shift
    kc = lax.broadcasted_iota(jnp.int32, (tq, tq), 1) >> shift
    m, l, acc = update(qi, jnp.where(kc <= qc, scores(qi), -jnp.inf), *carry)
    o_ref[0] = (acc / l).astype(o_ref.dtype)


def mla_attention(q, k, v, *, tq):
    batch, nh, seq, dk = q.shape
    scale = (MLA_NOPE + MLA_ROPE) ** -0.5
    return pl.pallas_call(
        functools.partial(_flash_kernel, tq=tq, scale=scale),
        grid=(batch, nh, seq // tq),
        in_specs=[pl.BlockSpec((1, 1, tq, dk), lambda b, h, i: (b, h, i, 0)),
                  pl.BlockSpec((1, 1, seq, dk), lambda b, h, i: (b, h, 0, 0)),
                  pl.BlockSpec((1, 1, seq, MLA_V), lambda b, h, i: (b, h, 0, 0))],
        out_specs=pl.BlockSpec((1, tq, MLA_V), lambda b, h, i: (b, i, h)),
        out_shape=jax.ShapeDtypeStruct((batch, seq, nh * MLA_V), BF16),
        compiler_params=_cparams("parallel", "parallel", "arbitrary"),
        name="mla_attention",
    )(q, k, v)


def _lru_kernel(x_ref, g_ref, cw_ref, cb_ref, wa_ref, ba_ref, wx_ref, bx_ref, lam_ref, o_ref,
                tail_ref, h_ref, a_scr, u_scr):
    @pl.when(pl.program_id(1) == 0)
    def _():
        tail_ref[...] = jnp.zeros_like(tail_ref)
        h_ref[...] = jnp.zeros_like(h_ref)

    x = x_ref[...]
    ts = x.shape[0]
    tail = tail_ref[...]
    xc = x * cw_ref[LRU_CONV - 1:LRU_CONV, :] + cb_ref[...]
    for sh in range(1, LRU_CONV):
        xc = xc + _shift_rows(x, sh, tail) * cw_ref[LRU_CONV - 1 - sh:LRU_CONV - sh, :]
    tail_ref[...] = x[ts - SUBLANES:ts, :]

    xb = xc.astype(BF16)
    ra, rx = [], []
    for n in range(LRU_BLOCKS):
        blk = xb[:, n * LRU_BLOCK:(n + 1) * LRU_BLOCK]
        ra.append(jnp.dot(blk, wa_ref[n], preferred_element_type=F32))
        rx.append(jnp.dot(blk, wx_ref[n], preferred_element_type=F32))
    r = jax.nn.sigmoid(jnp.concatenate(ra, axis=1) + ba_ref[...])
    gi = jax.nn.sigmoid(jnp.concatenate(rx, axis=1) + bx_ref[...])
    log_a = -LRU_C * r * _softplus(-lam_ref[...])
    a_scr[...] = jnp.exp(log_a)
    u_scr[...] = jnp.sqrt(1.0 - jnp.exp(2.0 * log_a)) * (gi * xc)

    row8 = lax.broadcasted_iota(jnp.int32, (SUBLANES, x.shape[1]), 0)

    def body(grp, h):
        base = pl.multiple_of(grp * SUBLANES, SUBLANES)
        a8 = a_scr[pl.ds(base, SUBLANES), :]
        u8 = u_scr[pl.ds(base, SUBLANES), :]
        hs = u8
        for s in range(SUBLANES):
            h = a8[s:s + 1, :] * h + u8[s:s + 1, :]
            hs = jnp.where(row8 == s, h, hs)
        u_scr[pl.ds(base, SUBLANES), :] = hs
        return h

    h_ref[...] = lax.fori_loop(0, ts // SUBLANES, body, h_ref[...])
    o_ref[...] = (u_scr[...] * _gelu_tanh(g_ref[...])).astype(o_ref.dtype)


def rglru_branch(proj, conv_w, conv_b, w_a, b_a, w_x, b_x, lam, *, batch, seq, ts):
    w = LRU_WIDTH
    spt = seq // ts
    vec = lambda: pl.BlockSpec((1, w), lambda b, s: (0, 0))
    blockdiag = lambda: pl.BlockSpec((LRU_BLOCKS, LRU_BLOCK, LRU_BLOCK), lambda b, s: (0, 0, 0))
    return pl.pallas_call(
        _lru_kernel,
        grid=(batch, spt),
        in_specs=[pl.BlockSpec((ts, w), lambda b, s: (b * spt + s, OFF_LRU_X // w)),
                  pl.BlockSpec((ts, w), lambda b, s: (b * spt + s, OFF_LRU_G // w)),
                  pl.BlockSpec((LRU_CONV, w), lambda b, s: (0, 0)),
                  vec(), blockdiag(), vec(), blockdiag(), vec(), vec()],
        out_specs=pl.BlockSpec((ts, w), lambda b, s: (b * spt + s, 0)),
        out_shape=jax.ShapeDtypeStruct((batch * seq, w), BF16),
        scratch_shapes=[pltpu.VMEM((SUBLANES, w), F32), pltpu.VMEM((1, w), F32),
                        pltpu.VMEM((ts, w), F32), pltpu.VMEM((ts, w), F32)],
        compiler_params=_cparams("parallel", "arbitrary"),
        name="rglru",
    )(proj, proj, conv_w, conv_b.reshape(1, w), w_a.astype(BF16), b_a.reshape(1, w),
      w_x.astype(BF16), b_x.reshape(1, w), lam.reshape(1, w))


def _rwkv_prep_kernel(r_ref, k_ref, v_ref, wa_ref, xg_ref,
                      mur_ref, muk_ref, muv_ref, muwa_ref, mug_ref,
                      w0_ref, w2_ref, a0_ref, a2_ref, g2_ref, kkw_ref, ka_ref,
                      ro_ref, ko_ref, vo_ref, nkk_ref, b_ref, d_ref, g_ref,
                      tr_ref, tk_ref, tv_ref, twa_ref, tg_ref):
    @pl.when(pl.program_id(1) == 0)
    def _():
        for t in (tr_ref, tk_ref, tv_ref, twa_ref, tg_ref):
            t[...] = jnp.zeros_like(t)

    def mix(x_ref, mu_ref, tail_ref):
        x = x_ref[...]
        prev = _shift_rows(x, 1, tail_ref[...])
        tail_ref[...] = x[x.shape[0] - SUBLANES:, :]
        return x + mu_ref[...] * (prev - x)

    r = mix(r_ref, mur_ref, tr_ref)
    k = mix(k_ref, muk_ref, tk_ref)
    v = mix(v_ref, muv_ref, tv_ref)
    xwa = mix(wa_ref, muwa_ref, twa_ref)
    xg = mix(xg_ref, mug_ref, tg_ref)
    w = -_softplus(-(w0_ref[...] + jnp.dot(jnp.tanh(xwa).astype(BF16), w2_ref[...],
                                           preferred_element_type=F32))) - 0.5
    a = jax.nn.sigmoid(a0_ref[...] + jnp.dot(xwa.astype(BF16), a2_ref[...], preferred_element_type=F32))
    g = jnp.dot(jax.nn.sigmoid(xg).astype(BF16), g2_ref[...], preferred_element_type=F32)

    kk = k * kkw_ref[...]
    ones_hl = _group_ones(2 * LANES, LANES, RW_HEAD)
    sq = kk * kk
    ss = jnp.concatenate([_head_sum(sq[:, c * LANES:(c + 1) * LANES], ones_hl)
                          for c in range(RW_WIDTH // LANES)], axis=1)
    kk = kk / jnp.maximum(jnp.sqrt(ss), 1e-12)

    ro_ref[...] = r
    ko_ref[...] = k * (1.0 + (a - 1.0) * ka_ref[...])
    vo_ref[...] = v
    nkk_ref[...] = -kk
    b_ref[...] = kk * a
    d_ref[...] = jnp.exp(-jnp.exp(w))
    g_ref[...] = g


def rwkv_prep(proj, mu, w0, w2, a0, a2, g2, k_k, k_a, *, batch, seq, ts):
    w = RW_WIDTH
    spt = seq // ts
    lora = RW_DECAY_LORA + RW_AAA_LORA
    row = lambda width, col: pl.BlockSpec((ts, width), lambda b, s: (b * spt + s, col))
    vec = lambda width: pl.BlockSpec((1, width), lambda b, s: (0, 0))
    mat = lambda rows: pl.BlockSpec((rows, w), lambda b, s: (0, 0))
    mu_r, mu_k, mu_v = (mu[i * w:(i + 1) * w].reshape(1, w) for i in range(3))
    mu_wa = mu[3 * w:3 * w + lora].reshape(1, lora)
    mu_g = jnp.pad(mu[3 * w + lora:], (0, XG_PAD - RW_GATE_LORA)).reshape(1, XG_PAD)
    zeros = jnp.zeros((RW_DECAY_LORA, w), w2.dtype)
    w2p = jnp.concatenate([w2, zeros], axis=0).astype(BF16)
    a2p = jnp.concatenate([zeros, a2], axis=0).astype(BF16)
    g2p = jnp.pad(g2, ((0, XG_PAD - RW_GATE_LORA), (0, 0))).astype(BF16)
    out = jax.ShapeDtypeStruct((batch * seq, w), F32)
    return pl.pallas_call(
        _rwkv_prep_kernel,
        grid=(batch, spt),
        in_specs=[row(w, OFF_RW_R // w), row(w, OFF_RW_K // w), row(w, OFF_RW_V // w),
                  row(lora, OFF_XWA // lora), row(XG_PAD, OFF_XG // XG_PAD),
                  vec(w), vec(w), vec(w), vec(lora), vec(XG_PAD),
                  vec(w), mat(lora), vec(w), mat(lora), mat(XG_PAD), vec(w), vec(w)],
        out_specs=[pl.BlockSpec((ts, w), lambda b, s: (b * spt + s, 0))] * 7,
        out_shape=[out] * 7,
        scratch_shapes=[pltpu.VMEM((SUBLANES, w), F32)] * 3
                       + [pltpu.VMEM((SUBLANES, lora), F32), pltpu.VMEM((SUBLANES, XG_PAD), F32)],
        compiler_params=_cparams("parallel", "arbitrary"),
        name="rwkv_prep",
    )(proj, proj, proj, proj, proj, mu_r, mu_k, mu_v, mu_wa, mu_g,
      w0.reshape(1, w), w2p, a0.reshape(1, w), a2p, g2p, k_k.reshape(1, w), k_a.reshape(1, w))


def _rwkv_rec_kernel(r_ref, k_ref, v_ref, nkk_ref, b_ref, d_ref, g_ref, rk_ref, gng_ref, gnb_ref,
                     y_ref, st_ref, q_ref, ot_ref):
    nb, tau, width = r_ref.shape
    npair = width // LANES
    hd = RW_HEAD
    pairs = [(bi, p) for bi in range(nb) for p in range(npair)]
    flat = [(i, bi, p) for i, (bi, p) in enumerate(pairs)]
    groups = [flat[j:j + RW_GROUP_PAIRS] for j in range(0, len(flat), RW_GROUP_PAIRS)]

    @pl.when(pl.program_id(0) == 0)
    def _():
        st_ref[...] = jnp.zeros_like(st_ref)

    q_ref[...] = jnp.zeros_like(q_ref)
    ot_ref[...] = jnp.zeros_like(ot_ref)

    row = lax.broadcasted_iota(jnp.int32, (hd, LANES), 0)
    lane = lax.broadcasted_iota(jnp.int32, (hd, LANES), 1)
    lane_in_head = lane & (hd - 1)
    diag = lane_in_head == row
    ones2 = _group_ones(2 * LANES, 2 * LANES, hd)

    def group(grp, carry):
        base = pl.multiple_of(grp * SUBLANES, SUBLANES)
        tile = lambda ref, bi, p: ref[bi, pl.ds(base, SUBLANES), pl.ds(p * LANES, LANES)]
        vcb = [None] * len(groups)
        for s in range(SUBLANES):
            sel = lane_in_head == (base + s - 1)
            for gi, grp_pairs in enumerate(groups):
                lhs = []
                for i, bi, p in grp_pairs:
                    pm = (st_ref[bi, p] * tile(nkk_ref, bi, p)[s:s + 1, :]).astype(BF16)
                    lhs.append(jnp.concatenate([pm, q_ref[pl.ds(i * hd, hd), :]], axis=1))
                if s % 2 == 0:
                    for i, bi, p in grp_pairs:
                        v8 = tile(v_ref, bi, p)
                        lhs.append(jnp.concatenate(
                            [jnp.where(diag, v8[s:s + 1, :], 0.0).astype(BF16),
                             jnp.where(diag, v8[s + 1:s + 2, :], 0.0).astype(BF16)], axis=1))
                out = jnp.dot(jnp.concatenate(lhs, axis=0), ones2, preferred_element_type=F32)
                if s % 2 == 0:
                    vcb[gi] = out[len(grp_pairs) * hd:, :]
                for j, (i, bi, p) in enumerate(grp_pairs):
                    rows = slice(j * hd, (j + 1) * hd)
                    ot_ref[bi, p] = jnp.where(sel, out[rows, LANES:], ot_ref[bi, p])
                    vc = vcb[gi][rows, 0:LANES] if s % 2 == 0 else vcb[gi][rows, LANES:]
                    new = (st_ref[bi, p] * tile(d_ref, bi, p)[s:s + 1, :]
                           + out[rows, 0:LANES] * tile(b_ref, bi, p)[s:s + 1, :]
                           + vc * tile(k_ref, bi, p)[s:s + 1, :])
                    st_ref[bi, p] = new
                    q_ref[pl.ds(i * hd, hd), :] = (new * tile(r_ref, bi, p)[s:s + 1, :]).astype(BF16)
        return carry

    lax.fori_loop(0, tau // SUBLANES, group, 0)

    o_last = jnp.dot(q_ref[...], ones2[0:LANES, 0:LANES], preferred_element_type=F32)
    sel = lane_in_head == (tau - 1)
    for i, (bi, p) in enumerate(pairs):
        ot_ref[bi, p] = jnp.where(sel, o_last[i * hd:(i + 1) * hd, :], ot_ref[bi, p])

    ones_hl = _group_ones(2 * LANES, LANES, hd)
    first_head = lane < hd
    inv_hd = 1.0 / hd
    for bi in range(nb):
        for p0 in range(0, npair, 2):
            tr = jnp.concatenate([ot_ref[bi, p0], ot_ref[bi, p0 + 1]], axis=0).T
            top, bot = tr[0:hd, :], tr[hd:2 * hd, :]
            nat = (jnp.where(first_head, top, pltpu.roll(bot, hd, axis=1)),
                   jnp.where(first_head, pltpu.roll(top, hd, axis=1), bot))
            for p, o in zip((p0, p0 + 1), nat):
                cols = pl.ds(p * LANES, LANES)
                mean = _head_sum(o, ones_hl) * inv_hd
                oc = o - mean
                var = _head_sum(oc * oc, ones_hl) * inv_hd
                on = oc * lax.rsqrt(var + GN_EPS) * gng_ref[:, cols] + gnb_ref[:, cols]
                bonus = _head_sum(r_ref[bi, :, cols] * k_ref[bi, :, cols] * rk_ref[:, cols], ones_hl)
                y = (on + bonus * v_ref[bi, :, cols]) * g_ref[bi, :, cols]
                y_ref[bi, :, cols] = y.astype(y_ref.dtype)


def rwkv_recurrence(r, k, v, nkk, b, d, g, r_k, gn_g, gn_b, *, batch, seq):
    width = RW_WIDTH
    tau = RW_HEAD
    npair = width // LANES
    spec = pl.BlockSpec((batch, tau, width), lambda i: (0, i, 0))
    vec = pl.BlockSpec((1, width), lambda i: (0, 0))
    acts = [a.reshape(batch, seq, width) for a in (r, k, v, nkk, b, d, g)]
    return pl.pallas_call(
        _rwkv_rec_kernel,
        grid=(seq // tau,),
        in_specs=[spec] * 7 + [vec] * 3,
        out_specs=spec,
        out_shape=jax.ShapeDtypeStruct((batch, seq, width), BF16),
        scratch_shapes=[pltpu.VMEM((batch, npair, RW_HEAD, LANES), F32),
                        pltpu.VMEM((batch * npair * RW_HEAD, LANES), BF16),
                        pltpu.VMEM((batch, npair, RW_HEAD, LANES), F32)],
        compiler_params=_cparams("arbitrary"),
        name="rwkv_recurrence",
    )(*acts, r_k.reshape(1, width), gn_g.reshape(1, width), gn_b.reshape(1, width))


def _merge_kernel(a_ref, l_ref, r_ref, wa_ref, wl_ref, wr_ref, ga_ref, gl_ref, gr_ref, o_ref):
    acc = jax.nn.sigmoid(ga_ref[...]) * jnp.dot(a_ref[...], wa_ref[...], preferred_element_type=F32)
    acc += jax.nn.sigmoid(gl_ref[...]) * jnp.dot(l_ref[...], wl_ref[...], preferred_element_type=F32)
    acc += jax.nn.sigmoid(gr_ref[...]) * jnp.dot(r_ref[...], wr_ref[...], preferred_element_type=F32)
    o_ref[...] = acc.astype(o_ref.dtype)


def merge_branches(y_a, y_l, y_r, w_a, w_l, w_r, proj, *, tm, tn):
    m = y_a.shape[0]
    n = w_a.shape[1]
    nj = n // tn
    act = lambda width: pl.BlockSpec((tm, width), lambda i, j: (i, 0))
    wgt = lambda width: pl.BlockSpec((width, tn), lambda i, j: (0, j))
    gate = lambda g: pl.BlockSpec((tm, tn), lambda i, j: (i, g * nj + j))
    return pl.pallas_call(
        _merge_kernel,
        grid=(m // tm, nj),
        in_specs=[act(y_a.shape[1]), act(y_l.shape[1]), act(y_r.shape[1]),
                  wgt(w_a.shape[0]), wgt(w_l.shape[0]), wgt(w_r.shape[0]),
                  gate(0), gate(1), gate(2)],
        out_specs=pl.BlockSpec((tm, tn), lambda i, j: (i, j)),
        out_shape=jax.ShapeDtypeStruct((m, n), BF16),
        compiler_params=_cparams("parallel", "parallel"),
        name="merge_branches",
    )(y_a, y_l, y_r, w_a, w_l, w_r, proj, proj, proj)


def _ffn_up_kernel(x_ref, wg_ref, wv_ref, cw_ref, cb_ref, o_ref, wgb_ref, wvb_ref, halo_ref,
                   *, tiles_per_seq, valid_tiles):
    j = pl.program_id(0)
    i = pl.program_id(1)

    @pl.when(i == 0)
    def _():
        wgb_ref[...] = wg_ref[...].astype(BF16)
        wvb_ref[...] = wv_ref[...].astype(BF16)

    @pl.when(i % tiles_per_seq == 0)
    def _():
        halo_ref[...] = jnp.zeros_like(halo_ref)

    @pl.when(j < valid_tiles)
    def _():
        x = x_ref[...]
        g = jnp.dot(x, wgb_ref[...], preferred_element_type=F32)
        tm = g.shape[0]
        tail = halo_ref[...]
        c = g * cw_ref[FFN_CONV - 1:FFN_CONV, :] + cb_ref[...]
        for sh in range(1, FFN_CONV):
            c = c + _shift_rows(g, sh, tail) * cw_ref[FFN_CONV - 1 - sh:FFN_CONV - sh, :]
        halo_ref[...] = g[tm - SUBLANES:tm, :]
        v = jnp.dot(x, wvb_ref[...], preferred_element_type=F32)
        o_ref[...] = (_gelu_tanh(c) * v).astype(o_ref.dtype)

    @pl.when(j >= valid_tiles)
    def _():
        o_ref[...] = jnp.zeros_like(o_ref)


def ffn_up(x, w, conv_w, conv_b, layer, *, seq, tm, tn, dff_pad):
    m, k = x.shape
    dff = conv_w.shape[-1]
    valid = dff // tn
    last = valid - 1
    return pl.pallas_call(
        functools.partial(_ffn_up_kernel, tiles_per_seq=seq // tm, valid_tiles=valid),
        grid=(dff_pad // tn, m // tm),
        in_specs=[pl.BlockSpec((tm, k), lambda j, i: (i, 0)),
                  pl.BlockSpec((None, k, tn), lambda j, i: (layer, 0, jnp.minimum(j, last))),
                  pl.BlockSpec((None, k, tn), lambda j, i: (layer, 0, valid + jnp.minimum(j, last))),
                  pl.BlockSpec((None, FFN_CONV, tn), lambda j, i: (layer, 0, jnp.minimum(j, last))),
                  pl.BlockSpec((None, 1, tn), lambda j, i: (layer, 0, jnp.minimum(j, last)))],
        out_specs=pl.BlockSpec((tm, tn), lambda j, i: (i, j)),
        out_shape=jax.ShapeDtypeStruct((m, dff_pad), BF16),
        scratch_shapes=[pltpu.VMEM((k, tn), BF16), pltpu.VMEM((k, tn), BF16),
                        pltpu.VMEM((SUBLANES, tn), F32)],
        compiler_params=_cparams("arbitrary", "arbitrary"),
        name="ffn_up",
    )(x, w, w, conv_w, conv_b)


def _rotate_half_cols(w):
    half = MLA_ROPE // 2
    return jnp.concatenate([-w[..., half:], w[..., :half]], axis=-1)


SRC_CKV = MLA_Q_RANK
SRC_KR = SRC_CKV + MLA_KV_RANK
SRC_LRU = SRC_KR + MLA_ROPE
SRC_RW = SRC_LRU + 2 * LRU_WIDTH
SRC_XWA = SRC_RW + 3 * RW_WIDTH
SRC_XG = SRC_XWA + RW_DECAY_LORA + RW_AAA_LORA
SRC_GATES = SRC_XG + RW_GATE_LORA


def _w_in_kernel(s_ref, o_ref):
    rows = s_ref.shape[0]

    def copy(dst, src, width):
        o_ref[:, dst:dst + width] = s_ref[:, src:src + width].astype(BF16)

    copy(OFF_GATES, SRC_GATES, 3 * D_MODEL)
    copy(OFF_LRU_X, SRC_LRU, 2 * LRU_WIDTH)
    copy(OFF_RW_R, SRC_RW, 3 * RW_WIDTH)
    copy(OFF_CQ, 0, MLA_Q_RANK)
    o_ref[:, OFF_CQ + MLA_Q_RANK:OFF_CQ + Q_PAD] = jnp.zeros((rows, Q_PAD - MLA_Q_RANK), BF16)
    copy(OFF_CKV, SRC_CKV, MLA_KV_RANK)
    half = MLA_ROPE // 2
    t = s_ref[:, SRC_KR:SRC_KR + LANES]
    lane = lax.broadcasted_iota(jnp.int32, t.shape, 1)
    rot = jnp.where(lane < MLA_ROPE + half, -pltpu.roll(t, half, axis=1),
                    pltpu.roll(t, MLA_ROPE + half, axis=1))
    o_ref[:, OFF_KR:OFF_KR + LANES] = jnp.where(lane < MLA_ROPE, t, rot).astype(BF16)
    copy(OFF_XWA, SRC_XWA, RW_DECAY_LORA + RW_AAA_LORA)
    xg = s_ref[:, SRC_XG:SRC_XG + XG_PAD]
    lane2 = lax.broadcasted_iota(jnp.int32, xg.shape, 1)
    o_ref[:, OFF_XG:OFF_XG + XG_PAD] = jnp.where(lane2 < RW_GATE_LORA, xg, 0.0).astype(BF16)


def relayout_w_in(w, layer, *, tr):
    _, d, n = w.shape
    return pl.pallas_call(
        _w_in_kernel,
        grid=(d // tr,),
        in_specs=[pl.BlockSpec((None, tr, n), lambda i: (layer, i, 0))],
        out_specs=pl.BlockSpec((tr, IN_COLS_PAD), lambda i: (i, 0)),
        out_shape=jax.ShapeDtypeStruct((d, IN_COLS_PAD), BF16),
        compiler_params=_cparams("parallel"),
        name="w_in_relayout",
    )(w)


def _cast_kernel(s_ref, o_ref, *, valid_blocks):
    i = pl.program_id(0)

    @pl.when(i < valid_blocks)
    def _():
        o_ref[...] = s_ref[...].astype(o_ref.dtype)

    @pl.when(i >= valid_blocks)
    def _():
        o_ref[...] = jnp.zeros_like(o_ref)


def cast_bf16(w, layer=None, *, tr, rows_out=None):
    r, c = w.shape[-2:]
    rows_out = r if rows_out is None else rows_out
    valid = r // tr
    if layer is None:
        src = pl.BlockSpec((tr, c), lambda i: (jnp.minimum(i, valid - 1), 0))
    else:
        src = pl.BlockSpec((None, tr, c), lambda i: (layer, jnp.minimum(i, valid - 1), 0))
    return pl.pallas_call(
        functools.partial(_cast_kernel, valid_blocks=valid),
        grid=(rows_out // tr,),
        in_specs=[src],
        out_specs=pl.BlockSpec((tr, c), lambda i: (i, 0)),
        out_shape=jax.ShapeDtypeStruct((rows_out, c), BF16),
        compiler_params=_cparams("parallel"),
        name="cast_bf16",
    )(w)


def _prep_w_uq(w):
    rope = w[..., MLA_NOPE:]
    w = jnp.concatenate([w, _rotate_half_cols(rope)], axis=-1)
    w = w.reshape(MLA_Q_RANK, MLA_HEADS * 2 * LANES)
    return jnp.pad(w, ((0, Q_PAD - MLA_Q_RANK), (0, 0))).astype(BF16)


def _rope_table(positions):
    inv_freq = ROPE_THETA ** (-jnp.arange(0, MLA_ROPE, 2, dtype=F32) / MLA_ROPE)
    ang = positions.astype(F32)[..., None] * inv_freq
    cos, sin = jnp.cos(ang), jnp.sin(ang)
    return jnp.concatenate([cos, cos, sin, sin], axis=-1).reshape(-1, 2 * MLA_ROPE)


class _Tiles:
    in_proj = (1024, 1024)
    mla_up_rows = 512
    attn_q = 512
    lru_rows = 512
    rwkv_prep_rows = 256
    merge = (512, 1024)
    out_mm = (1024, 1024, 1024)
    ln_rows = 256
    ffn_up = (1024, 256)
    w_in_rows = 64
    cast_rows = 256


def kernel(x, positions, w_in, mla_q_norm, mla_w_uq, mla_kv_norm, mla_w_ukv, lru_conv_w, lru_conv_b, lru_w_a, lru_b_a, lru_w_x, lru_b_x, lru_lambda, rw_mu, rw_w0, rw_w2, rw_a0, rw_a2, rw_g2, rw_k_k, rw_k_a, rw_r_k, rw_gn_g, rw_gn_b, w_o_mla, w_o_lru, w_o_rwkv, w_out, ln1_g, ln1_b, ffn_w_up, ffn_conv_w, ffn_conv_b, ffn_w_down, ln2_g, ln2_b):
    batch, seq, d = x.shape
    m = batch * seq
    t = _Tiles
    cs = _rope_table(positions)
    xf = x.reshape(m, d)
    cast = functools.partial(cast_bf16, tr=t.cast_rows)
    xb = cast(xf)
    w_ukv_all = mla_w_ukv.reshape(DEPTH, MLA_KV_RANK, -1)
    conv_b_all = ffn_conv_b.reshape(DEPTH, 1, -1)
    for l in range(DEPTH):
        proj = matmul(xb, relayout_w_in(w_in, l, tr=t.w_in_rows), tm=t.in_proj[0], tn=t.in_proj[1])

        q_gain = jnp.pad(mla_q_norm[l], (0, Q_PAD - MLA_Q_RANK)).reshape(1, Q_PAD)
        q = mla_q_up(proj, q_gain, _prep_w_uq(mla_w_uq[l]), cs, batch=batch, seq=seq, tm=t.mla_up_rows)
        k, v = mla_kv_up(proj, mla_kv_norm[l].reshape(1, -1), cast(w_ukv_all, l), cs, batch=batch,
                         seq=seq, tm=t.mla_up_rows)
        y_a = mla_attention(q, k, v, tq=t.attn_q).reshape(m, -1)

        y_l = rglru_branch(proj, lru_conv_w[l], lru_conv_b[l], lru_w_a[l], lru_b_a[l],
                           lru_w_x[l], lru_b_x[l], lru_lambda[l], batch=batch, seq=seq, ts=t.lru_rows)

        rw = rwkv_prep(proj, rw_mu[l], rw_w0[l], rw_w2[l], rw_a0[l], rw_a2[l], rw_g2[l],
                       rw_k_k[l], rw_k_a[l], batch=batch, seq=seq, ts=t.rwkv_prep_rows)
        y_r = rwkv_recurrence(*rw, rw_r_k[l], rw_gn_g[l], rw_gn_b[l], batch=batch, seq=seq)
        y_r = y_r.reshape(m, RW_WIDTH)

        merged = merge_branches(y_a, y_l, y_r, cast(w_o_mla, l), cast(w_o_lru, l), cast(w_o_rwkv, l),
                                proj, tm=t.merge[0], tn=t.merge[1])
        mm = dict(tm=t.out_mm[0], tn=t.out_mm[1], tk=t.out_mm[2], ln_rows=t.ln_rows)
        xf, xb = matmul_residual_layernorm(merged, cast(w_out, l), xf, ln1_g[l], ln1_b[l],
                                           name="mixer_out", **mm)

        h = ffn_up(xb, ffn_w_up, ffn_conv_w, conv_b_all, l, seq=seq, tm=t.ffn_up[0],
                   tn=t.ffn_up[1], dff_pad=D_FF_PAD)
        xf, xb = matmul_residual_layernorm(h, cast(ffn_w_down, l, rows_out=D_FF_PAD), xf, ln2_g[l],
                                           ln2_b[l], name="ffn_down", **mm)
    return xf.reshape(batch, seq, d)
```

```python
import functools

import jax
import jax.numpy as jnp
from jax import lax
from jax.experimental import pallas as pl
from jax.experimental.pallas import tpu as pltpu

F32 = jnp.float32
BF16 = jnp.bfloat16

D_MODEL = 4096
DEPTH = 2
CHUNK = 64
MLA_HEADS = 16
MLA_Q_RANK = 896
MLA_KV_RANK = 512
MLA_NOPE = 128
MLA_ROPE = 64
MLA_V = 128
ROPE_THETA = 10000.0
LRU_WIDTH = 1024
LRU_BLOCKS = 8
LRU_BLOCK = LRU_WIDTH // LRU_BLOCKS
LRU_CONV = 4
LRU_C = 8.0
RW_WIDTH = 1024
RW_HEAD = 64
RW_HEADS = RW_WIDTH // RW_HEAD
RW_DECAY_LORA = 64
RW_AAA_LORA = 64
RW_GATE_LORA = 160
D_FF = 11008
FFN_CONV = 3
ALPHA = (2 * DEPTH) ** 0.25
LN_EPS = 1e-5
RMS_EPS = 1e-6
GN_EPS = 64e-5

LANES = 128
SUBLANES = 8
VMEM_LIMIT_BYTES = 56 * 1024 * 1024

Q_PAD = 1024
D_FF_PAD = 11264
XG_PAD = 256
OFF_GATES = 0
OFF_LRU_X = 3 * D_MODEL
OFF_LRU_G = OFF_LRU_X + LRU_WIDTH
OFF_RW_R = OFF_LRU_G + LRU_WIDTH
OFF_RW_K = OFF_RW_R + RW_WIDTH
OFF_RW_V = OFF_RW_K + RW_WIDTH
OFF_CQ = OFF_RW_V + RW_WIDTH
OFF_CKV = OFF_CQ + Q_PAD
OFF_KR = OFF_CKV + MLA_KV_RANK
OFF_XWA = OFF_KR + 2 * MLA_ROPE
OFF_XG = OFF_XWA + RW_DECAY_LORA + RW_AAA_LORA
IN_COLS_PAD = OFF_XG + XG_PAD

RW_GROUP_PAIRS = 4
_SQRT_2_OVER_PI = 0.7978845608028654
_LOG2_E = 1.4426950408889634


def _cparams(*sem):
    return pltpu.CompilerParams(dimension_semantics=sem, vmem_limit_bytes=VMEM_LIMIT_BYTES)


def _gelu_tanh(x):
    return 0.5 * x * (1.0 + jnp.tanh(_SQRT_2_OVER_PI * (x + 0.044715 * (x * x * x))))


def _softplus(z):
    return jnp.maximum(z, 0.0) + jnp.log1p(jnp.exp(-jnp.abs(z)))


def _shift_rows(x, shift, tail):
    row = lax.broadcasted_iota(jnp.int32, x.shape, 0)
    xs = pltpu.roll(x, shift, axis=0)
    for j in range(shift):
        src = SUBLANES - shift + j
        xs = jnp.where(row == j, tail[src:src + 1, :], xs)
    return xs


def _group_ones(rows, cols, group):
    shift = group.bit_length() - 1
    r = lax.broadcasted_iota(jnp.int32, (rows, cols), 0)
    c = lax.broadcasted_iota(jnp.int32, (rows, cols), 1)
    return jnp.where(((r & (cols - 1)) >> shift) == (c >> shift), 1.0, 0.0).astype(BF16)


def _head_sum(x, ones_hl):
    hi = x.astype(BF16)
    lo = (x - hi.astype(F32)).astype(BF16)
    return jnp.dot(jnp.concatenate([hi, lo], axis=1), ones_hl, preferred_element_type=F32)


def _mm_kernel(a_ref, b_ref, o_ref):
    o_ref[...] = lax.dot_general(a_ref[...], b_ref[...], (((1,), (1,)), ((), ())),
                                 preferred_element_type=F32).astype(o_ref.dtype)


def matmul_nt(a, b, *, tm, tn, out_dtype=F32):
    m, k = a.shape
    n, _ = b.shape
    return pl.pallas_call(
        _mm_kernel,
        grid=(m // tm, n // tn),
        in_specs=[pl.BlockSpec((tm, k), lambda i, j: (i, 0)),
                  pl.BlockSpec((tn, k), lambda i, j: (j, 0))],
        out_specs=pl.BlockSpec((tm, tn), lambda i, j: (i, j)),
        out_shape=jax.ShapeDtypeStruct((m, n), out_dtype),
        compiler_params=_cparams("parallel", "parallel"),
        name="in_proj",
    )(a, b)


def _mm_res_kernel(a_ref, w_ref, x_ref, o_ref, acc_ref):
    k = pl.program_id(2)

    @pl.when(k == 0)
    def _():
        acc_ref[...] = ALPHA * x_ref[...]

    acc_ref[...] += jnp.dot(a_ref[...], w_ref[...], preferred_element_type=F32)

    @pl.when(k == pl.num_programs(2) - 1)
    def _():
        o_ref[...] = acc_ref[...]


def _ln_kernel(y_ref, g_ref, b_ref, o_ref, ob_ref):
    y = y_ref[...]
    mu = jnp.mean(y, axis=-1, keepdims=True)
    yc = y - mu
    var = jnp.mean(yc * yc, axis=-1, keepdims=True)
    o = yc * lax.rsqrt(var + LN_EPS) * g_ref[...] + b_ref[...]
    o_ref[...] = o
    ob_ref[...] = o.astype(BF16)


def matmul_residual_layernorm(a, w, x, g, b, *, tm, tn, tk, ln_rows, name):
    m, k = a.shape
    n = w.shape[1]
    y = pl.pallas_call(
        _mm_res_kernel,
        grid=(m // tm, n // tn, k // tk),
        in_specs=[pl.BlockSpec((tm, tk), lambda i, j, kk: (i, kk)),
                  pl.BlockSpec((tk, tn), lambda i, j, kk: (kk, j)),
                  pl.BlockSpec((tm, tn), lambda i, j, kk: (i, j))],
        out_specs=pl.BlockSpec((tm, tn), lambda i, j, kk: (i, j)),
        out_shape=jax.ShapeDtypeStruct((m, n), F32),
        scratch_shapes=[pltpu.VMEM((tm, tn), F32)],
        compiler_params=_cparams("parallel", "parallel", "arbitrary"),
        name=name,
    )(a, w, x)
    return pl.pallas_call(
        _ln_kernel,
        grid=(m // ln_rows,),
        in_specs=[pl.BlockSpec((ln_rows, n), lambda i: (i, 0)),
                  pl.BlockSpec((1, n), lambda i: (0, 0)),
                  pl.BlockSpec((1, n), lambda i: (0, 0))],
        out_specs=[pl.BlockSpec((ln_rows, n), lambda i: (i, 0)),
                   pl.BlockSpec((ln_rows, n), lambda i: (i, 0))],
        out_shape=[jax.ShapeDtypeStruct((m, n), F32), jax.ShapeDtypeStruct((m, n), BF16)],
        compiler_params=_cparams("parallel"),
        name="layernorm",
    )(y, g.reshape(1, n), b.reshape(1, n))


def _rope_half(block, cs):
    p = block * cs
    return p + pltpu.roll(p, MLA_ROPE, axis=1)


def _q_up_kernel(p_ref, g_ref, w_ref, cs_ref, q_ref, *, rank):
    x = p_ref[...]
    ms = jnp.sum(x * x, axis=-1, keepdims=True) * (1.0 / rank)
    xn = (x * lax.rsqrt(ms + RMS_EPS) * g_ref[...]).astype(BF16)
    cs = cs_ref[...]
    hw = 2 * LANES
    for h in range(q_ref.shape[1]):
        acc = jnp.dot(xn, w_ref[:, h * hw:(h + 1) * hw], preferred_element_type=F32)
        rot = _rope_half(acc[:, LANES:hw], cs)
        q_ref[0, h, :, 0:MLA_NOPE] = acc[:, 0:MLA_NOPE].astype(BF16)
        q_ref[0, h, :, MLA_NOPE:MLA_NOPE + MLA_ROPE] = rot[:, 0:MLA_ROPE].astype(BF16)


def mla_q_up(proj, gain, w, cs, *, batch, seq, tm):
    nh = MLA_HEADS
    spt = seq // tm
    return pl.pallas_call(
        functools.partial(_q_up_kernel, rank=MLA_Q_RANK),
        grid=(batch * spt,),
        in_specs=[pl.BlockSpec((tm, Q_PAD), lambda i: (i, OFF_CQ // Q_PAD)),
                  pl.BlockSpec((1, Q_PAD), lambda i: (0, 0)),
                  pl.BlockSpec(w.shape, lambda i: (0, 0)),
                  pl.BlockSpec((tm, LANES), lambda i: (i, 0))],
        out_specs=pl.BlockSpec((1, nh, tm, MLA_NOPE + MLA_ROPE), lambda i: (i // spt, 0, i % spt, 0)),
        out_shape=jax.ShapeDtypeStruct((batch, nh, seq, MLA_NOPE + MLA_ROPE), BF16),
        compiler_params=_cparams("parallel"),
        name="mla_q_up",
    )(proj, gain, w, cs)


def _kv_up_kernel(p_ref, kr_ref, g_ref, w_ref, cs_ref, k_ref, v_ref, *, rank):
    x = p_ref[...]
    ms = jnp.sum(x * x, axis=-1, keepdims=True) * (1.0 / rank)
    xn = (x * lax.rsqrt(ms + RMS_EPS) * g_ref[...]).astype(BF16)
    krope = _rope_half(kr_ref[...], cs_ref[...])[:, 0:MLA_ROPE].astype(BF16)
    hw = MLA_NOPE + MLA_V
    for h in range(k_ref.shape[1]):
        acc = jnp.dot(xn, w_ref[:, h * hw:(h + 1) * hw], preferred_element_type=F32)
        k_ref[0, h, :, 0:MLA_NOPE] = acc[:, 0:MLA_NOPE].astype(BF16)
        k_ref[0, h, :, MLA_NOPE:MLA_NOPE + MLA_ROPE] = krope
        v_ref[0, h] = acc[:, MLA_NOPE:hw].astype(BF16)


def mla_kv_up(proj, gain, w, cs, *, batch, seq, tm):
    nh = MLA_HEADS
    spt = seq // tm
    dk = MLA_NOPE + MLA_ROPE
    return pl.pallas_call(
        functools.partial(_kv_up_kernel, rank=MLA_KV_RANK),
        grid=(batch * spt,),
        in_specs=[pl.BlockSpec((tm, MLA_KV_RANK), lambda i: (i, OFF_CKV // MLA_KV_RANK)),
                  pl.BlockSpec((tm, LANES), lambda i: (i, OFF_KR // LANES)),
                  pl.BlockSpec((1, MLA_KV_RANK), lambda i: (0, 0)),
                  pl.BlockSpec(w.shape, lambda i: (0, 0)),
                  pl.BlockSpec((tm, LANES), lambda i: (i, 0))],
        out_specs=[pl.BlockSpec((1, nh, tm, dk), lambda i: (i // spt, 0, i % spt, 0)),
                   pl.BlockSpec((1, nh, tm, MLA_V), lambda i: (i // spt, 0, i % spt, 0))],
        out_shape=[jax.ShapeDtypeStruct((batch, nh, seq, dk), BF16),
                   jax.ShapeDtypeStruct((batch, nh, seq, MLA_V), BF16)],
        compiler_params=_cparams("parallel"),
        name="mla_kv_up",
    )(proj, proj, gain, w, cs)


def _flash_kernel(q_ref, k_ref, v_ref, o_ref, *, tq, scale):
    qi = pl.program_id(2)
    q = q_ref[0, 0]

    def scores(j):
        k = k_ref[0, 0, pl.ds(pl.multiple_of(j * tq, tq), tq), :]
        return lax.dot_general(q, k, (((1,), (1,)), ((), ())),
                               preferred_element_type=F32) * (scale * _LOG2_E)

    def update(j, s, m, l, acc):
        v = v_ref[0, 0, pl.ds(pl.multiple_of(j * tq, tq), tq), :]
        m_new = jnp.maximum(m, jnp.max(s, axis=-1, keepdims=True))
        alpha = jnp.exp2(m - m_new)
        p = jnp.exp2(s - m_new)
        l = alpha * l + jnp.sum(p, axis=-1, keepdims=True)
        acc = alpha * acc + jnp.dot(p.astype(BF16), v, preferred_element_type=F32)
        return m_new, l, acc

    def pair(jj, carry):
        s_a, s_b = scores(2 * jj), scores(2 * jj + 1)
        return update(2 * jj + 1, s_b, *update(2 * jj, s_a, *carry))

    def single(_, carry):
        return update(qi - 1, scores(qi - 1), *carry)

    carry = (jnp.full((tq, 1), -1e30, F32), jnp.zeros((tq, 1), F32), jnp.zeros((tq, MLA_V), F32))
    carry = lax.fori_loop(0, qi >> 1, pair, carry)
    carry = lax.fori_loop(0, qi & 1, single, carry)
    shift = CHUNK.bit_length() - 1
    qc = lax.broadcasted_iota(jnp.int32, (tq, tq), 0) >> shift
    kc = lax.broadcasted_iota(jnp.int32, (tq, tq), 1) >> shift
    m, l, acc = update(qi, jnp.where(kc <= qc, scores(qi), -jnp.inf), *carry)
    o_ref[0] = (acc / l).astype(o_ref.dtype)


def mla_attention(q, k, v, *, tq):
    batch, nh, seq, dk = q.shape
    scale = (MLA_NOPE + MLA_ROPE) ** -0.5
    return pl.pallas_call(
        functools.partial(_flash_kernel, tq=tq, scale=scale),
        grid=(batch, nh, seq // tq),
        in_specs=[pl.BlockSpec((1, 1, tq, dk), lambda b, h, i: (b, h, i, 0)),
                  pl.BlockSpec((1, 1, seq, dk), lambda b, h, i: (b, h, 0, 0)),
                  pl.BlockSpec((1, 1, seq, MLA_V), lambda b, h, i: (b, h, 0, 0))],
        out_specs=pl.BlockSpec((1, tq, MLA_V), lambda b, h, i: (b, i, h)),
        out_shape=jax.ShapeDtypeStruct((batch, seq, nh * MLA_V), BF16),
        compiler_params=_cparams("parallel", "parallel", "arbitrary"),
        name="mla_attention",
    )(q, k, v)


def _lru_kernel(x_ref, g_ref, cw_ref, cb_ref, wa_ref, ba_ref, wx_ref, bx_ref, lam_ref, o_ref,
                tail_ref, h_ref, a_scr, u_scr):
    @pl.when(pl.program_id(1) == 0)
    def _():
        tail_ref[...] = jnp.zeros_like(tail_ref)
        h_ref[...] = jnp.zeros_like(h_ref)

    x = x_ref[...]
    ts = x.shape[0]
    tail = tail_ref[...]
    xc = x * cw_ref[LRU_CONV - 1:LRU_CONV, :] + cb_ref[...]
    for sh in range(1, LRU_CONV):
        xc = xc + _shift_rows(x, sh, tail) * cw_ref[LRU_CONV - 1 - sh:LRU_CONV - sh, :]
    tail_ref[...] = x[ts - SUBLANES:ts, :]

    xb = xc.astype(BF16)
    ra, rx = [], []
    for n in range(LRU_BLOCKS):
        blk = xb[:, n * LRU_BLOCK:(n + 1) * LRU_BLOCK]
        ra.append(jnp.dot(blk, wa_ref[n], preferred_element_type=F32))
        rx.append(jnp.dot(blk, wx_ref[n], preferred_element_type=F32))
    r = jax.nn.sigmoid(jnp.concatenate(ra, axis=1) + ba_ref[...])
    gi = jax.nn.sigmoid(jnp.concatenate(rx, axis=1) + bx_ref[...])
    log_a = -LRU_C * r * _softplus(-lam_ref[...])
    a_scr[...] = jnp.exp(log_a)
    u_scr[...] = jnp.sqrt(1.0 - jnp.exp(2.0 * log_a)) * (gi * xc)

    row8 = lax.broadcasted_iota(jnp.int32, (SUBLANES, x.shape[1]), 0)

    def body(grp, h):
        base = pl.multiple_of(grp * SUBLANES, SUBLANES)
        a8 = a_scr[pl.ds(base, SUBLANES), :]
        u8 = u_scr[pl.ds(base, SUBLANES), :]
        hs = u8
        for s in range(SUBLANES):
            h = a8[s:s + 1, :] * h + u8[s:s + 1, :]
            hs = jnp.where(row8 == s, h, hs)
        u_scr[pl.ds(base, SUBLANES), :] = hs
        return h

    h_ref[...] = lax.fori_loop(0, ts // SUBLANES, body, h_ref[...])
    o_ref[...] = (u_scr[...] * _gelu_tanh(g_ref[...])).astype(o_ref.dtype)


def rglru_branch(proj, conv_w, conv_b, w_a, b_a, w_x, b_x, lam, *, batch, seq, ts):
    w = LRU_WIDTH
    spt = seq // ts
    vec = lambda: pl.BlockSpec((1, w), lambda b, s: (0, 0))
    blockdiag = lambda: pl.BlockSpec((LRU_BLOCKS, LRU_BLOCK, LRU_BLOCK), lambda b, s: (0, 0, 0))
    return pl.pallas_call(
        _lru_kernel,
        grid=(batch, spt),
        in_specs=[pl.BlockSpec((ts, w), lambda b, s: (b * spt + s, OFF_LRU_X // w)),
                  pl.BlockSpec((ts, w), lambda b, s: (b * spt + s, OFF_LRU_G // w)),
                  pl.BlockSpec((LRU_CONV, w), lambda b, s: (0, 0)),
                  vec(), blockdiag(), vec(), blockdiag(), vec(), vec()],
        out_specs=pl.BlockSpec((ts, w), lambda b, s: (b * spt + s, 0)),
        out_shape=jax.ShapeDtypeStruct((batch * seq, w), BF16),
        scratch_shapes=[pltpu.VMEM((SUBLANES, w), F32), pltpu.VMEM((1, w), F32),
                        pltpu.VMEM((ts, w), F32), pltpu.VMEM((ts, w), F32)],
        compiler_params=_cparams("parallel", "arbitrary"),
        name="rglru",
    )(proj, proj, conv_w, conv_b.reshape(1, w), w_a.astype(BF16), b_a.reshape(1, w),
      w_x.astype(BF16), b_x.reshape(1, w), lam.reshape(1, w))


def _rwkv_prep_kernel(r_ref, k_ref, v_ref, wa_ref, xg_ref,
                      mur_ref, muk_ref, muv_ref, muwa_ref, mug_ref,
                      w0_ref, w2_ref, a0_ref, a2_ref, g2_ref, kkw_ref, ka_ref,
                      ro_ref, ko_ref, vo_ref, nkk_ref, b_ref, d_ref, g_ref,
                      tr_ref, tk_ref, tv_ref, twa_ref, tg_ref):
    @pl.when(pl.program_id(1) == 0)
    def _():
        for t in (tr_ref, tk_ref, tv_ref, twa_ref, tg_ref):
            t[...] = jnp.zeros_like(t)

    def mix(x_ref, mu_ref, tail_ref):
        x = x_ref[...]
        prev = _shift_rows(x, 1, tail_ref[...])
        tail_ref[...] = x[x.shape[0] - SUBLANES:, :]
        return x + mu_ref[...] * (prev - x)

    r = mix(r_ref, mur_ref, tr_ref)
    k = mix(k_ref, muk_ref, tk_ref)
    v = mix(v_ref, muv_ref, tv_ref)
    xwa = mix(wa_ref, muwa_ref, twa_ref)
    xg = mix(xg_ref, mug_ref, tg_ref)
    w = -_softplus(-(w0_ref[...] + jnp.dot(jnp.tanh(xwa).astype(BF16), w2_ref[...],
                                           preferred_element_type=F32))) - 0.5
    a = jax.nn.sigmoid(a0_ref[...] + jnp.dot(xwa.astype(BF16), a2_ref[...], preferred_element_type=F32))
    g = jnp.dot(jax.nn.sigmoid(xg).astype(BF16), g2_ref[...], preferred_element_type=F32)

    kk = k * kkw_ref[...]
    ones_hl = _group_ones(2 * LANES, LANES, RW_HEAD)
    sq = kk * kk
    ss = jnp.concatenate([_head_sum(sq[:, c * LANES:(c + 1) * LANES], ones_hl)
                          for c in range(RW_WIDTH // LANES)], axis=1)
    kk = kk / jnp.maximum(jnp.sqrt(ss), 1e-12)

    ro_ref[...] = r
    ko_ref[...] = k * (1.0 + (a - 1.0) * ka_ref[...])
    vo_ref[...] = v
    nkk_ref[...] = -kk
    b_ref[...] = kk * a
    d_ref[...] = jnp.exp(-jnp.exp(w))
    g_ref[...] = g


def rwkv_prep(proj, mu, w0, w2, a0, a2, g2, k_k, k_a, *, batch, seq, ts):
    w = RW_WIDTH
    spt = seq // ts
    lora = RW_DECAY_LORA + RW_AAA_LORA
    row = lambda width, col: pl.BlockSpec((ts, width), lambda b, s: (b * spt + s, col))
    vec = lambda width: pl.BlockSpec((1, width), lambda b, s: (0, 0))
    mat = lambda rows: pl.BlockSpec((rows, w), lambda b, s: (0, 0))
    mu_r, mu_k, mu_v = (mu[i * w:(i + 1) * w].reshape(1, w) for i in range(3))
    mu_wa = mu[3 * w:3 * w + lora].reshape(1, lora)
    mu_g = jnp.pad(mu[3 * w + lora:], (0, XG_PAD - RW_GATE_LORA)).reshape(1, XG_PAD)
    zeros = jnp.zeros((RW_DECAY_LORA, w), w2.dtype)
    w2p = jnp.concatenate([w2, zeros], axis=0).astype(BF16)
    a2p = jnp.concatenate([zeros, a2], axis=0).astype(BF16)
    g2p = jnp.pad(g2, ((0, XG_PAD - RW_GATE_LORA), (0, 0))).astype(BF16)
    out = jax.ShapeDtypeStruct((batch * seq, w), F32)
    return pl.pallas_call(
        _rwkv_prep_kernel,
        grid=(batch, spt),
        in_specs=[row(w, OFF_RW_R // w), row(w, OFF_RW_K // w), row(w, OFF_RW_V // w),
                  row(lora, OFF_XWA // lora), row(XG_PAD, OFF_XG // XG_PAD),
                  vec(w), vec(w), vec(w), vec(lora), vec(XG_PAD),
                  vec(w), mat(lora), vec(w), mat(lora), mat(XG_PAD), vec(w), vec(w)],
        out_specs=[pl.BlockSpec((ts, w), lambda b, s: (b * spt + s, 0))] * 7,
        out_shape=[out] * 7,
        scratch_shapes=[pltpu.VMEM((SUBLANES, w), F32)] * 3
                       + [pltpu.VMEM((SUBLANES, lora), F32), pltpu.VMEM((SUBLANES, XG_PAD), F32)],
        compiler_params=_cparams("parallel", "arbitrary"),
        name="rwkv_prep",
    )(proj, proj, proj, proj, proj, mu_r, mu_k, mu_v, mu_wa, mu_g,
      w0.reshape(1, w), w2p, a0.reshape(1, w), a2p, g2p, k_k.reshape(1, w), k_a.reshape(1, w))


def _rwkv_rec_kernel(r_ref, k_ref, v_ref, nkk_ref, b_ref, d_ref, g_ref, rk_ref, gng_ref, gnb_ref,
                     y_ref, st_ref, q_ref, ot_ref):
    nb, tau, width = r_ref.shape
    npair = width // LANES
    hd = RW_HEAD
    pairs = [(bi, p) for bi in range(nb) for p in range(npair)]
    flat = [(i, bi, p) for i, (bi, p) in enumerate(pairs)]
    groups = [flat[j:j + RW_GROUP_PAIRS] for j in range(0, len(flat), RW_GROUP_PAIRS)]

    @pl.when(pl.program_id(0) == 0)
    def _():
        st_ref[...] = jnp.zeros_like(st_ref)

    q_ref[...] = jnp.zeros_like(q_ref)
    ot_ref[...] = jnp.zeros_like(ot_ref)

    row = lax.broadcasted_iota(jnp.int32, (hd, LANES), 0)
    lane = lax.broadcasted_iota(jnp.int32, (hd, LANES), 1)
    lane_in_head = lane & (hd - 1)
    diag = lane_in_head == row
    ones2 = _group_ones(2 * LANES, 2 * LANES, hd)

    def group(grp, carry):
        base = pl.multiple_of(grp * SUBLANES, SUBLANES)
        tile = lambda ref, bi, p: ref[bi, pl.ds(base, SUBLANES), pl.ds(p * LANES, LANES)]
        vcb = [None] * len(groups)
        for s in range(SUBLANES):
            sel = lane_in_head == (base + s - 1)
            for gi, grp_pairs in enumerate(groups):
                lhs = []
                for i, bi, p in grp_pairs:
                    pm = (st_ref[bi, p] * tile(nkk_ref, bi, p)[s:s + 1, :]).astype(BF16)
                    lhs.append(jnp.concatenate([pm, q_ref[pl.ds(i * hd, hd), :]], axis=1))
                if s % 2 == 0:
                    for i, bi, p in grp_pairs:
                        v8 = tile(v_ref, bi, p)
                        lhs.append(jnp.concatenate(
                            [jnp.where(diag, v8[s:s + 1, :], 0.0).astype(BF16),
                             jnp.where(diag, v8[s + 1:s + 2, :], 0.0).astype(BF16)], axis=1))
                out = jnp.dot(jnp.concatenate(lhs, axis=0), ones2, preferred_element_type=F32)
                if s % 2 == 0:
                    vcb[gi] = out[len(grp_pairs) * hd:, :]
                for j, (i, bi, p) in enumerate(grp_pairs):
                    rows = slice(j * hd, (j + 1) * hd)
                    ot_ref[bi, p] = jnp.where(sel, out[rows, LANES:], ot_ref[bi, p])
                    vc = vcb[gi][rows, 0:LANES] if s % 2 == 0 else vcb[gi][rows, LANES:]
                    new = (st_ref[bi, p] * tile(d_ref, bi, p)[s:s + 1, :]
                           + out[rows, 0:LANES] * tile(b_ref, bi, p)[s:s + 1, :]
                           + vc * tile(k_ref, bi, p)[s:s + 1, :])
                    st_ref[bi, p] = new
                    q_ref[pl.ds(i * hd, hd), :] = (new * tile(r_ref, bi, p)[s:s + 1, :]).astype(BF16)
        return carry

    lax.fori_loop(0, tau // SUBLANES, group, 0)

    o_last = jnp.dot(q_ref[...], ones2[0:LANES, 0:LANES], preferred_element_type=F32)
    sel = lane_in_head == (tau - 1)
    for i, (bi, p) in enumerate(pairs):
        ot_ref[bi, p] = jnp.where(sel, o_last[i * hd:(i + 1) * hd, :], ot_ref[bi, p])

    ones_hl = _group_ones(2 * LANES, LANES, hd)
    first_head = lane < hd
    inv_hd = 1.0 / hd
    for bi in range(nb):
        nat = []
        for p0 in range(0, npair, 2):
            tr = jnp.concatenate([ot_ref[bi, p0], ot_ref[bi, p0 + 1]], axis=0).T
            top, bot = tr[0:hd, :], tr[hd:2 * hd, :]
            nat.append(jnp.where(first_head, top, pltpu.roll(bot, hd, axis=1)))
            nat.append(jnp.where(first_head, pltpu.roll(top, hd, axis=1), bot))
        cols = [pl.ds(p * LANES, LANES) for p in range(npair)]
        o = jnp.concatenate(nat, axis=0)
        mean = _head_sum(o, ones_hl) * inv_hd
        oc = o - mean
        var = _head_sum(oc * oc, ones_hl) * inv_hd
        rkr = jnp.concatenate([r_ref[bi, :, c] * k_ref[bi, :, c] * rk_ref[:, c] for c in cols], axis=0)
        bonus = _head_sum(rkr, ones_hl)
        on = oc * lax.rsqrt(var + GN_EPS)
        for p, c in enumerate(cols):
            rows = slice(p * tau, (p + 1) * tau)
            y = on[rows] * gng_ref[:, c] + gnb_ref[:, c] + bonus[rows] * v_ref[bi, :, c]
            y_ref[bi, :, c] = (y * g_ref[bi, :, c]).astype(y_ref.dtype)


def rwkv_recurrence(r, k, v, nkk, b, d, g, r_k, gn_g, gn_b, *, batch, seq):
    width = RW_WIDTH
    tau = RW_HEAD
    npair = width // LANES
    spec = pl.BlockSpec((batch, tau, width), lambda i: (0, i, 0))
    vec = pl.BlockSpec((1, width), lambda i: (0, 0))
    acts = [a.reshape(batch, seq, width) for a in (r, k, v, nkk, b, d, g)]
    return pl.pallas_call(
        _rwkv_rec_kernel,
        grid=(seq // tau,),
        in_specs=[spec] * 7 + [vec] * 3,
        out_specs=spec,
        out_shape=jax.ShapeDtypeStruct((batch, seq, width), BF16),
        scratch_shapes=[pltpu.VMEM((batch, npair, RW_HEAD, LANES), F32),
                        pltpu.VMEM((batch * npair * RW_HEAD, LANES), BF16),
                        pltpu.VMEM((batch, npair, RW_HEAD, LANES), F32)],
        compiler_params=_cparams("arbitrary"),
        name="rwkv_recurrence",
    )(*acts, r_k.reshape(1, width), gn_g.reshape(1, width), gn_b.reshape(1, width))


def _merge_kernel(a_ref, l_ref, r_ref, wa_ref, wl_ref, wr_ref, ga_ref, gl_ref, gr_ref, o_ref):
    acc = jax.nn.sigmoid(ga_ref[...]) * jnp.dot(a_ref[...], wa_ref[...], preferred_element_type=F32)
    acc += jax.nn.sigmoid(gl_ref[...]) * jnp.dot(l_ref[...], wl_ref[...], preferred_element_type=F32)
    acc += jax.nn.sigmoid(gr_ref[...]) * jnp.dot(r_ref[...], wr_ref[...], preferred_element_type=F32)
    o_ref[...] = acc.astype(o_ref.dtype)


def merge_branches(y_a, y_l, y_r, w_a, w_l, w_r, proj, *, tm, tn):
    m = y_a.shape[0]
    n = w_a.shape[1]
    nj = n // tn
    act = lambda width: pl.BlockSpec((tm, width), lambda i, j: (i, 0))
    wgt = lambda width: pl.BlockSpec((width, tn), lambda i, j: (0, j))
    gate = lambda g: pl.BlockSpec((tm, tn), lambda i, j: (i, g * nj + j))
    return pl.pallas_call(
        _merge_kernel,
        grid=(m // tm, nj),
        in_specs=[act(y_a.shape[1]), act(y_l.shape[1]), act(y_r.shape[1]),
                  wgt(w_a.shape[0]), wgt(w_l.shape[0]), wgt(w_r.shape[0]),
                  gate(0), gate(1), gate(2)],
        out_specs=pl.BlockSpec((tm, tn), lambda i, j: (i, j)),
        out_shape=jax.ShapeDtypeStruct((m, n), BF16),
        compiler_params=_cparams("parallel", "parallel"),
        name="merge_branches",
    )(y_a, y_l, y_r, w_a, w_l, w_r, proj, proj, proj)


def _ffn_up_kernel(x_ref, wg_ref, wv_ref, cw_ref, cb_ref, o_ref, wgb_ref, wvb_ref, halo_ref,
                   *, tiles_per_seq, valid_tiles):
    j = pl.program_id(0)
    i = pl.program_id(1)

    @pl.when(i == 0)
    def _():
        wgb_ref[...] = wg_ref[...].astype(BF16)
        wvb_ref[...] = wv_ref[...].astype(BF16)

    @pl.when(i % tiles_per_seq == 0)
    def _():
        halo_ref[...] = jnp.zeros_like(halo_ref)

    @pl.when(j < valid_tiles)
    def _():
        x = x_ref[...]
        g = jnp.dot(x, wgb_ref[...], preferred_element_type=F32)
        tm = g.shape[0]
        tail = halo_ref[...]
        c = g * cw_ref[FFN_CONV - 1:FFN_CONV, :] + cb_ref[...]
        for sh in range(1, FFN_CONV):
            c = c + _shift_rows(g, sh, tail) * cw_ref[FFN_CONV - 1 - sh:FFN_CONV - sh, :]
        halo_ref[...] = g[tm - SUBLANES:tm, :]
        v = jnp.dot(x, wvb_ref[...], preferred_element_type=F32)
        o_ref[...] = (_gelu_tanh(c) * v).astype(o_ref.dtype)

    @pl.when(j >= valid_tiles)
    def _():
        o_ref[...] = jnp.zeros_like(o_ref)


def ffn_up(x, w, conv_w, conv_b, layer, *, seq, tm, tn, dff_pad):
    m, k = x.shape
    dff = conv_w.shape[-1]
    valid = dff // tn
    last = valid - 1
    return pl.pallas_call(
        functools.partial(_ffn_up_kernel, tiles_per_seq=seq // tm, valid_tiles=valid),
        grid=(dff_pad // tn, m // tm),
        in_specs=[pl.BlockSpec((tm, k), lambda j, i: (i, 0)),
                  pl.BlockSpec((None, k, tn), lambda j, i: (layer, 0, jnp.minimum(j, last))),
                  pl.BlockSpec((None, k, tn), lambda j, i: (layer, 0, valid + jnp.minimum(j, last))),
                  pl.BlockSpec((None, FFN_CONV, tn), lambda j, i: (layer, 0, jnp.minimum(j, last))),
                  pl.BlockSpec((None, 1, tn), lambda j, i: (layer, 0, jnp.minimum(j, last)))],
        out_specs=pl.BlockSpec((tm, tn), lambda j, i: (i, j)),
        out_shape=jax.ShapeDtypeStruct((m, dff_pad), BF16),
        scratch_shapes=[pltpu.VMEM((k, tn), BF16), pltpu.VMEM((k, tn), BF16),
                        pltpu.VMEM((SUBLANES, tn), F32)],
        compiler_params=_cparams("arbitrary", "arbitrary"),
        name="ffn_up",
    )(x, w, w, conv_w, conv_b)


def _rotate_half_cols(w):
    half = MLA_ROPE // 2
    return jnp.concatenate([-w[..., half:], w[..., :half]], axis=-1)


SRC_CKV = MLA_Q_RANK
SRC_KR = SRC_CKV + MLA_KV_RANK
SRC_LRU = SRC_KR + MLA_ROPE
SRC_RW = SRC_LRU + 2 * LRU_WIDTH
SRC_XWA = SRC_RW + 3 * RW_WIDTH
SRC_XG = SRC_XWA + RW_DECAY_LORA + RW_AAA_LORA
SRC_GATES = SRC_XG + RW_GATE_LORA


W_IN_TILE = 256
_T_LRU = (3 * D_MODEL) // W_IN_TILE
_T_RW = _T_LRU + (2 * LRU_WIDTH) // W_IN_TILE
_T_CQ = _T_RW + (3 * RW_WIDTH) // W_IN_TILE
_T_CKV = _T_CQ + Q_PAD // W_IN_TILE
_T_MIX = _T_CKV + MLA_KV_RANK // W_IN_TILE
_T_XG = _T_MIX + 1


def _w_in_src_row(t):
    tile = W_IN_TILE
    return jnp.where(
        t < _T_LRU, SRC_GATES + tile * t, jnp.where(
            t < _T_RW, SRC_LRU + tile * (t - _T_LRU), jnp.where(
                t < _T_CQ, SRC_RW + tile * (t - _T_RW), jnp.where(
                    t < _T_CKV, tile * (t - _T_CQ), jnp.where(
                        t < _T_MIX, SRC_CKV + tile * (t - _T_CKV), jnp.where(
                            t == _T_MIX, SRC_KR, SRC_XG))))))


def _w_in_kernel(a_ref, b_ref, o_ref):
    t = pl.program_id(0)
    a = a_ref[...]

    @pl.when(t != _T_MIX)
    def _():
        valid = jnp.where(t == _T_CKV - 1, MLA_Q_RANK - (Q_PAD - W_IN_TILE),
                          jnp.where(t == _T_XG, RW_GATE_LORA, W_IN_TILE))
        row = lax.broadcasted_iota(jnp.int32, a.shape, 0)
        o_ref[...] = jnp.where(row < valid, a, 0.0).astype(BF16)

    @pl.when(t == _T_MIX)
    def _():
        half = MLA_ROPE // 2
        lora = RW_DECAY_LORA + RW_AAA_LORA
        o_ref[...] = jnp.concatenate(
            [a[0:MLA_ROPE], -a[half:MLA_ROPE], a[0:half], b_ref[0:lora]], axis=0).astype(BF16)


def relayout_w_in(w_t, layer, *, k):
    window = lambda index: pl.BlockSpec((None, pl.Element(W_IN_TILE), pl.Element(k)), index)
    return pl.pallas_call(
        _w_in_kernel,
        grid=(IN_COLS_PAD // W_IN_TILE,),
        in_specs=[window(lambda t: (layer, pl.multiple_of(_w_in_src_row(t), SUBLANES), 0)),
                  window(lambda t: (layer, SRC_XWA, 0))],
        out_specs=pl.BlockSpec((W_IN_TILE, k), lambda t: (t, 0)),
        out_shape=jax.ShapeDtypeStruct((IN_COLS_PAD, k), BF16),
        compiler_params=_cparams("arbitrary"),
        name="w_in_relayout",
    )(w_t, w_t)


def _cast_kernel(s_ref, o_ref, *, valid_blocks):
    i = pl.program_id(0)

    @pl.when(i < valid_blocks)
    def _():
        o_ref[...] = s_ref[...].astype(o_ref.dtype)

    @pl.when(i >= valid_blocks)
    def _():
        o_ref[...] = jnp.zeros_like(o_ref)


def cast_bf16(w, layer=None, *, tr, rows_out=None):
    r, c = w.shape[-2:]
    rows_out = r if rows_out is None else rows_out
    valid = r // tr
    if layer is None:
        src = pl.BlockSpec((tr, c), lambda i: (jnp.minimum(i, valid - 1), 0))
    else:
        src = pl.BlockSpec((None, tr, c), lambda i: (layer, jnp.minimum(i, valid - 1), 0))
    return pl.pallas_call(
        functools.partial(_cast_kernel, valid_blocks=valid),
        grid=(rows_out // tr,),
        in_specs=[src],
        out_specs=pl.BlockSpec((tr, c), lambda i: (i, 0)),
        out_shape=jax.ShapeDtypeStruct((rows_out, c), BF16),
        compiler_params=_cparams("parallel"),
        name="cast_bf16",
    )(w)


def _prep_w_uq(w):
    rope = w[..., MLA_NOPE:]
    w = jnp.concatenate([w, _rotate_half_cols(rope)], axis=-1)
    w = w.reshape(MLA_Q_RANK, MLA_HEADS * 2 * LANES)
    return jnp.pad(w, ((0, Q_PAD - MLA_Q_RANK), (0, 0))).astype(BF16)


def _rope_table(positions):
    inv_freq = ROPE_THETA ** (-jnp.arange(0, MLA_ROPE, 2, dtype=F32) / MLA_ROPE)
    ang = positions.astype(F32)[..., None] * inv_freq
    cos, sin = jnp.cos(ang), jnp.sin(ang)
    return jnp.concatenate([cos, cos, sin, sin], axis=-1).reshape(-1, 2 * MLA_ROPE)


class _Tiles:
    in_proj = (1024, 1024)
    mla_up_rows = 512
    attn_q = 512
    lru_rows = 512
    rwkv_prep_rows = 256
    merge = (512, 1024)
    out_mm = (1024, 1024, 1024)
    ln_rows = 256
    ffn_up = (1024, 256)
    cast_rows = 256


def kernel(x, positions, w_in, mla_q_norm, mla_w_uq, mla_kv_norm, mla_w_ukv, lru_conv_w, lru_conv_b, lru_w_a, lru_b_a, lru_w_x, lru_b_x, lru_lambda, rw_mu, rw_w0, rw_w2, rw_a0, rw_a2, rw_g2, rw_k_k, rw_k_a, rw_r_k, rw_gn_g, rw_gn_b, w_o_mla, w_o_lru, w_o_rwkv, w_out, ln1_g, ln1_b, ffn_w_up, ffn_conv_w, ffn_conv_b, ffn_w_down, ln2_g, ln2_b):
    batch, seq, d = x.shape
    m = batch * seq
    t = _Tiles
    cs = _rope_table(positions)
    xf = x.reshape(m, d)
    cast = functools.partial(cast_bf16, tr=t.cast_rows)
    xb = cast(xf)
    w_ukv_all = mla_w_ukv.reshape(DEPTH, MLA_KV_RANK, -1)
    conv_b_all = ffn_conv_b.reshape(DEPTH, 1, -1)
    w_in_t = jnp.transpose(w_in, (0, 2, 1))
    for l in range(DEPTH):
        proj = matmul_nt(xb, relayout_w_in(w_in_t, l, k=d), tm=t.in_proj[0], tn=t.in_proj[1])

        q_gain = jnp.pad(mla_q_norm[l], (0, Q_PAD - MLA_Q_RANK)).reshape(1, Q_PAD)
        q = mla_q_up(proj, q_gain, _prep_w_uq(mla_w_uq[l]), cs, batch=batch, seq=seq, tm=t.mla_up_rows)
        k, v = mla_kv_up(proj, mla_kv_norm[l].reshape(1, -1), cast(w_ukv_all, l), cs, batch=batch,
                         seq=seq, tm=t.mla_up_rows)
        y_a = mla_attention(q, k, v, tq=t.attn_q).reshape(m, -1)

        y_l = rglru_branch(proj, lru_conv_w[l], lru_conv_b[l], lru_w_a[l], lru_b_a[l],
                           lru_w_x[l], lru_b_x[l], lru_lambda[l], batch=batch, seq=seq, ts=t.lru_rows)

        rw = rwkv_prep(proj, rw_mu[l], rw_w0[l], rw_w2[l], rw_a0[l], rw_a2[l], rw_g2[l],
                       rw_k_k[l], rw_k_a[l], batch=batch, seq=seq, ts=t.rwkv_prep_rows)
        y_r = rwkv_recurrence(*rw, rw_r_k[l], rw_gn_g[l], rw_gn_b[l], batch=batch, seq=seq)
        y_r = y_r.reshape(m, RW_WIDTH)

        merged = merge_branches(y_a, y_l, y_r, cast(w_o_mla, l), cast(w_o_lru, l), cast(w_o_rwkv, l),
                                proj, tm=t.merge[0], tn=t.merge[1])
        mm = dict(tm=t.out_mm[0], tn=t.out_mm[1], tk=t.out_mm[2], ln_rows=t.ln_rows)
        xf, xb = matmul_residual_layernorm(merged, cast(w_out, l), xf, ln1_g[l], ln1_b[l],
                                           name="mixer_out", **mm)

        h = ffn_up(xb, ffn_w_up, ffn_conv_w, conv_b_all, l, seq=seq, tm=t.ffn_up[0],
                   tn=t.ffn_up[1], dff_pad=D_FF_PAD)
        xf, xb = matmul_residual_layernorm(h, cast(ffn_w_down, l, rows_out=D_FF_PAD), xf, ln2_g[l],
                                           ln2_b[l], name="ffn_down", **mm)
    return xf.reshape(batch, seq, d)
```

```python
import functools

import jax
import jax.numpy as jnp
from jax import lax
from jax.experimental import pallas as pl
from jax.experimental.pallas import tpu as pltpu

F32 = jnp.float32
BF16 = jnp.bfloat16

D_MODEL = 4096
DEPTH = 2
CHUNK = 64
MLA_HEADS = 16
MLA_Q_RANK = 896
MLA_KV_RANK = 512
MLA_NOPE = 128
MLA_ROPE = 64
MLA_V = 128
ROPE_THETA = 10000.0
LRU_WIDTH = 1024
LRU_BLOCKS = 8
LRU_BLOCK = LRU_WIDTH // LRU_BLOCKS
LRU_CONV = 4
LRU_C = 8.0
RW_WIDTH = 1024
RW_HEAD = 64
RW_HEADS = RW_WIDTH // RW_HEAD
RW_DECAY_LORA = 64
RW_AAA_LORA = 64
RW_GATE_LORA = 160
D_FF = 11008
FFN_CONV = 3
ALPHA = (2 * DEPTH) ** 0.25
LN_EPS = 1e-5
RMS_EPS = 1e-6
GN_EPS = 64e-5

LANES = 128
SUBLANES = 8
VMEM_LIMIT_BYTES = 56 * 1024 * 1024

Q_PAD = 1024
D_FF_PAD = 11264
XG_PAD = 256
OFF_GATES = 0
OFF_LRU_X = 3 * D_MODEL
OFF_LRU_G = OFF_LRU_X + LRU_WIDTH
OFF_RW_R = OFF_LRU_G + LRU_WIDTH
OFF_RW_K = OFF_RW_R + RW_WIDTH
OFF_RW_V = OFF_RW_K + RW_WIDTH
OFF_CQ = OFF_RW_V + RW_WIDTH
OFF_CKV = OFF_CQ + Q_PAD
OFF_KR = OFF_CKV + MLA_KV_RANK
OFF_XWA = OFF_KR + 2 * MLA_ROPE
OFF_XG = OFF_XWA + RW_DECAY_LORA + RW_AAA_LORA
IN_COLS_PAD = OFF_XG + XG_PAD

RW_GROUP_PAIRS = 4
_SQRT_2_OVER_PI = 0.7978845608028654
_LOG2_E = 1.4426950408889634


def _cparams(*sem):
    return pltpu.CompilerParams(dimension_semantics=sem, vmem_limit_bytes=VMEM_LIMIT_BYTES)


def _gelu_tanh(x):
    return 0.5 * x * (1.0 + jnp.tanh(_SQRT_2_OVER_PI * (x + 0.044715 * (x * x * x))))


def _softplus(z):
    return jnp.maximum(z, 0.0) + jnp.log1p(jnp.exp(-jnp.abs(z)))


def _shift_rows(x, shift, tail):
    row = lax.broadcasted_iota(jnp.int32, x.shape, 0)
    xs = pltpu.roll(x, shift, axis=0)
    for j in range(shift):
        src = SUBLANES - shift + j
        xs = jnp.where(row == j, tail[src:src + 1, :], xs)
    return xs


def _group_ones(rows, cols, group):
    shift = group.bit_length() - 1
    r = lax.broadcasted_iota(jnp.int32, (rows, cols), 0)
    c = lax.broadcasted_iota(jnp.int32, (rows, cols), 1)
    return jnp.where(((r & (cols - 1)) >> shift) == (c >> shift), 1.0, 0.0).astype(BF16)


def _head_sum(x, ones_hl):
    hi = x.astype(BF16)
    lo = (x - hi.astype(F32)).astype(BF16)
    return jnp.dot(jnp.concatenate([hi, lo], axis=1), ones_hl, preferred_element_type=F32)


def _mm_kernel(a_ref, b_ref, o_ref):
    o_ref[...] = lax.dot_general(a_ref[...], b_ref[...], (((1,), (1,)), ((), ())),
                                 preferred_element_type=F32).astype(o_ref.dtype)


def matmul_nt(a, b, *, tm, tn, out_dtype=F32):
    m, k = a.shape
    n, _ = b.shape
    return pl.pallas_call(
        _mm_kernel,
        grid=(m // tm, n // tn),
        in_specs=[pl.BlockSpec((tm, k), lambda i, j: (i, 0)),
                  pl.BlockSpec((tn, k), lambda i, j: (j, 0))],
        out_specs=pl.BlockSpec((tm, tn), lambda i, j: (i, j)),
        out_shape=jax.ShapeDtypeStruct((m, n), out_dtype),
        compiler_params=_cparams("parallel", "parallel"),
        name="in_proj",
    )(a, b)


def _mm_res_kernel(a_ref, w_ref, x_ref, o_ref, acc_ref):
    k = pl.program_id(2)

    @pl.when(k == 0)
    def _():
        acc_ref[...] = ALPHA * x_ref[...]

    acc_ref[...] += jnp.dot(a_ref[...], w_ref[...], preferred_element_type=F32)

    @pl.when(k == pl.num_programs(2) - 1)
    def _():
        o_ref[...] = acc_ref[...]


def _ln_kernel(y_ref, g_ref, b_ref, o_ref, ob_ref):
    y = y_ref[...]
    mu = jnp.mean(y, axis=-1, keepdims=True)
    yc = y - mu
    var = jnp.mean(yc * yc, axis=-1, keepdims=True)
    o = yc * lax.rsqrt(var + LN_EPS) * g_ref[...] + b_ref[...]
    o_ref[...] = o
    ob_ref[...] = o.astype(BF16)


def matmul_residual_layernorm(a, w, x, g, b, *, tm, tn, tk, ln_rows, name):
    m, k = a.shape
    n = w.shape[1]
    y = pl.pallas_call(
        _mm_res_kernel,
        grid=(m // tm, n // tn, k // tk),
        in_specs=[pl.BlockSpec((tm, tk), lambda i, j, kk: (i, kk)),
                  pl.BlockSpec((tk, tn), lambda i, j, kk: (kk, j)),
                  pl.BlockSpec((tm, tn), lambda i, j, kk: (i, j))],
        out_specs=pl.BlockSpec((tm, tn), lambda i, j, kk: (i, j)),
        out_shape=jax.ShapeDtypeStruct((m, n), F32),
        scratch_shapes=[pltpu.VMEM((tm, tn), F32)],
        compiler_params=_cparams("parallel", "parallel", "arbitrary"),
        name=name,
    )(a, w, x)
    return pl.pallas_call(
        _ln_kernel,
        grid=(m // ln_rows,),
        in_specs=[pl.BlockSpec((ln_rows, n), lambda i: (i, 0)),
                  pl.BlockSpec((1, n), lambda i: (0, 0)),
                  pl.BlockSpec((1, n), lambda i: (0, 0))],
        out_specs=[pl.BlockSpec((ln_rows, n), lambda i: (i, 0)),
                   pl.BlockSpec((ln_rows, n), lambda i: (i, 0))],
        out_shape=[jax.ShapeDtypeStruct((m, n), F32), jax.ShapeDtypeStruct((m, n), BF16)],
        compiler_params=_cparams("parallel"),
        name="layernorm",
    )(y, g.reshape(1, n), b.reshape(1, n))


def _rope_half(block, cs):
    p = block * cs
    return p + pltpu.roll(p, MLA_ROPE, axis=1)


def _q_up_kernel(p_ref, g_ref, w_ref, cs_ref, q_ref, *, rank):
    x = p_ref[...]
    ms = jnp.sum(x * x, axis=-1, keepdims=True) * (1.0 / rank)
    xn = (x * lax.rsqrt(ms + RMS_EPS) * g_ref[...]).astype(BF16)
    cs = cs_ref[...]
    hw = 2 * LANES
    for h in range(q_ref.shape[1]):
        acc = jnp.dot(xn, w_ref[:, h * hw:(h + 1) * hw], preferred_element_type=F32)
        rot = _rope_half(acc[:, LANES:hw], cs)
        q_ref[0, h, :, 0:MLA_NOPE] = acc[:, 0:MLA_NOPE].astype(BF16)
        q_ref[0, h, :, MLA_NOPE:MLA_NOPE + MLA_ROPE] = rot[:, 0:MLA_ROPE].astype(BF16)


def mla_q_up(proj, gain, w, cs, *, batch, seq, tm):
    nh = MLA_HEADS
    spt = seq // tm
    return pl.pallas_call(
        functools.partial(_q_up_kernel, rank=MLA_Q_RANK),
        grid=(batch * spt,),
        in_specs=[pl.BlockSpec((tm, Q_PAD), lambda i: (i, OFF_CQ // Q_PAD)),
                  pl.BlockSpec((1, Q_PAD), lambda i: (0, 0)),
                  pl.BlockSpec(w.shape, lambda i: (0, 0)),
                  pl.BlockSpec((tm, LANES), lambda i: (i, 0))],
        out_specs=pl.BlockSpec((1, nh, tm, MLA_NOPE + MLA_ROPE), lambda i: (i // spt, 0, i % spt, 0)),
        out_shape=jax.ShapeDtypeStruct((batch, nh, seq, MLA_NOPE + MLA_ROPE), BF16),
        compiler_params=_cparams("parallel"),
        name="mla_q_up",
    )(proj, gain, w, cs)


def _kv_up_kernel(p_ref, kr_ref, g_ref, w_ref, cs_ref, k_ref, v_ref, *, rank):
    x = p_ref[...]
    ms = jnp.sum(x * x, axis=-1, keepdims=True) * (1.0 / rank)
    xn = (x * lax.rsqrt(ms + RMS_EPS) * g_ref[...]).astype(BF16)
    krope = _rope_half(kr_ref[...], cs_ref[...])[:, 0:MLA_ROPE].astype(BF16)
    hw = MLA_NOPE + MLA_V
    for h in range(k_ref.shape[1]):
        acc = jnp.dot(xn, w_ref[:, h * hw:(h + 1) * hw], preferred_element_type=F32)
        k_ref[0, h, :, 0:MLA_NOPE] = acc[:, 0:MLA_NOPE].astype(BF16)
        k_ref[0, h, :, MLA_NOPE:MLA_NOPE + MLA_ROPE] = krope
        v_ref[0, h] = acc[:, MLA_NOPE:hw].astype(BF16)


def mla_kv_up(proj, gain, w, cs, *, batch, seq, tm):
    nh = MLA_HEADS
    spt = seq // tm
    dk = MLA_NOPE + MLA_ROPE
    return pl.pallas_call(
        functools.partial(_kv_up_kernel, rank=MLA_KV_RANK),
        grid=(batch * spt,),
        in_specs=[pl.BlockSpec((tm, MLA_KV_RANK), lambda i: (i, OFF_CKV // MLA_KV_RANK)),
                  pl.BlockSpec((tm, LANES), lambda i: (i, OFF_KR // LANES)),
                  pl.BlockSpec((1, MLA_KV_RANK), lambda i: (0, 0)),
                  pl.BlockSpec(w.shape, lambda i: (0, 0)),
                  pl.BlockSpec((tm, LANES), lambda i: (i, 0))],
        out_specs=[pl.BlockSpec((1, nh, tm, dk), lambda i: (i // spt, 0, i % spt, 0)),
                   pl.BlockSpec((1, nh, tm, MLA_V), lambda i: (i // spt, 0, i % spt, 0))],
        out_shape=[jax.ShapeDtypeStruct((batch, nh, seq, dk), BF16),
                   jax.ShapeDtypeStruct((batch, nh, seq, MLA_V), BF16)],
        compiler_params=_cparams("parallel"),
        name="mla_kv_up",
    )(proj, proj, gain, w, cs)


def _flash_kernel(q_ref, k_ref, v_ref, o_ref, *, tq, scale):
    qi = pl.program_id(2)
    heads = range(q_ref.shape[1])
    qs = [q_ref[0, h] for h in heads]

    def scores(h, j):
        k = k_ref[0, h, pl.ds(pl.multiple_of(j * tq, tq), tq), :]
        return lax.dot_general(qs[h], k, (((1,), (1,)), ((), ())),
                               preferred_element_type=F32) * (scale * _LOG2_E)

    def update(h, j, s, m, l, acc):
        v = v_ref[0, h, pl.ds(pl.multiple_of(j * tq, tq), tq), :]
        m_new = jnp.maximum(m, jnp.max(s, axis=-1, keepdims=True))
        alpha = jnp.exp2(m - m_new)
        p = jnp.exp2(s - m_new)
        l = alpha * l + jnp.sum(p, axis=-1, keepdims=True)
        acc = alpha * acc + jnp.dot(p.astype(BF16), v, preferred_element_type=F32)
        return m_new, l, acc

    def pair(jj, carry):
        out = []
        for h in heads:
            s_a, s_b = scores(h, 2 * jj), scores(h, 2 * jj + 1)
            out.append(update(h, 2 * jj + 1, s_b, *update(h, 2 * jj, s_a, *carry[h])))
        return tuple(out)

    def single(_, carry):
        return tuple(update(h, qi - 1, scores(h, qi - 1), *carry[h]) for h in heads)

    carry = tuple((jnp.full((tq, 1), -1e30, F32), jnp.zeros((tq, 1), F32),
                   jnp.zeros((tq, MLA_V), F32)) for _ in heads)
    carry = lax.fori_loop(0, qi >> 1, pair, carry)
    carry = lax.fori_loop(0, qi & 1, single, carry)
    shift = CHUNK.bit_length() - 1
    qc = lax.broadcasted_iota(jnp.int32, (tq, tq), 0) >> shift
    kc = lax.broadcasted_iota(jnp.int32, (tq, tq), 1) >> shift
    for h in heads:
        m, l, acc = update(h, qi, jnp.where(kc <= qc, scores(h, qi), -jnp.inf), *carry[h])
        o_ref[0, :, h * MLA_V:(h + 1) * MLA_V] = (acc / l).astype(o_ref.dtype)


def mla_attention(q, k, v, *, tq, hps):
    batch, nh, seq, dk = q.shape
    scale = (MLA_NOPE + MLA_ROPE) ** -0.5
    return pl.pallas_call(
        functools.partial(_flash_kernel, tq=tq, scale=scale),
        grid=(batch, nh // hps, seq // tq),
        in_specs=[pl.BlockSpec((1, hps, tq, dk), lambda b, h, i: (b, h, i, 0)),
                  pl.BlockSpec((1, hps, seq, dk), lambda b, h, i: (b, h, 0, 0)),
                  pl.BlockSpec((1, hps, seq, MLA_V), lambda b, h, i: (b, h, 0, 0))],
        out_specs=pl.BlockSpec((1, tq, hps * MLA_V), lambda b, h, i: (b, i, h)),
        out_shape=jax.ShapeDtypeStruct((batch, seq, nh * MLA_V), BF16),
        compiler_params=_cparams("parallel", "parallel", "arbitrary"),
        name="mla_attention",
    )(q, k, v)


def _lru_kernel(x_ref, g_ref, cw_ref, cb_ref, wa_ref, ba_ref, wx_ref, bx_ref, lam_ref, o_ref,
                tail_ref, h_ref, a_scr, u_scr):
    @pl.when(pl.program_id(1) == 0)
    def _():
        tail_ref[...] = jnp.zeros_like(tail_ref)
        h_ref[...] = jnp.zeros_like(h_ref)

    x = x_ref[...]
    ts = x.shape[0]
    tail = tail_ref[...]
    xc = x * cw_ref[LRU_CONV - 1:LRU_CONV, :] + cb_ref[...]
    for sh in range(1, LRU_CONV):
        xc = xc + _shift_rows(x, sh, tail) * cw_ref[LRU_CONV - 1 - sh:LRU_CONV - sh, :]
    tail_ref[...] = x[ts - SUBLANES:ts, :]

    xb = xc.astype(BF16)
    ra, rx = [], []
    for n in range(LRU_BLOCKS):
        blk = xb[:, n * LRU_BLOCK:(n + 1) * LRU_BLOCK]
        ra.append(jnp.dot(blk, wa_ref[n], preferred_element_type=F32))
        rx.append(jnp.dot(blk, wx_ref[n], preferred_element_type=F32))
    r = jax.nn.sigmoid(jnp.concatenate(ra, axis=1) + ba_ref[...])
    gi = jax.nn.sigmoid(jnp.concatenate(rx, axis=1) + bx_ref[...])
    log_a = -LRU_C * r * _softplus(-lam_ref[...])
    a_scr[...] = jnp.exp(log_a)
    u_scr[...] = jnp.sqrt(1.0 - jnp.exp(2.0 * log_a)) * (gi * xc)

    row8 = lax.broadcasted_iota(jnp.int32, (SUBLANES, x.shape[1]), 0)

    def body(grp, h):
        base = pl.multiple_of(grp * SUBLANES, SUBLANES)
        a8 = a_scr[pl.ds(base, SUBLANES), :]
        u8 = u_scr[pl.ds(base, SUBLANES), :]
        hs = u8
        for s in range(SUBLANES):
            h = a8[s:s + 1, :] * h + u8[s:s + 1, :]
            hs = jnp.where(row8 == s, h, hs)
        u_scr[pl.ds(base, SUBLANES), :] = hs
        return h

    h_ref[...] = lax.fori_loop(0, ts // SUBLANES, body, h_ref[...])
    o_ref[...] = (u_scr[...] * _gelu_tanh(g_ref[...])).astype(o_ref.dtype)


def rglru_branch(proj, conv_w, conv_b, w_a, b_a, w_x, b_x, lam, *, batch, seq, ts):
    w = LRU_WIDTH
    spt = seq // ts
    vec = lambda: pl.BlockSpec((1, w), lambda b, s: (0, 0))
    blockdiag = lambda: pl.BlockSpec((LRU_BLOCKS, LRU_BLOCK, LRU_BLOCK), lambda b, s: (0, 0, 0))
    return pl.pallas_call(
        _lru_kernel,
        grid=(batch, spt),
        in_specs=[pl.BlockSpec((ts, w), lambda b, s: (b * spt + s, OFF_LRU_X // w)),
                  pl.BlockSpec((ts, w), lambda b, s: (b * spt + s, OFF_LRU_G // w)),
                  pl.BlockSpec((LRU_CONV, w), lambda b, s: (0, 0)),
                  vec(), blockdiag(), vec(), blockdiag(), vec(), vec()],
        out_specs=pl.BlockSpec((ts, w), lambda b, s: (b * spt + s, 0)),
        out_shape=jax.ShapeDtypeStruct((batch * seq, w), BF16),
        scratch_shapes=[pltpu.VMEM((SUBLANES, w), F32), pltpu.VMEM((1, w), F32),
                        pltpu.VMEM((ts, w), F32), pltpu.VMEM((ts, w), F32)],
        compiler_params=_cparams("parallel", "arbitrary"),
        name="rglru",
    )(proj, proj, conv_w, conv_b.reshape(1, w), w_a.astype(BF16), b_a.reshape(1, w),
      w_x.astype(BF16), b_x.reshape(1, w), lam.reshape(1, w))


def _rwkv_prep_kernel(r_ref, k_ref, v_ref, wa_ref, xg_ref,
                      mur_ref, muk_ref, muv_ref, muwa_ref, mug_ref,
                      w0_ref, w2_ref, a0_ref, a2_ref, g2_ref, kkw_ref, ka_ref,
                      ro_ref, ko_ref, vo_ref, nkk_ref, b_ref, d_ref, g_ref,
                      tr_ref, tk_ref, tv_ref, twa_ref, tg_ref):
    @pl.when(pl.program_id(1) == 0)
    def _():
        for t in (tr_ref, tk_ref, tv_ref, twa_ref, tg_ref):
            t[...] = jnp.zeros_like(t)

    def mix(x_ref, mu_ref, tail_ref):
        x = x_ref[...]
        prev = _shift_rows(x, 1, tail_ref[...])
        tail_ref[...] = x[x.shape[0] - SUBLANES:, :]
        return x + mu_ref[...] * (prev - x)

    r = mix(r_ref, mur_ref, tr_ref)
    k = mix(k_ref, muk_ref, tk_ref)
    v = mix(v_ref, muv_ref, tv_ref)
    xwa = mix(wa_ref, muwa_ref, twa_ref)
    xg = mix(xg_ref, mug_ref, tg_ref)
    w = -_softplus(-(w0_ref[...] + jnp.dot(jnp.tanh(xwa).astype(BF16), w2_ref[...],
                                           preferred_element_type=F32))) - 0.5
    a = jax.nn.sigmoid(a0_ref[...] + jnp.dot(xwa.astype(BF16), a2_ref[...], preferred_element_type=F32))
    g = jnp.dot(jax.nn.sigmoid(xg).astype(BF16), g2_ref[...], preferred_element_type=F32)

    kk = k * kkw_ref[...]
    ones_hl = _group_ones(2 * LANES, LANES, RW_HEAD)
    sq = kk * kk
    ss = jnp.concatenate([_head_sum(sq[:, c * LANES:(c + 1) * LANES], ones_hl)
                          for c in range(RW_WIDTH // LANES)], axis=1)
    kk = kk / jnp.maximum(jnp.sqrt(ss), 1e-12)

    ro_ref[...] = r
    ko_ref[...] = k * (1.0 + (a - 1.0) * ka_ref[...])
    vo_ref[...] = v
    nkk_ref[...] = -kk
    b_ref[...] = kk * a
    d_ref[...] = jnp.exp(-jnp.exp(w))
    g_ref[...] = g


def rwkv_prep(proj, mu, w0, w2, a0, a2, g2, k_k, k_a, *, batch, seq, ts):
    w = RW_WIDTH
    spt = seq // ts
    lora = RW_DECAY_LORA + RW_AAA_LORA
    row = lambda width, col: pl.BlockSpec((ts, width), lambda b, s: (b * spt + s, col))
    vec = lambda width: pl.BlockSpec((1, width), lambda b, s: (0, 0))
    mat = lambda rows: pl.BlockSpec((rows, w), lambda b, s: (0, 0))
    mu_r, mu_k, mu_v = (mu[i * w:(i + 1) * w].reshape(1, w) for i in range(3))
    mu_wa = mu[3 * w:3 * w + lora].reshape(1, lora)
    mu_g = jnp.pad(mu[3 * w + lora:], (0, XG_PAD - RW_GATE_LORA)).reshape(1, XG_PAD)
    zeros = jnp.zeros((RW_DECAY_LORA, w), w2.dtype)
    w2p = jnp.concatenate([w2, zeros], axis=0).astype(BF16)
    a2p = jnp.concatenate([zeros, a2], axis=0).astype(BF16)
    g2p = jnp.pad(g2, ((0, XG_PAD - RW_GATE_LORA), (0, 0))).astype(BF16)
    out = jax.ShapeDtypeStruct((batch * seq, w), F32)
    return pl.pallas_call(
        _rwkv_prep_kernel,
        grid=(batch, spt),
        in_specs=[row(w, OFF_RW_R // w), row(w, OFF_RW_K // w), row(w, OFF_RW_V // w),
                  row(lora, OFF_XWA // lora), row(XG_PAD, OFF_XG // XG_PAD),
                  vec(w), vec(w), vec(w), vec(lora), vec(XG_PAD),
                  vec(w), mat(lora), vec(w), mat(lora), mat(XG_PAD), vec(w), vec(w)],
        out_specs=[pl.BlockSpec((ts, w), lambda b, s: (b * spt + s, 0))] * 7,
        out_shape=[out] * 7,
        scratch_shapes=[pltpu.VMEM((SUBLANES, w), F32)] * 3
                       + [pltpu.VMEM((SUBLANES, lora), F32), pltpu.VMEM((SUBLANES, XG_PAD), F32)],
        compiler_params=_cparams("parallel", "arbitrary"),
        name="rwkv_prep",
    )(proj, proj, proj, proj, proj, mu_r, mu_k, mu_v, mu_wa, mu_g,
      w0.reshape(1, w), w2p, a0.reshape(1, w), a2p, g2p, k_k.reshape(1, w), k_a.reshape(1, w))


def _rwkv_rec_kernel(r_ref, k_ref, v_ref, nkk_ref, b_ref, d_ref, g_ref, rk_ref, gng_ref, gnb_ref,
                     y_ref, st_ref, q_ref, ot_ref):
    nb, tau, width = r_ref.shape
    npair = width // LANES
    hd = RW_HEAD
    pairs = [(bi, p) for bi in range(nb) for p in range(npair)]
    flat = [(i, bi, p) for i, (bi, p) in enumerate(pairs)]
    groups = [flat[j:j + RW_GROUP_PAIRS] for j in range(0, len(flat), RW_GROUP_PAIRS)]

    @pl.when(pl.program_id(0) == 0)
    def _():
        st_ref[...] = jnp.zeros_like(st_ref)

    q_ref[...] = jnp.zeros_like(q_ref)
    ot_ref[...] = jnp.zeros_like(ot_ref)

    row = lax.broadcasted_iota(jnp.int32, (hd, LANES), 0)
    lane = lax.broadcasted_iota(jnp.int32, (hd, LANES), 1)
    lane_in_head = lane & (hd - 1)
    diag = lane_in_head == row
    ones2 = _group_ones(2 * LANES, 2 * LANES, hd)

    def group(grp, carry):
        base = pl.multiple_of(grp * SUBLANES, SUBLANES)
        tile = lambda ref, bi, p: ref[bi, pl.ds(base, SUBLANES), pl.ds(p * LANES, LANES)]
        vcb = [None] * len(groups)
        for s in range(SUBLANES):
            sel = lane_in_head == (base + s - 1)
            for gi, grp_pairs in enumerate(groups):
                lhs = []
                for i, bi, p in grp_pairs:
                    pm = (st_ref[bi, p] * tile(nkk_ref, bi, p)[s:s + 1, :]).astype(BF16)
                    lhs.append(jnp.concatenate([pm, q_ref[pl.ds(i * hd, hd), :]], axis=1))
                if s % 2 == 0:
                    for i, bi, p in grp_pairs:
                        v8 = tile(v_ref, bi, p)
                        lhs.append(jnp.concatenate(
                            [jnp.where(diag, v8[s:s + 1, :], 0.0).astype(BF16),
                             jnp.where(diag, v8[s + 1:s + 2, :], 0.0).astype(BF16)], axis=1))
                out = jnp.dot(jnp.concatenate(lhs, axis=0), ones2, preferred_element_type=F32)
                if s % 2 == 0:
                    vcb[gi] = out[len(grp_pairs) * hd:, :]
                for j, (i, bi, p) in enumerate(grp_pairs):
                    rows = slice(j * hd, (j + 1) * hd)
                    ot_ref[bi, p] = jnp.where(sel, out[rows, LANES:], ot_ref[bi, p])
                    vc = vcb[gi][rows, 0:LANES] if s % 2 == 0 else vcb[gi][rows, LANES:]
                    new = (st_ref[bi, p] * tile(d_ref, bi, p)[s:s + 1, :]
                           + out[rows, 0:LANES] * tile(b_ref, bi, p)[s:s + 1, :]
                           + vc * tile(k_ref, bi, p)[s:s + 1, :])
                    st_ref[bi, p] = new
                    q_ref[pl.ds(i * hd, hd), :] = (new * tile(r_ref, bi, p)[s:s + 1, :]).astype(BF16)
        return carry

    lax.fori_loop(0, tau // SUBLANES, group, 0)

    o_last = jnp.dot(q_ref[...], ones2[0:LANES, 0:LANES], preferred_element_type=F32)
    sel = lane_in_head == (tau - 1)
    for i, (bi, p) in enumerate(pairs):
        ot_ref[bi, p] = jnp.where(sel, o_last[i * hd:(i + 1) * hd, :], ot_ref[bi, p])

    ones_hl = _group_ones(2 * LANES, LANES, hd)
    first_head = lane < hd
    inv_hd = 1.0 / hd
    for bi in range(nb):
        nat = []
        for p0 in range(0, npair, 2):
            tr = jnp.concatenate([ot_ref[bi, p0], ot_ref[bi, p0 + 1]], axis=0).T
            top, bot = tr[0:hd, :], tr[hd:2 * hd, :]
            nat.append(jnp.where(first_head, top, pltpu.roll(bot, hd, axis=1)))
            nat.append(jnp.where(first_head, pltpu.roll(top, hd, axis=1), bot))
        cols = [pl.ds(p * LANES, LANES) for p in range(npair)]
        o = jnp.concatenate(nat, axis=0)
        mean = _head_sum(o, ones_hl) * inv_hd
        oc = o - mean
        var = _head_sum(oc * oc, ones_hl) * inv_hd
        rkr = jnp.concatenate([r_ref[bi, :, c] * k_ref[bi, :, c] * rk_ref[:, c] for c in cols], axis=0)
        bonus = _head_sum(rkr, ones_hl)
        on = oc * lax.rsqrt(var + GN_EPS)
        for p, c in enumerate(cols):
            rows = slice(p * tau, (p + 1) * tau)
            y = on[rows] * gng_ref[:, c] + gnb_ref[:, c] + bonus[rows] * v_ref[bi, :, c]
            y_ref[bi, :, c] = (y * g_ref[bi, :, c]).astype(y_ref.dtype)


def rwkv_recurrence(r, k, v, nkk, b, d, g, r_k, gn_g, gn_b, *, batch, seq):
    width = RW_WIDTH
    tau = RW_HEAD
    npair = width // LANES
    spec = pl.BlockSpec((batch, tau, width), lambda i: (0, i, 0))
    vec = pl.BlockSpec((1, width), lambda i: (0, 0))
    acts = [a.reshape(batch, seq, width) for a in (r, k, v, nkk, b, d, g)]
    return pl.pallas_call(
        _rwkv_rec_kernel,
        grid=(seq // tau,),
        in_specs=[spec] * 7 + [vec] * 3,
        out_specs=spec,
        out_shape=jax.ShapeDtypeStruct((batch, seq, width), BF16),
        scratch_shapes=[pltpu.VMEM((batch, npair, RW_HEAD, LANES), F32),
                        pltpu.VMEM((batch * npair * RW_HEAD, LANES), BF16),
                        pltpu.VMEM((batch, npair, RW_HEAD, LANES), F32)],
        compiler_params=_cparams("arbitrary"),
        name="rwkv_recurrence",
    )(*acts, r_k.reshape(1, width), gn_g.reshape(1, width), gn_b.reshape(1, width))


def _merge_kernel(a_ref, l_ref, r_ref, wa_ref, wl_ref, wr_ref, ga_ref, gl_ref, gr_ref, o_ref):
    acc = jax.nn.sigmoid(ga_ref[...]) * jnp.dot(a_ref[...], wa_ref[...], preferred_element_type=F32)
    acc += jax.nn.sigmoid(gl_ref[...]) * jnp.dot(l_ref[...], wl_ref[...], preferred_element_type=F32)
    acc += jax.nn.sigmoid(gr_ref[...]) * jnp.dot(r_ref[...], wr_ref[...], preferred_element_type=F32)
    o_ref[...] = acc.astype(o_ref.dtype)


def merge_branches(y_a, y_l, y_r, w_a, w_l, w_r, proj, *, tm, tn):
    m = y_a.shape[0]
    n = w_a.shape[1]
    nj = n // tn
    act = lambda width: pl.BlockSpec((tm, width), lambda i, j: (i, 0))
    wgt = lambda width: pl.BlockSpec((width, tn), lambda i, j: (0, j))
    gate = lambda g: pl.BlockSpec((tm, tn), lambda i, j: (i, g * nj + j))
    return pl.pallas_call(
        _merge_kernel,
        grid=(m // tm, nj),
        in_specs=[act(y_a.shape[1]), act(y_l.shape[1]), act(y_r.shape[1]),
                  wgt(w_a.shape[0]), wgt(w_l.shape[0]), wgt(w_r.shape[0]),
                  gate(0), gate(1), gate(2)],
        out_specs=pl.BlockSpec((tm, tn), lambda i, j: (i, j)),
        out_shape=jax.ShapeDtypeStruct((m, n), BF16),
        compiler_params=_cparams("parallel", "parallel"),
        name="merge_branches",
    )(y_a, y_l, y_r, w_a, w_l, w_r, proj, proj, proj)


def _ffn_up_kernel(x_ref, wg_ref, wv_ref, cw_ref, cb_ref, o_ref, wgb_ref, wvb_ref, halo_ref,
                   *, tiles_per_seq, valid_tiles):
    j = pl.program_id(0)
    i = pl.program_id(1)

    @pl.when(i == 0)
    def _():
        wgb_ref[...] = wg_ref[...].astype(BF16)
        wvb_ref[...] = wv_ref[...].astype(BF16)

    @pl.when(i % tiles_per_seq == 0)
    def _():
        halo_ref[...] = jnp.zeros_like(halo_ref)

    @pl.when(j < valid_tiles)
    def _():
        x = x_ref[...]
        g = jnp.dot(x, wgb_ref[...], preferred_element_type=F32)
        tm = g.shape[0]
        tail = halo_ref[...]
        c = g * cw_ref[FFN_CONV - 1:FFN_CONV, :] + cb_ref[...]
        for sh in range(1, FFN_CONV):
            c = c + _shift_rows(g, sh, tail) * cw_ref[FFN_CONV - 1 - sh:FFN_CONV - sh, :]
        halo_ref[...] = g[tm - SUBLANES:tm, :]
        v = jnp.dot(x, wvb_ref[...], preferred_element_type=F32)
        o_ref[...] = (_gelu_tanh(c) * v).astype(o_ref.dtype)

    @pl.when(j >= valid_tiles)
    def _():
        o_ref[...] = jnp.zeros_like(o_ref)


def ffn_up(x, w, conv_w, conv_b, layer, *, seq, tm, tn, dff_pad):
    m, k = x.shape
    dff = conv_w.shape[-1]
    valid = dff // tn
    last = valid - 1
    return pl.pallas_call(
        functools.partial(_ffn_up_kernel, tiles_per_seq=seq // tm, valid_tiles=valid),
        grid=(dff_pad // tn, m // tm),
        in_specs=[pl.BlockSpec((tm, k), lambda j, i: (i, 0)),
                  pl.BlockSpec((None, k, tn), lambda j, i: (layer, 0, jnp.minimum(j, last))),
                  pl.BlockSpec((None, k, tn), lambda j, i: (layer, 0, valid + jnp.minimum(j, last))),
                  pl.BlockSpec((None, FFN_CONV, tn), lambda j, i: (layer, 0, jnp.minimum(j, last))),
                  pl.BlockSpec((None, 1, tn), lambda j, i: (layer, 0, jnp.minimum(j, last)))],
        out_specs=pl.BlockSpec((tm, tn), lambda j, i: (i, j)),
        out_shape=jax.ShapeDtypeStruct((m, dff_pad), BF16),
        scratch_shapes=[pltpu.VMEM((k, tn), BF16), pltpu.VMEM((k, tn), BF16),
                        pltpu.VMEM((SUBLANES, tn), F32)],
        compiler_params=_cparams("arbitrary", "arbitrary"),
        name="ffn_up",
    )(x, w, w, conv_w, conv_b)


def _rotate_half_cols(w):
    half = MLA_ROPE // 2
    return jnp.concatenate([-w[..., half:], w[..., :half]], axis=-1)


SRC_CKV = MLA_Q_RANK
SRC_KR = SRC_CKV + MLA_KV_RANK
SRC_LRU = SRC_KR + MLA_ROPE
SRC_RW = SRC_LRU + 2 * LRU_WIDTH
SRC_XWA = SRC_RW + 3 * RW_WIDTH
SRC_XG = SRC_XWA + RW_DECAY_LORA + RW_AAA_LORA
SRC_GATES = SRC_XG + RW_GATE_LORA


W_IN_TILE = 256
_T_LRU = (3 * D_MODEL) // W_IN_TILE
_T_RW = _T_LRU + (2 * LRU_WIDTH) // W_IN_TILE
_T_CQ = _T_RW + (3 * RW_WIDTH) // W_IN_TILE
_T_CKV = _T_CQ + Q_PAD // W_IN_TILE
_T_MIX = _T_CKV + MLA_KV_RANK // W_IN_TILE
_T_XG = _T_MIX + 1


def _w_in_src_row(t):
    tile = W_IN_TILE
    return jnp.where(
        t < _T_LRU, SRC_GATES + tile * t, jnp.where(
            t < _T_RW, SRC_LRU + tile * (t - _T_LRU), jnp.where(
                t < _T_CQ, SRC_RW + tile * (t - _T_RW), jnp.where(
                    t < _T_CKV, tile * (t - _T_CQ), jnp.where(
                        t < _T_MIX, SRC_CKV + tile * (t - _T_CKV), jnp.where(
                            t == _T_MIX, SRC_KR, SRC_XG))))))


def _w_in_kernel(a_ref, b_ref, o_ref):
    t = pl.program_id(0)
    a = a_ref[...]

    @pl.when(t != _T_MIX)
    def _():
        valid = jnp.where(t == _T_CKV - 1, MLA_Q_RANK - (Q_PAD - W_IN_TILE),
                          jnp.where(t == _T_XG, RW_GATE_LORA, W_IN_TILE))
        row = lax.broadcasted_iota(jnp.int32, a.shape, 0)
        o_ref[...] = jnp.where(row < valid, a, 0.0).astype(BF16)

    @pl.when(t == _T_MIX)
    def _():
        half = MLA_ROPE // 2
        lora = RW_DECAY_LORA + RW_AAA_LORA
        o_ref[...] = jnp.concatenate(
            [a[0:MLA_ROPE], -a[half:MLA_ROPE], a[0:half], b_ref[0:lora]], axis=0).astype(BF16)


def relayout_w_in(w_t, layer, *, k):
    window = lambda index: pl.BlockSpec((None, pl.Element(W_IN_TILE), pl.Element(k)), index)
    return pl.pallas_call(
        _w_in_kernel,
        grid=(IN_COLS_PAD // W_IN_TILE,),
        in_specs=[window(lambda t: (layer, pl.multiple_of(_w_in_src_row(t), SUBLANES), 0)),
                  window(lambda t: (layer, SRC_XWA, 0))],
        out_specs=pl.BlockSpec((W_IN_TILE, k), lambda t: (t, 0)),
        out_shape=jax.ShapeDtypeStruct((IN_COLS_PAD, k), BF16),
        compiler_params=_cparams("arbitrary"),
        name="w_in_relayout",
    )(w_t, w_t)


def _cast_kernel(s_ref, o_ref, *, valid_blocks):
    i = pl.program_id(0)

    @pl.when(i < valid_blocks)
    def _():
        o_ref[...] = s_ref[...].astype(o_ref.dtype)

    @pl.when(i >= valid_blocks)
    def _():
        o_ref[...] = jnp.zeros_like(o_ref)


def cast_bf16(w, layer=None, *, tr, rows_out=None):
    r, c = w.shape[-2:]
    rows_out = r if rows_out is None else rows_out
    valid = r // tr
    if layer is None:
        src = pl.BlockSpec((tr, c), lambda i: (jnp.minimum(i, valid - 1), 0))
    else:
        src = pl.BlockSpec((None, tr, c), lambda i: (layer, jnp.minimum(i, valid - 1), 0))
    return pl.pallas_call(
        functools.partial(_cast_kernel, valid_blocks=valid),
        grid=(rows_out // tr,),
        in_specs=[src],
        out_specs=pl.BlockSpec((tr, c), lambda i: (i, 0)),
        out_shape=jax.ShapeDtypeStruct((rows_out, c), BF16),
        compiler_params=_cparams("parallel"),
        name="cast_bf16",
    )(w)


def _prep_w_uq(w):
    rope = w[..., MLA_NOPE:]
    w = jnp.concatenate([w, _rotate_half_cols(rope)], axis=-1)
    w = w.reshape(MLA_Q_RANK, MLA_HEADS * 2 * LANES)
    return jnp.pad(w, ((0, Q_PAD - MLA_Q_RANK), (0, 0))).astype(BF16)


def _rope_table(positions):
    inv_freq = ROPE_THETA ** (-jnp.arange(0, MLA_ROPE, 2, dtype=F32) / MLA_ROPE)
    ang = positions.astype(F32)[..., None] * inv_freq
    cos, sin = jnp.cos(ang), jnp.sin(ang)
    return jnp.concatenate([cos, cos, sin, sin], axis=-1).reshape(-1, 2 * MLA_ROPE)


class _Tiles:
    in_proj = (1024, 1024)
    mla_up_rows = 512
    attn_q = 512
    attn_heads = 2
    lru_rows = 512
    rwkv_prep_rows = 256
    merge = (512, 1024)
    out_mm = (1024, 1024, 1024)
    ln_rows = 256
    ffn_up = (1024, 256)
    cast_rows = 256


def kernel(x, positions, w_in, mla_q_norm, mla_w_uq, mla_kv_norm, mla_w_ukv, lru_conv_w, lru_conv_b, lru_w_a, lru_b_a, lru_w_x, lru_b_x, lru_lambda, rw_mu, rw_w0, rw_w2, rw_a0, rw_a2, rw_g2, rw_k_k, rw_k_a, rw_r_k, rw_gn_g, rw_gn_b, w_o_mla, w_o_lru, w_o_rwkv, w_out, ln1_g, ln1_b, ffn_w_up, ffn_conv_w, ffn_conv_b, ffn_w_down, ln2_g, ln2_b):
    batch, seq, d = x.shape
    m = batch * seq
    t = _Tiles
    cs = _rope_table(positions)
    xf = x.reshape(m, d)
    cast = functools.partial(cast_bf16, tr=t.cast_rows)
    xb = cast(xf)
    w_ukv_all = mla_w_ukv.reshape(DEPTH, MLA_KV_RANK, -1)
    conv_b_all = ffn_conv_b.reshape(DEPTH, 1, -1)
    w_in_t = jnp.transpose(w_in, (0, 2, 1))
    for l in range(DEPTH):
        proj = matmul_nt(xb, relayout_w_in(w_in_t, l, k=d), tm=t.in_proj[0], tn=t.in_proj[1])

        q_gain = jnp.pad(mla_q_norm[l], (0, Q_PAD - MLA_Q_RANK)).reshape(1, Q_PAD)
        q = mla_q_up(proj, q_gain, _prep_w_uq(mla_w_uq[l]), cs, batch=batch, seq=seq, tm=t.mla_up_rows)
        k, v = mla_kv_up(proj, mla_kv_norm[l].reshape(1, -1), cast(w_ukv_all, l), cs, batch=batch,
                         seq=seq, tm=t.mla_up_rows)
        y_a = mla_attention(q, k, v, tq=t.attn_q, hps=t.attn_heads).reshape(m, -1)

        y_l = rglru_branch(proj, lru_conv_w[l], lru_conv_b[l], lru_w_a[l], lru_b_a[l],
                           lru_w_x[l], lru_b_x[l], lru_lambda[l], batch=batch, seq=seq, ts=t.lru_rows)

        rw = rwkv_prep(proj, rw_mu[l], rw_w0[l], rw_w2[l], rw_a0[l], rw_a2[l], rw_g2[l],
                       rw_k_k[l], rw_k_a[l], batch=batch, seq=seq, ts=t.rwkv_prep_rows)
        y_r = rwkv_recurrence(*rw, rw_r_k[l], rw_gn_g[l], rw_gn_b[l], batch=batch, seq=seq)
        y_r = y_r.reshape(m, RW_WIDTH)

        merged = merge_branches(y_a, y_l, y_r, cast(w_o_mla, l), cast(w_o_lru, l), cast(w_o_rwkv, l),
                                proj, tm=t.merge[0], tn=t.merge[1])
        mm = dict(tm=t.out_mm[0], tn=t.out_mm[1], tk=t.out_mm[2], ln_rows=t.ln_rows)
        xf, xb = matmul_residual_layernorm(merged, cast(w_out, l), xf, ln1_g[l], ln1_b[l],
                                           name="mixer_out", **mm)

        h = ffn_up(xb, ffn_w_up, ffn_conv_w, conv_b_all, l, seq=seq, tm=t.ffn_up[0],
                   tn=t.ffn_up[1], dff_pad=D_FF_PAD)
        xf, xb = matmul_residual_layernorm(h, cast(ffn_w_down, l, rows_out=D_FF_PAD), xf, ln2_g[l],
                                           ln2_b[l], name="ffn_down", **mm)
    return xf.reshape(batch, seq, d)
```

```python
import functools

import jax
import jax.numpy as jnp
from jax import lax
from jax.experimental import pallas as pl
from jax.experimental.pallas import tpu as pltpu

F32 = jnp.float32
BF16 = jnp.bfloat16

D_MODEL = 4096
DEPTH = 2
CHUNK = 64
MLA_HEADS = 16
MLA_Q_RANK = 896
MLA_KV_RANK = 512
MLA_NOPE = 128
MLA_ROPE = 64
MLA_V = 128
ROPE_THETA = 10000.0
LRU_WIDTH = 1024
LRU_BLOCKS = 8
LRU_BLOCK = LRU_WIDTH // LRU_BLOCKS
LRU_CONV = 4
LRU_C = 8.0
RW_WIDTH = 1024
RW_HEAD = 64
RW_HEADS = RW_WIDTH // RW_HEAD
RW_DECAY_LORA = 64
RW_AAA_LORA = 64
RW_GATE_LORA = 160
D_FF = 11008
FFN_CONV = 3
ALPHA = (2 * DEPTH) ** 0.25
LN_EPS = 1e-5
RMS_EPS = 1e-6
GN_EPS = 64e-5

LANES = 128
SUBLANES = 8
VMEM_LIMIT_BYTES = 56 * 1024 * 1024

Q_PAD = 1024
D_FF_PAD = 11264
XG_PAD = 256
OFF_GATES = 0
OFF_LRU_X = 3 * D_MODEL
OFF_LRU_G = OFF_LRU_X + LRU_WIDTH
OFF_RW_R = OFF_LRU_G + LRU_WIDTH
OFF_RW_K = OFF_RW_R + RW_WIDTH
OFF_RW_V = OFF_RW_K + RW_WIDTH
OFF_CQ = OFF_RW_V + RW_WIDTH
OFF_CKV = OFF_CQ + Q_PAD
OFF_KR = OFF_CKV + MLA_KV_RANK
OFF_XWA = OFF_KR + 2 * MLA_ROPE
OFF_XG = OFF_XWA + RW_DECAY_LORA + RW_AAA_LORA
IN_COLS_PAD = OFF_XG + XG_PAD

RW_GROUP_PAIRS = 4
_SQRT_2_OVER_PI = 0.7978845608028654
_LOG2_E = 1.4426950408889634


def _cparams(*sem):
    return pltpu.CompilerParams(dimension_semantics=sem, vmem_limit_bytes=VMEM_LIMIT_BYTES)


def _gelu_tanh(x):
    return 0.5 * x * (1.0 + jnp.tanh(_SQRT_2_OVER_PI * (x + 0.044715 * (x * x * x))))


def _softplus(z):
    return jnp.maximum(z, 0.0) + jnp.log1p(jnp.exp(-jnp.abs(z)))


def _shift_rows(x, shift, tail):
    row = lax.broadcasted_iota(jnp.int32, x.shape, 0)
    xs = pltpu.roll(x, shift, axis=0)
    for j in range(shift):
        src = SUBLANES - shift + j
        xs = jnp.where(row == j, tail[src:src + 1, :], xs)
    return xs


def _group_ones(rows, cols, group):
    shift = group.bit_length() - 1
    r = lax.broadcasted_iota(jnp.int32, (rows, cols), 0)
    c = lax.broadcasted_iota(jnp.int32, (rows, cols), 1)
    return jnp.where(((r & (cols - 1)) >> shift) == (c >> shift), 1.0, 0.0).astype(BF16)


def _head_sum(x, ones_hl):
    hi = x.astype(BF16)
    lo = (x - hi.astype(F32)).astype(BF16)
    return jnp.dot(jnp.concatenate([hi, lo], axis=1), ones_hl, preferred_element_type=F32)


def _mm_kernel(a_ref, b_ref, o_ref):
    o_ref[...] = lax.dot_general(a_ref[...], b_ref[...], (((1,), (1,)), ((), ())),
                                 preferred_element_type=F32).astype(o_ref.dtype)


def matmul_nt(a, b, *, tm, tn, out_dtype=F32):
    m, k = a.shape
    n, _ = b.shape
    return pl.pallas_call(
        _mm_kernel,
        grid=(m // tm, n // tn),
        in_specs=[pl.BlockSpec((tm, k), lambda i, j: (i, 0)),
                  pl.BlockSpec((tn, k), lambda i, j: (j, 0))],
        out_specs=pl.BlockSpec((tm, tn), lambda i, j: (i, j)),
        out_shape=jax.ShapeDtypeStruct((m, n), out_dtype),
        compiler_params=_cparams("parallel", "parallel"),
        name="in_proj",
    )(a, b)


def _mm_res_kernel(a_ref, w_ref, x_ref, o_ref, acc_ref):
    k = pl.program_id(2)
    last = pl.num_programs(2) - 1

    @pl.when(k == 0)
    def _():
        acc_ref[...] = ALPHA * x_ref[...]

    @pl.when(k < last)
    def _():
        acc_ref[...] += jnp.dot(a_ref[...], w_ref[...], preferred_element_type=F32)

    @pl.when(k == last)
    def _():
        o_ref[...] = acc_ref[...] + jnp.dot(a_ref[...], w_ref[...], preferred_element_type=F32)


def _ln_kernel(y_ref, g_ref, b_ref, o_ref, ob_ref):
    y = y_ref[...]
    mu = jnp.mean(y, axis=-1, keepdims=True)
    yc = y - mu
    var = jnp.mean(yc * yc, axis=-1, keepdims=True)
    o = yc * lax.rsqrt(var + LN_EPS) * g_ref[...] + b_ref[...]
    o_ref[...] = o
    ob_ref[...] = o.astype(BF16)


def matmul_residual_layernorm(a, w, x, g, b, *, tm, tn, tk, ln_rows, name):
    m, k = a.shape
    n = w.shape[1]
    y = pl.pallas_call(
        _mm_res_kernel,
        grid=(m // tm, n // tn, k // tk),
        in_specs=[pl.BlockSpec((tm, tk), lambda i, j, kk: (i, kk)),
                  pl.BlockSpec((tk, tn), lambda i, j, kk: (kk, j)),
                  pl.BlockSpec((tm, tn), lambda i, j, kk: (i, j))],
        out_specs=pl.BlockSpec((tm, tn), lambda i, j, kk: (i, j)),
        out_shape=jax.ShapeDtypeStruct((m, n), F32),
        scratch_shapes=[pltpu.VMEM((tm, tn), F32)],
        compiler_params=_cparams("parallel", "parallel", "arbitrary"),
        name=name,
    )(a, w, x)
    return pl.pallas_call(
        _ln_kernel,
        grid=(m // ln_rows,),
        in_specs=[pl.BlockSpec((ln_rows, n), lambda i: (i, 0)),
                  pl.BlockSpec((1, n), lambda i: (0, 0)),
                  pl.BlockSpec((1, n), lambda i: (0, 0))],
        out_specs=[pl.BlockSpec((ln_rows, n), lambda i: (i, 0)),
                   pl.BlockSpec((ln_rows, n), lambda i: (i, 0))],
        out_shape=[jax.ShapeDtypeStruct((m, n), F32), jax.ShapeDtypeStruct((m, n), BF16)],
        compiler_params=_cparams("parallel"),
        name="layernorm",
    )(y, g.reshape(1, n), b.reshape(1, n))


def _rope_half(block, cs):
    p = block * cs
    return p + pltpu.roll(p, MLA_ROPE, axis=1)


def _q_up_kernel(p_ref, g_ref, w_ref, cs_ref, q_ref, *, rank):
    x = p_ref[...]
    ms = jnp.sum(x * x, axis=-1, keepdims=True) * (1.0 / rank)
    xn = (x * lax.rsqrt(ms + RMS_EPS) * g_ref[...]).astype(BF16)
    cs = cs_ref[...]
    hw = 2 * LANES
    for h in range(q_ref.shape[1]):
        acc = jnp.dot(xn, w_ref[:, h * hw:(h + 1) * hw], preferred_element_type=F32)
        rot = _rope_half(acc[:, LANES:hw], cs)
        q_ref[0, h, :, 0:MLA_NOPE] = acc[:, 0:MLA_NOPE].astype(BF16)
        q_ref[0, h, :, MLA_NOPE:MLA_NOPE + MLA_ROPE] = rot[:, 0:MLA_ROPE].astype(BF16)


def mla_q_up(proj, gain, w, cs, *, batch, seq, tm):
    nh = MLA_HEADS
    spt = seq // tm
    return pl.pallas_call(
        functools.partial(_q_up_kernel, rank=MLA_Q_RANK),
        grid=(batch * spt,),
        in_specs=[pl.BlockSpec((tm, Q_PAD), lambda i: (i, OFF_CQ // Q_PAD)),
                  pl.BlockSpec((1, Q_PAD), lambda i: (0, 0)),
                  pl.BlockSpec(w.shape, lambda i: (0, 0)),
                  pl.BlockSpec((tm, LANES), lambda i: (i, 0))],
        out_specs=pl.BlockSpec((1, nh, tm, MLA_NOPE + MLA_ROPE), lambda i: (i // spt, 0, i % spt, 0)),
        out_shape=jax.ShapeDtypeStruct((batch, nh, seq, MLA_NOPE + MLA_ROPE), BF16),
        compiler_params=_cparams("parallel"),
        name="mla_q_up",
    )(proj, gain, w, cs)


def _kv_up_kernel(p_ref, kr_ref, g_ref, w_ref, cs_ref, k_ref, v_ref, *, rank):
    x = p_ref[...]
    ms = jnp.sum(x * x, axis=-1, keepdims=True) * (1.0 / rank)
    xn = (x * lax.rsqrt(ms + RMS_EPS) * g_ref[...]).astype(BF16)
    krope = _rope_half(kr_ref[...], cs_ref[...])[:, 0:MLA_ROPE].astype(BF16)
    hw = MLA_NOPE + MLA_V
    for h in range(k_ref.shape[1]):
        acc = jnp.dot(xn, w_ref[:, h * hw:(h + 1) * hw], preferred_element_type=F32)
        k_ref[0, h, :, 0:MLA_NOPE] = acc[:, 0:MLA_NOPE].astype(BF16)
        k_ref[0, h, :, MLA_NOPE:MLA_NOPE + MLA_ROPE] = krope
        v_ref[0, h] = acc[:, MLA_NOPE:hw].astype(BF16)


def mla_kv_up(proj, gain, w, cs, *, batch, seq, tm):
    nh = MLA_HEADS
    spt = seq // tm
    dk = MLA_NOPE + MLA_ROPE
    return pl.pallas_call(
        functools.partial(_kv_up_kernel, rank=MLA_KV_RANK),
        grid=(batch * spt,),
        in_specs=[pl.BlockSpec((tm, MLA_KV_RANK), lambda i: (i, OFF_CKV // MLA_KV_RANK)),
                  pl.BlockSpec((tm, LANES), lambda i: (i, OFF_KR // LANES)),
                  pl.BlockSpec((1, MLA_KV_RANK), lambda i: (0, 0)),
                  pl.BlockSpec(w.shape, lambda i: (0, 0)),
                  pl.BlockSpec((tm, LANES), lambda i: (i, 0))],
        out_specs=[pl.BlockSpec((1, nh, tm, dk), lambda i: (i // spt, 0, i % spt, 0)),
                   pl.BlockSpec((1, nh, tm, MLA_V), lambda i: (i // spt, 0, i % spt, 0))],
        out_shape=[jax.ShapeDtypeStruct((batch, nh, seq, dk), BF16),
                   jax.ShapeDtypeStruct((batch, nh, seq, MLA_V), BF16)],
        compiler_params=_cparams("parallel"),
        name="mla_kv_up",
    )(proj, proj, gain, w, cs)


def _flash_kernel(q_ref, k_ref, v_ref, o_ref, *, tq, scale):
    qi = pl.program_id(2)
    heads = range(q_ref.shape[1])
    qs = [q_ref[0, h] for h in heads]

    def scores(h, j):
        k = k_ref[0, h, pl.ds(pl.multiple_of(j * tq, tq), tq), :]
        return lax.dot_general(qs[h], k, (((1,), (1,)), ((), ())),
                               preferred_element_type=F32) * (scale * _LOG2_E)

    def update(h, j, s, m, l, acc):
        v = v_ref[0, h, pl.ds(pl.multiple_of(j * tq, tq), tq), :]
        m_new = jnp.maximum(m, jnp.max(s, axis=-1, keepdims=True))
        alpha = jnp.exp2(m - m_new)
        p = jnp.exp2(s - m_new)
        l = alpha * l + jnp.sum(p, axis=-1, keepdims=True)
        acc = alpha * acc + jnp.dot(p.astype(BF16), v, preferred_element_type=F32)
        return m_new, l, acc

    def pair(jj, carry):
        out = []
        for h in heads:
            s_a, s_b = scores(h, 2 * jj), scores(h, 2 * jj + 1)
            out.append(update(h, 2 * jj + 1, s_b, *update(h, 2 * jj, s_a, *carry[h])))
        return tuple(out)

    def single(_, carry):
        return tuple(update(h, qi - 1, scores(h, qi - 1), *carry[h]) for h in heads)

    carry = tuple((jnp.full((tq, 1), -1e30, F32), jnp.zeros((tq, 1), F32),
                   jnp.zeros((tq, MLA_V), F32)) for _ in heads)
    carry = lax.fori_loop(0, qi >> 1, pair, carry)
    carry = lax.fori_loop(0, qi & 1, single, carry)
    shift = CHUNK.bit_length() - 1
    qc = lax.broadcasted_iota(jnp.int32, (tq, tq), 0) >> shift
    kc = lax.broadcasted_iota(jnp.int32, (tq, tq), 1) >> shift
    for h in heads:
        m, l, acc = update(h, qi, jnp.where(kc <= qc, scores(h, qi), -jnp.inf), *carry[h])
        o_ref[0, :, h * MLA_V:(h + 1) * MLA_V] = (acc / l).astype(o_ref.dtype)


def mla_attention(q, k, v, *, tq, hps):
    batch, nh, seq, dk = q.shape
    scale = (MLA_NOPE + MLA_ROPE) ** -0.5
    return pl.pallas_call(
        functools.partial(_flash_kernel, tq=tq, scale=scale),
        grid=(batch, nh // hps, seq // tq),
        in_specs=[pl.BlockSpec((1, hps, tq, dk), lambda b, h, i: (b, h, i, 0)),
                  pl.BlockSpec((1, hps, seq, dk), lambda b, h, i: (b, h, 0, 0)),
                  pl.BlockSpec((1, hps, seq, MLA_V), lambda b, h, i: (b, h, 0, 0))],
        out_specs=pl.BlockSpec((1, tq, hps * MLA_V), lambda b, h, i: (b, i, h)),
        out_shape=jax.ShapeDtypeStruct((batch, seq, nh * MLA_V), BF16),
        compiler_params=_cparams("parallel", "parallel", "arbitrary"),
        name="mla_attention",
    )(q, k, v)


def _lru_kernel(x_ref, g_ref, cw_ref, cb_ref, wa_ref, ba_ref, wx_ref, bx_ref, lam_ref, o_ref,
                tail_ref, h_ref, a_scr, u_scr):
    @pl.when(pl.program_id(1) == 0)
    def _():
        tail_ref[...] = jnp.zeros_like(tail_ref)
        h_ref[...] = jnp.zeros_like(h_ref)

    x = x_ref[...]
    ts = x.shape[0]
    tail = tail_ref[...]
    xc = x * cw_ref[LRU_CONV - 1:LRU_CONV, :] + cb_ref[...]
    for sh in range(1, LRU_CONV):
        xc = xc + _shift_rows(x, sh, tail) * cw_ref[LRU_CONV - 1 - sh:LRU_CONV - sh, :]
    tail_ref[...] = x[ts - SUBLANES:ts, :]

    xb = xc.astype(BF16)
    ra, rx = [], []
    for n in range(LRU_BLOCKS):
        blk = xb[:, n * LRU_BLOCK:(n + 1) * LRU_BLOCK]
        ra.append(jnp.dot(blk, wa_ref[n], preferred_element_type=F32))
        rx.append(jnp.dot(blk, wx_ref[n], preferred_element_type=F32))
    r = jax.nn.sigmoid(jnp.concatenate(ra, axis=1) + ba_ref[...])
    gi = jax.nn.sigmoid(jnp.concatenate(rx, axis=1) + bx_ref[...])
    log_a = -LRU_C * r * _softplus(-lam_ref[...])
    a_scr[...] = jnp.exp(log_a)
    u_scr[...] = jnp.sqrt(1.0 - jnp.exp(2.0 * log_a)) * (gi * xc)

    row8 = lax.broadcasted_iota(jnp.int32, (SUBLANES, x.shape[1]), 0)

    def body(grp, h):
        base = pl.multiple_of(grp * SUBLANES, SUBLANES)
        a8 = a_scr[pl.ds(base, SUBLANES), :]
        u8 = u_scr[pl.ds(base, SUBLANES), :]
        hs = u8
        for s in range(SUBLANES):
            h = a8[s:s + 1, :] * h + u8[s:s + 1, :]
            hs = jnp.where(row8 == s, h, hs)
        u_scr[pl.ds(base, SUBLANES), :] = hs
        return h

    h_ref[...] = lax.fori_loop(0, ts // SUBLANES, body, h_ref[...])
    o_ref[...] = (u_scr[...] * _gelu_tanh(g_ref[...])).astype(o_ref.dtype)


def rglru_branch(proj, conv_w, conv_b, w_a, b_a, w_x, b_x, lam, *, batch, seq, ts):
    w = LRU_WIDTH
    spt = seq // ts
    vec = lambda: pl.BlockSpec((1, w), lambda b, s: (0, 0))
    blockdiag = lambda: pl.BlockSpec((LRU_BLOCKS, LRU_BLOCK, LRU_BLOCK), lambda b, s: (0, 0, 0))
    return pl.pallas_call(
        _lru_kernel,
        grid=(batch, spt),
        in_specs=[pl.BlockSpec((ts, w), lambda b, s: (b * spt + s, OFF_LRU_X // w)),
                  pl.BlockSpec((ts, w), lambda b, s: (b * spt + s, OFF_LRU_G // w)),
                  pl.BlockSpec((LRU_CONV, w), lambda b, s: (0, 0)),
                  vec(), blockdiag(), vec(), blockdiag(), vec(), vec()],
        out_specs=pl.BlockSpec((ts, w), lambda b, s: (b * spt + s, 0)),
        out_shape=jax.ShapeDtypeStruct((batch * seq, w), BF16),
        scratch_shapes=[pltpu.VMEM((SUBLANES, w), F32), pltpu.VMEM((1, w), F32),
                        pltpu.VMEM((ts, w), F32), pltpu.VMEM((ts, w), F32)],
        compiler_params=_cparams("parallel", "arbitrary"),
        name="rglru",
    )(proj, proj, conv_w, conv_b.reshape(1, w), w_a.astype(BF16), b_a.reshape(1, w),
      w_x.astype(BF16), b_x.reshape(1, w), lam.reshape(1, w))


def _rwkv_prep_kernel(r_ref, k_ref, v_ref, wa_ref, xg_ref,
                      mur_ref, muk_ref, muv_ref, muwa_ref, mug_ref,
                      w0_ref, w2_ref, a0_ref, a2_ref, g2_ref, kkw_ref, ka_ref,
                      ro_ref, ko_ref, vo_ref, nkk_ref, b_ref, d_ref, g_ref,
                      tr_ref, tk_ref, tv_ref, twa_ref, tg_ref):
    @pl.when(pl.program_id(1) == 0)
    def _():
        for t in (tr_ref, tk_ref, tv_ref, twa_ref, tg_ref):
            t[...] = jnp.zeros_like(t)

    def mix(x_ref, mu_ref, tail_ref):
        x = x_ref[...]
        prev = _shift_rows(x, 1, tail_ref[...])
        tail_ref[...] = x[x.shape[0] - SUBLANES:, :]
        return x + mu_ref[...] * (prev - x)

    r = mix(r_ref, mur_ref, tr_ref)
    k = mix(k_ref, muk_ref, tk_ref)
    v = mix(v_ref, muv_ref, tv_ref)
    xwa = mix(wa_ref, muwa_ref, twa_ref)
    xg = mix(xg_ref, mug_ref, tg_ref)
    w = -_softplus(-(w0_ref[...] + jnp.dot(jnp.tanh(xwa).astype(BF16), w2_ref[...],
                                           preferred_element_type=F32))) - 0.5
    a = jax.nn.sigmoid(a0_ref[...] + jnp.dot(xwa.astype(BF16), a2_ref[...], preferred_element_type=F32))
    g = jnp.dot(jax.nn.sigmoid(xg).astype(BF16), g2_ref[...], preferred_element_type=F32)

    kk = k * kkw_ref[...]
    ones_hl = _group_ones(2 * LANES, LANES, RW_HEAD)
    sq = kk * kk
    ss = jnp.concatenate([_head_sum(sq[:, c * LANES:(c + 1) * LANES], ones_hl)
                          for c in range(RW_WIDTH // LANES)], axis=1)
    kk = kk / jnp.maximum(jnp.sqrt(ss), 1e-12)

    ro_ref[...] = r
    ko_ref[...] = k * (1.0 + (a - 1.0) * ka_ref[...])
    vo_ref[...] = v
    nkk_ref[...] = -kk
    b_ref[...] = kk * a
    d_ref[...] = jnp.exp(-jnp.exp(w))
    g_ref[...] = g


def rwkv_prep(proj, mu, w0, w2, a0, a2, g2, k_k, k_a, *, batch, seq, ts):
    w = RW_WIDTH
    spt = seq // ts
    lora = RW_DECAY_LORA + RW_AAA_LORA
    row = lambda width, col: pl.BlockSpec((ts, width), lambda b, s: (b * spt + s, col))
    vec = lambda width: pl.BlockSpec((1, width), lambda b, s: (0, 0))
    mat = lambda rows: pl.BlockSpec((rows, w), lambda b, s: (0, 0))
    mu_r, mu_k, mu_v = (mu[i * w:(i + 1) * w].reshape(1, w) for i in range(3))
    mu_wa = mu[3 * w:3 * w + lora].reshape(1, lora)
    mu_g = jnp.pad(mu[3 * w + lora:], (0, XG_PAD - RW_GATE_LORA)).reshape(1, XG_PAD)
    zeros = jnp.zeros((RW_DECAY_LORA, w), w2.dtype)
    w2p = jnp.concatenate([w2, zeros], axis=0).astype(BF16)
    a2p = jnp.concatenate([zeros, a2], axis=0).astype(BF16)
    g2p = jnp.pad(g2, ((0, XG_PAD - RW_GATE_LORA), (0, 0))).astype(BF16)
    out = jax.ShapeDtypeStruct((batch * seq, w), F32)
    return pl.pallas_call(
        _rwkv_prep_kernel,
        grid=(batch, spt),
        in_specs=[row(w, OFF_RW_R // w), row(w, OFF_RW_K // w), row(w, OFF_RW_V // w),
                  row(lora, OFF_XWA // lora), row(XG_PAD, OFF_XG // XG_PAD),
                  vec(w), vec(w), vec(w), vec(lora), vec(XG_PAD),
                  vec(w), mat(lora), vec(w), mat(lora), mat(XG_PAD), vec(w), vec(w)],
        out_specs=[pl.BlockSpec((ts, w), lambda b, s: (b * spt + s, 0))] * 7,
        out_shape=[out] * 7,
        scratch_shapes=[pltpu.VMEM((SUBLANES, w), F32)] * 3
                       + [pltpu.VMEM((SUBLANES, lora), F32), pltpu.VMEM((SUBLANES, XG_PAD), F32)],
        compiler_params=_cparams("parallel", "arbitrary"),
        name="rwkv_prep",
    )(proj, proj, proj, proj, proj, mu_r, mu_k, mu_v, mu_wa, mu_g,
      w0.reshape(1, w), w2p, a0.reshape(1, w), a2p, g2p, k_k.reshape(1, w), k_a.reshape(1, w))


def _rwkv_rec_kernel(r_ref, k_ref, v_ref, nkk_ref, b_ref, d_ref, g_ref, rk_ref, gng_ref, gnb_ref,
                     y_ref, st_ref, q_ref, ot_ref):
    nb, tau, width = r_ref.shape
    npair = width // LANES
    hd = RW_HEAD
    pairs = [(bi, p) for bi in range(nb) for p in range(npair)]
    flat = [(i, bi, p) for i, (bi, p) in enumerate(pairs)]
    groups = [flat[j:j + RW_GROUP_PAIRS] for j in range(0, len(flat), RW_GROUP_PAIRS)]

    @pl.when(pl.program_id(0) == 0)
    def _():
        st_ref[...] = jnp.zeros_like(st_ref)

    q_ref[...] = jnp.zeros_like(q_ref)
    ot_ref[...] = jnp.zeros_like(ot_ref)

    row = lax.broadcasted_iota(jnp.int32, (hd, LANES), 0)
    lane = lax.broadcasted_iota(jnp.int32, (hd, LANES), 1)
    lane_in_head = lane & (hd - 1)
    diag = lane_in_head == row
    ones2 = _group_ones(2 * LANES, 2 * LANES, hd)

    def group(grp, carry):
        base = pl.multiple_of(grp * SUBLANES, SUBLANES)
        tile = lambda ref, bi, p: ref[bi, pl.ds(base, SUBLANES), pl.ds(p * LANES, LANES)]
        vcb = [None] * len(groups)
        for s in range(SUBLANES):
            sel = lane_in_head == (base + s - 1)
            for gi, grp_pairs in enumerate(groups):
                lhs = []
                for i, bi, p in grp_pairs:
                    pm = (st_ref[bi, p] * tile(nkk_ref, bi, p)[s:s + 1, :]).astype(BF16)
                    lhs.append(jnp.concatenate([pm, q_ref[pl.ds(i * hd, hd), :]], axis=1))
                if s % 2 == 0:
                    for i, bi, p in grp_pairs:
                        v8 = tile(v_ref, bi, p)
                        lhs.append(jnp.concatenate(
                            [jnp.where(diag, v8[s:s + 1, :], 0.0).astype(BF16),
                             jnp.where(diag, v8[s + 1:s + 2, :], 0.0).astype(BF16)], axis=1))
                out = jnp.dot(jnp.concatenate(lhs, axis=0), ones2, preferred_element_type=F32)
                if s % 2 == 0:
                    vcb[gi] = out[len(grp_pairs) * hd:, :]
                for j, (i, bi, p) in enumerate(grp_pairs):
                    rows = slice(j * hd, (j + 1) * hd)
                    ot_ref[bi, p] = jnp.where(sel, out[rows, LANES:], ot_ref[bi, p])
                    vc = vcb[gi][rows, 0:LANES] if s % 2 == 0 else vcb[gi][rows, LANES:]
                    new = (st_ref[bi, p] * tile(d_ref, bi, p)[s:s + 1, :]
                           + out[rows, 0:LANES] * tile(b_ref, bi, p)[s:s + 1, :]
                           + vc * tile(k_ref, bi, p)[s:s + 1, :])
                    st_ref[bi, p] = new
                    q_ref[pl.ds(i * hd, hd), :] = (new * tile(r_ref, bi, p)[s:s + 1, :]).astype(BF16)
        return carry

    lax.fori_loop(0, tau // SUBLANES, group, 0)

    o_last = jnp.dot(q_ref[...], ones2[0:LANES, 0:LANES], preferred_element_type=F32)
    sel = lane_in_head == (tau - 1)
    for i, (bi, p) in enumerate(pairs):
        ot_ref[bi, p] = jnp.where(sel, o_last[i * hd:(i + 1) * hd, :], ot_ref[bi, p])

    ones_hl = _group_ones(2 * LANES, LANES, hd)
    first_head = lane < hd
    inv_hd = 1.0 / hd
    for bi in range(nb):
        nat = []
        for p0 in range(0, npair, 2):
            tr = jnp.concatenate([ot_ref[bi, p0], ot_ref[bi, p0 + 1]], axis=0).T
            top, bot = tr[0:hd, :], tr[hd:2 * hd, :]
            nat.append(jnp.where(first_head, top, pltpu.roll(bot, hd, axis=1)))
            nat.append(jnp.where(first_head, pltpu.roll(top, hd, axis=1), bot))
        cols = [pl.ds(p * LANES, LANES) for p in range(npair)]
        o = jnp.concatenate(nat, axis=0)
        mean = _head_sum(o, ones_hl) * inv_hd
        oc = o - mean
        var = _head_sum(oc * oc, ones_hl) * inv_hd
        rkr = jnp.concatenate([r_ref[bi, :, c] * k_ref[bi, :, c] * rk_ref[:, c] for c in cols], axis=0)
        bonus = _head_sum(rkr, ones_hl)
        on = oc * lax.rsqrt(var + GN_EPS)
        for p, c in enumerate(cols):
            rows = slice(p * tau, (p + 1) * tau)
            y = on[rows] * gng_ref[:, c] + gnb_ref[:, c] + bonus[rows] * v_ref[bi, :, c]
            y_ref[bi, :, c] = (y * g_ref[bi, :, c]).astype(y_ref.dtype)


def rwkv_recurrence(r, k, v, nkk, b, d, g, r_k, gn_g, gn_b, *, batch, seq):
    width = RW_WIDTH
    tau = RW_HEAD
    npair = width // LANES
    spec = pl.BlockSpec((batch, tau, width), lambda i: (0, i, 0))
    vec = pl.BlockSpec((1, width), lambda i: (0, 0))
    acts = [a.reshape(batch, seq, width) for a in (r, k, v, nkk, b, d, g)]
    return pl.pallas_call(
        _rwkv_rec_kernel,
        grid=(seq // tau,),
        in_specs=[spec] * 7 + [vec] * 3,
        out_specs=spec,
        out_shape=jax.ShapeDtypeStruct((batch, seq, width), BF16),
        scratch_shapes=[pltpu.VMEM((batch, npair, RW_HEAD, LANES), F32),
                        pltpu.VMEM((batch * npair * RW_HEAD, LANES), BF16),
                        pltpu.VMEM((batch, npair, RW_HEAD, LANES), F32)],
        compiler_params=_cparams("arbitrary"),
        name="rwkv_recurrence",
    )(*acts, r_k.reshape(1, width), gn_g.reshape(1, width), gn_b.reshape(1, width))


def _merge_kernel(a_ref, l_ref, r_ref, wa_ref, wl_ref, wr_ref, ga_ref, gl_ref, gr_ref, o_ref):
    acc = jax.nn.sigmoid(ga_ref[...]) * jnp.dot(a_ref[...], wa_ref[...], preferred_element_type=F32)
    acc += jax.nn.sigmoid(gl_ref[...]) * jnp.dot(l_ref[...], wl_ref[...], preferred_element_type=F32)
    acc += jax.nn.sigmoid(gr_ref[...]) * jnp.dot(r_ref[...], wr_ref[...], preferred_element_type=F32)
    o_ref[...] = acc.astype(o_ref.dtype)


def merge_branches(y_a, y_l, y_r, w_a, w_l, w_r, proj, *, tm, tn):
    m = y_a.shape[0]
    n = w_a.shape[1]
    nj = n // tn
    act = lambda width: pl.BlockSpec((tm, width), lambda i, j: (i, 0))
    wgt = lambda width: pl.BlockSpec((width, tn), lambda i, j: (0, j))
    gate = lambda g: pl.BlockSpec((tm, tn), lambda i, j: (i, g * nj + j))
    return pl.pallas_call(
        _merge_kernel,
        grid=(m // tm, nj),
        in_specs=[act(y_a.shape[1]), act(y_l.shape[1]), act(y_r.shape[1]),
                  wgt(w_a.shape[0]), wgt(w_l.shape[0]), wgt(w_r.shape[0]),
                  gate(0), gate(1), gate(2)],
        out_specs=pl.BlockSpec((tm, tn), lambda i, j: (i, j)),
        out_shape=jax.ShapeDtypeStruct((m, n), BF16),
        compiler_params=_cparams("parallel", "parallel"),
        name="merge_branches",
    )(y_a, y_l, y_r, w_a, w_l, w_r, proj, proj, proj)


def _ffn_up_kernel(x_ref, wg_ref, wv_ref, cw_ref, cb_ref, o_ref, wgb_ref, wvb_ref, halo_ref,
                   *, tiles_per_seq, valid_tiles):
    j = pl.program_id(0)
    i = pl.program_id(1)

    @pl.when(i == 0)
    def _():
        wgb_ref[...] = wg_ref[...].astype(BF16)
        wvb_ref[...] = wv_ref[...].astype(BF16)

    @pl.when(i % tiles_per_seq == 0)
    def _():
        halo_ref[...] = jnp.zeros_like(halo_ref)

    @pl.when(j < valid_tiles)
    def _():
        x = x_ref[...]
        g = jnp.dot(x, wgb_ref[...], preferred_element_type=F32)
        tm = g.shape[0]
        tail = halo_ref[...]
        c = g * cw_ref[FFN_CONV - 1:FFN_CONV, :] + cb_ref[...]
        for sh in range(1, FFN_CONV):
            c = c + _shift_rows(g, sh, tail) * cw_ref[FFN_CONV - 1 - sh:FFN_CONV - sh, :]
        halo_ref[...] = g[tm - SUBLANES:tm, :]
        v = jnp.dot(x, wvb_ref[...], preferred_element_type=F32)
        o_ref[...] = (_gelu_tanh(c) * v).astype(o_ref.dtype)

    @pl.when(j >= valid_tiles)
    def _():
        o_ref[...] = jnp.zeros_like(o_ref)


def ffn_up(x, w, conv_w, conv_b, layer, *, seq, tm, tn, dff_pad):
    m, k = x.shape
    dff = conv_w.shape[-1]
    valid = dff // tn
    last = valid - 1
    return pl.pallas_call(
        functools.partial(_ffn_up_kernel, tiles_per_seq=seq // tm, valid_tiles=valid),
        grid=(dff_pad // tn, m // tm),
        in_specs=[pl.BlockSpec((tm, k), lambda j, i: (i, 0)),
                  pl.BlockSpec((None, k, tn), lambda j, i: (layer, 0, jnp.minimum(j, last))),
                  pl.BlockSpec((None, k, tn), lambda j, i: (layer, 0, valid + jnp.minimum(j, last))),
                  pl.BlockSpec((None, FFN_CONV, tn), lambda j, i: (layer, 0, jnp.minimum(j, last))),
                  pl.BlockSpec((None, 1, tn), lambda j, i: (layer, 0, jnp.minimum(j, last)))],
        out_specs=pl.BlockSpec((tm, tn), lambda j, i: (i, j)),
        out_shape=jax.ShapeDtypeStruct((m, dff_pad), BF16),
        scratch_shapes=[pltpu.VMEM((k, tn), BF16), pltpu.VMEM((k, tn), BF16),
                        pltpu.VMEM((SUBLANES, tn), F32)],
        compiler_params=_cparams("arbitrary", "arbitrary"),
        name="ffn_up",
    )(x, w, w, conv_w, conv_b)


def _rotate_half_cols(w):
    half = MLA_ROPE // 2
    return jnp.concatenate([-w[..., half:], w[..., :half]], axis=-1)


SRC_CKV = MLA_Q_RANK
SRC_KR = SRC_CKV + MLA_KV_RANK
SRC_LRU = SRC_KR + MLA_ROPE
SRC_RW = SRC_LRU + 2 * LRU_WIDTH
SRC_XWA = SRC_RW + 3 * RW_WIDTH
SRC_XG = SRC_XWA + RW_DECAY_LORA + RW_AAA_LORA
SRC_GATES = SRC_XG + RW_GATE_LORA


W_IN_TILE = 256
_T_LRU = (3 * D_MODEL) // W_IN_TILE
_T_RW = _T_LRU + (2 * LRU_WIDTH) // W_IN_TILE
_T_CQ = _T_RW + (3 * RW_WIDTH) // W_IN_TILE
_T_CKV = _T_CQ + Q_PAD // W_IN_TILE
_T_MIX = _T_CKV + MLA_KV_RANK // W_IN_TILE
_T_XG = _T_MIX + 1


def _w_in_src_row(t):
    tile = W_IN_TILE
    return jnp.where(
        t < _T_LRU, SRC_GATES + tile * t, jnp.where(
            t < _T_RW, SRC_LRU + tile * (t - _T_LRU), jnp.where(
                t < _T_CQ, SRC_RW + tile * (t - _T_RW), jnp.where(
                    t < _T_CKV, tile * (t - _T_CQ), jnp.where(
                        t < _T_MIX, SRC_CKV + tile * (t - _T_CKV), jnp.where(
                            t == _T_MIX, SRC_KR, SRC_XG))))))


def _w_in_kernel(a_ref, b_ref, o_ref):
    t = pl.program_id(0)
    a = a_ref[...]

    @pl.when(t != _T_MIX)
    def _():
        valid = jnp.where(t == _T_CKV - 1, MLA_Q_RANK - (Q_PAD - W_IN_TILE),
                          jnp.where(t == _T_XG, RW_GATE_LORA, W_IN_TILE))
        row = lax.broadcasted_iota(jnp.int32, a.shape, 0)
        o_ref[...] = jnp.where(row < valid, a, 0.0).astype(BF16)

    @pl.when(t == _T_MIX)
    def _():
        half = MLA_ROPE // 2
        lora = RW_DECAY_LORA + RW_AAA_LORA
        o_ref[...] = jnp.concatenate(
            [a[0:MLA_ROPE], -a[half:MLA_ROPE], a[0:half], b_ref[0:lora]], axis=0).astype(BF16)


def relayout_w_in(w_t, layer, *, k):
    window = lambda index: pl.BlockSpec((None, pl.Element(W_IN_TILE), pl.Element(k)), index)
    return pl.pallas_call(
        _w_in_kernel,
        grid=(IN_COLS_PAD // W_IN_TILE,),
        in_specs=[window(lambda t: (layer, pl.multiple_of(_w_in_src_row(t), SUBLANES), 0)),
                  window(lambda t: (layer, SRC_XWA, 0))],
        out_specs=pl.BlockSpec((W_IN_TILE, k), lambda t: (t, 0)),
        out_shape=jax.ShapeDtypeStruct((IN_COLS_PAD, k), BF16),
        compiler_params=_cparams("arbitrary"),
        name="w_in_relayout",
    )(w_t, w_t)


def _cast_kernel(s_ref, o_ref, *, valid_blocks):
    i = pl.program_id(0)

    @pl.when(i < valid_blocks)
    def _():
        o_ref[...] = s_ref[...].astype(o_ref.dtype)

    @pl.when(i >= valid_blocks)
    def _():
        o_ref[...] = jnp.zeros_like(o_ref)


def cast_bf16(w, layer=None, *, tr, rows_out=None):
    r, c = w.shape[-2:]
    rows_out = r if rows_out is None else rows_out
    valid = r // tr
    if layer is None:
        src = pl.BlockSpec((tr, c), lambda i: (jnp.minimum(i, valid - 1), 0))
    else:
        src = pl.BlockSpec((None, tr, c), lambda i: (layer, jnp.minimum(i, valid - 1), 0))
    return pl.pallas_call(
        functools.partial(_cast_kernel, valid_blocks=valid),
        grid=(rows_out // tr,),
        in_specs=[src],
        out_specs=pl.BlockSpec((tr, c), lambda i: (i, 0)),
        out_shape=jax.ShapeDtypeStruct((rows_out, c), BF16),
        compiler_params=_cparams("parallel"),
        name="cast_bf16",
    )(w)


def _prep_w_uq(w):
    rope = w[..., MLA_NOPE:]
    w = jnp.concatenate([w, _rotate_half_cols(rope)], axis=-1)
    w = w.reshape(MLA_Q_RANK, MLA_HEADS * 2 * LANES)
    return jnp.pad(w, ((0, Q_PAD - MLA_Q_RANK), (0, 0))).astype(BF16)


def _rope_table(positions):
    inv_freq = ROPE_THETA ** (-jnp.arange(0, MLA_ROPE, 2, dtype=F32) / MLA_ROPE)
    ang = positions.astype(F32)[..., None] * inv_freq
    cos, sin = jnp.cos(ang), jnp.sin(ang)
    return jnp.concatenate([cos, cos, sin, sin], axis=-1).reshape(-1, 2 * MLA_ROPE)


class _Tiles:
    in_proj = (1024, 1024)
    mla_up_rows = 512
    attn_q = 512
    attn_heads = 4
    lru_rows = 512
    rwkv_prep_rows = 256
    merge = (512, 1024)
    out_mm = (1024, 1024, 1024)
    ln_rows = 256
    ffn_up = (1024, 256)
    cast_rows = 256


def kernel(x, positions, w_in, mla_q_norm, mla_w_uq, mla_kv_norm, mla_w_ukv, lru_conv_w, lru_conv_b, lru_w_a, lru_b_a, lru_w_x, lru_b_x, lru_lambda, rw_mu, rw_w0, rw_w2, rw_a0, rw_a2, rw_g2, rw_k_k, rw_k_a, rw_r_k, rw_gn_g, rw_gn_b, w_o_mla, w_o_lru, w_o_rwkv, w_out, ln1_g, ln1_b, ffn_w_up, ffn_conv_w, ffn_conv_b, ffn_w_down, ln2_g, ln2_b):
    batch, seq, d = x.shape
    m = batch * seq
    t = _Tiles
    cs = _rope_table(positions)
    xf = x.reshape(m, d)
    cast = functools.partial(cast_bf16, tr=t.cast_rows)
    xb = cast(xf)
    w_ukv_all = mla_w_ukv.reshape(DEPTH, MLA_KV_RANK, -1)
    conv_b_all = ffn_conv_b.reshape(DEPTH, 1, -1)
    w_in_t = jnp.transpose(w_in, (0, 2, 1))
    for l in range(DEPTH):
        proj = matmul_nt(xb, relayout_w_in(w_in_t, l, k=d), tm=t.in_proj[0], tn=t.in_proj[1])

        q_gain = jnp.pad(mla_q_norm[l], (0, Q_PAD - MLA_Q_RANK)).reshape(1, Q_PAD)
        q = mla_q_up(proj, q_gain, _prep_w_uq(mla_w_uq[l]), cs, batch=batch, seq=seq, tm=t.mla_up_rows)
        k, v = mla_kv_up(proj, mla_kv_norm[l].reshape(1, -1), cast(w_ukv_all, l), cs, batch=batch,
                         seq=seq, tm=t.mla_up_rows)
        y_a = mla_attention(q, k, v, tq=t.attn_q, hps=t.attn_heads).reshape(m, -1)

        y_l = rglru_branch(proj, lru_conv_w[l], lru_conv_b[l], lru_w_a[l], lru_b_a[l],
                           lru_w_x[l], lru_b_x[l], lru_lambda[l], batch=batch, seq=seq, ts=t.lru_rows)

        rw = rwkv_prep(proj, rw_mu[l], rw_w0[l], rw_w2[l], rw_a0[l], rw_a2[l], rw_g2[l],
                       rw_k_k[l], rw_k_a[l], batch=batch, seq=seq, ts=t.rwkv_prep_rows)
        y_r = rwkv_recurrence(*rw, rw_r_k[l], rw_gn_g[l], rw_gn_b[l], batch=batch, seq=seq)
        y_r = y_r.reshape(m, RW_WIDTH)

        merged = merge_branches(y_a, y_l, y_r, cast(w_o_mla, l), cast(w_o_lru, l), cast(w_o_rwkv, l),
                                proj, tm=t.merge[0], tn=t.merge[1])
        mm = dict(tm=t.out_mm[0], tn=t.out_mm[1], tk=t.out_mm[2], ln_rows=t.ln_rows)
        xf, xb = matmul_residual_layernorm(merged, cast(w_out, l), xf, ln1_g[l], ln1_b[l],
                                           name="mixer_out", **mm)

        h = ffn_up(xb, ffn_w_up, ffn_conv_w, conv_b_all, l, seq=seq, tm=t.ffn_up[0],
                   tn=t.ffn_up[1], dff_pad=D_FF_PAD)
        xf, xb = matmul_residual_layernorm(h, cast(ffn_w_down, l, rows_out=D_FF_PAD), xf, ln2_g[l],
                                           ln2_b[l], name="ffn_down", **mm)
    return xf.reshape(batch, seq, d)
```

```python
import functools

import jax
import jax.numpy as jnp
from jax import lax
from jax.experimental import pallas as pl
from jax.experimental.pallas import tpu as pltpu

F32 = jnp.float32
BF16 = jnp.bfloat16

D_MODEL = 4096
DEPTH = 2
CHUNK = 64
MLA_HEADS = 16
MLA_Q_RANK = 896
MLA_KV_RANK = 512
MLA_NOPE = 128
MLA_ROPE = 64
MLA_V = 128
ROPE_THETA = 10000.0
LRU_WIDTH = 1024
LRU_BLOCKS = 8
LRU_BLOCK = LRU_WIDTH // LRU_BLOCKS
LRU_CONV = 4
LRU_C = 8.0
RW_WIDTH = 1024
RW_HEAD = 64
RW_HEADS = RW_WIDTH // RW_HEAD
RW_DECAY_LORA = 64
RW_AAA_LORA = 64
RW_GATE_LORA = 160
D_FF = 11008
FFN_CONV = 3
ALPHA = (2 * DEPTH) ** 0.25
LN_EPS = 1e-5
RMS_EPS = 1e-6
GN_EPS = 64e-5

LANES = 128
SUBLANES = 8
VMEM_LIMIT_BYTES = 56 * 1024 * 1024

Q_PAD = 1024
D_FF_PAD = 11264
XG_PAD = 256
OFF_GATES = 0
OFF_LRU_X = 3 * D_MODEL
OFF_LRU_G = OFF_LRU_X + LRU_WIDTH
OFF_RW_R = OFF_LRU_G + LRU_WIDTH
OFF_RW_K = OFF_RW_R + RW_WIDTH
OFF_RW_V = OFF_RW_K + RW_WIDTH
OFF_CQ = OFF_RW_V + RW_WIDTH
OFF_CKV = OFF_CQ + Q_PAD
OFF_KR = OFF_CKV + MLA_KV_RANK
OFF_XWA = OFF_KR + 2 * MLA_ROPE
OFF_XG = OFF_XWA + RW_DECAY_LORA + RW_AAA_LORA
IN_COLS_PAD = OFF_XG + XG_PAD

RW_GROUP_PAIRS = 4
_SQRT_2_OVER_PI = 0.7978845608028654
_LOG2_E = 1.4426950408889634


def _cparams(*sem):
    return pltpu.CompilerParams(dimension_semantics=sem, vmem_limit_bytes=VMEM_LIMIT_BYTES)


def _gelu_tanh(x):
    return 0.5 * x * (1.0 + jnp.tanh(_SQRT_2_OVER_PI * (x + 0.044715 * (x * x * x))))


def _softplus(z):
    return jnp.maximum(z, 0.0) + jnp.log1p(jnp.exp(-jnp.abs(z)))


def _shift_rows(x, shift, tail):
    row = lax.broadcasted_iota(jnp.int32, x.shape, 0)
    xs = pltpu.roll(x, shift, axis=0)
    for j in range(shift):
        src = SUBLANES - shift + j
        xs = jnp.where(row == j, tail[src:src + 1, :], xs)
    return xs


def _group_ones(rows, cols, group):
    shift = group.bit_length() - 1
    r = lax.broadcasted_iota(jnp.int32, (rows, cols), 0)
    c = lax.broadcasted_iota(jnp.int32, (rows, cols), 1)
    return jnp.where(((r & (cols - 1)) >> shift) == (c >> shift), 1.0, 0.0).astype(BF16)


def _head_sum(x, ones_hl):
    hi = x.astype(BF16)
    lo = (x - hi.astype(F32)).astype(BF16)
    return jnp.dot(jnp.concatenate([hi, lo], axis=1), ones_hl, preferred_element_type=F32)


class _SideCast:
    def __init__(self, w, layer, *, rows, rows_out=None):
        r, self.cols = w.shape[-2:]
        self.w, self.layer, self.rows = w, layer, rows
        self.valid = r // rows
        self.nblocks = (r if rows_out is None else rows_out) // rows

    def in_spec(self, step_of):
        last = self.valid - 1
        return pl.BlockSpec((None, self.rows, self.cols),
                            lambda *g: (self.layer, jnp.minimum(step_of(*g), last), 0))

    def out_spec(self, step_of):
        last = self.nblocks - 1
        return pl.BlockSpec((self.rows, self.cols), lambda *g: (jnp.minimum(step_of(*g), last), 0))

    def out_shape(self):
        return jax.ShapeDtypeStruct((self.nblocks * self.rows, self.cols), BF16)

    def emit(self, step, src_ref, dst_ref):
        if self.valid == self.nblocks:
            dst_ref[...] = src_ref[...].astype(BF16)
            return
        blk = jnp.minimum(step, self.nblocks - 1)

        @pl.when(blk < self.valid)
        def _():
            dst_ref[...] = src_ref[...].astype(BF16)

        @pl.when(blk >= self.valid)
        def _():
            dst_ref[...] = jnp.zeros_like(dst_ref)


def _mm_kernel(a_ref, b_ref, s_ref, o_ref, so_ref, *, side, nj):
    side.emit(pl.program_id(0) * nj + pl.program_id(1), s_ref, so_ref)
    o_ref[...] = lax.dot_general(a_ref[...], b_ref[...], (((1,), (1,)), ((), ())),
                                 preferred_element_type=F32).astype(o_ref.dtype)


def matmul_nt(a, b, side, *, tm, tn, out_dtype=F32):
    m, k = a.shape
    n, _ = b.shape
    nj = n // tn
    assert (m // tm) * nj >= side.nblocks
    step_of = lambda i, j: i * nj + j
    return pl.pallas_call(
        functools.partial(_mm_kernel, side=side, nj=nj),
        grid=(m // tm, nj),
        in_specs=[pl.BlockSpec((tm, k), lambda i, j: (i, 0)),
                  pl.BlockSpec((tn, k), lambda i, j: (j, 0)),
                  side.in_spec(step_of)],
        out_specs=[pl.BlockSpec((tm, tn), lambda i, j: (i, j)), side.out_spec(step_of)],
        out_shape=[jax.ShapeDtypeStruct((m, n), out_dtype), side.out_shape()],
        compiler_params=_cparams("arbitrary", "arbitrary"),
        name="in_proj",
    )(a, b, side.w)


def _mm_res_kernel(a_ref, w_ref, x_ref, o_ref, acc_ref):
    k = pl.program_id(2)
    last = pl.num_programs(2) - 1

    @pl.when(k == 0)
    def _():
        acc_ref[...] = ALPHA * x_ref[...]

    @pl.when(k < last)
    def _():
        acc_ref[...] += jnp.dot(a_ref[...], w_ref[...], preferred_element_type=F32)

    @pl.when(k == last)
    def _():
        o_ref[...] = acc_ref[...] + jnp.dot(a_ref[...], w_ref[...], preferred_element_type=F32)


def _ln_kernel(y_ref, g_ref, b_ref, o_ref, ob_ref):
    y = y_ref[...]
    mu = jnp.mean(y, axis=-1, keepdims=True)
    yc = y - mu
    var = jnp.mean(yc * yc, axis=-1, keepdims=True)
    o = yc * lax.rsqrt(var + LN_EPS) * g_ref[...] + b_ref[...]
    o_ref[...] = o
    ob_ref[...] = o.astype(BF16)


def matmul_residual_layernorm(a, w, x, g, b, *, tm, tn, tk, ln_rows, name):
    m, k = a.shape
    n = w.shape[1]
    y = pl.pallas_call(
        _mm_res_kernel,
        grid=(m // tm, n // tn, k // tk),
        in_specs=[pl.BlockSpec((tm, tk), lambda i, j, kk: (i, kk)),
                  pl.BlockSpec((tk, tn), lambda i, j, kk: (kk, j)),
                  pl.BlockSpec((tm, tn), lambda i, j, kk: (i, j))],
        out_specs=pl.BlockSpec((tm, tn), lambda i, j, kk: (i, j)),
        out_shape=jax.ShapeDtypeStruct((m, n), F32),
        scratch_shapes=[pltpu.VMEM((tm, tn), F32)],
        compiler_params=_cparams("parallel", "parallel", "arbitrary"),
        name=name,
    )(a, w, x)
    return pl.pallas_call(
        _ln_kernel,
        grid=(m // ln_rows,),
        in_specs=[pl.BlockSpec((ln_rows, n), lambda i: (i, 0)),
                  pl.BlockSpec((1, n), lambda i: (0, 0)),
                  pl.BlockSpec((1, n), lambda i: (0, 0))],
        out_specs=[pl.BlockSpec((ln_rows, n), lambda i: (i, 0)),
                   pl.BlockSpec((ln_rows, n), lambda i: (i, 0))],
        out_shape=[jax.ShapeDtypeStruct((m, n), F32), jax.ShapeDtypeStruct((m, n), BF16)],
        compiler_params=_cparams("parallel"),
        name="layernorm",
    )(y, g.reshape(1, n), b.reshape(1, n))


def _rope_half(block, cs):
    p = block * cs
    return p + pltpu.roll(p, MLA_ROPE, axis=1)


def _q_up_kernel(p_ref, g_ref, w_ref, cs_ref, q_ref, *, rank):
    x = p_ref[...]
    ms = jnp.sum(x * x, axis=-1, keepdims=True) * (1.0 / rank)
    xn = (x * lax.rsqrt(ms + RMS_EPS) * g_ref[...]).astype(BF16)
    cs = cs_ref[...]
    hw = 2 * LANES
    for h in range(q_ref.shape[1]):
        acc = jnp.dot(xn, w_ref[:, h * hw:(h + 1) * hw], preferred_element_type=F32)
        rot = _rope_half(acc[:, LANES:hw], cs)
        q_ref[0, h, :, 0:MLA_NOPE] = acc[:, 0:MLA_NOPE].astype(BF16)
        q_ref[0, h, :, MLA_NOPE:MLA_NOPE + MLA_ROPE] = rot[:, 0:MLA_ROPE].astype(BF16)


def mla_q_up(proj, gain, w, cs, *, batch, seq, tm):
    nh = MLA_HEADS
    spt = seq // tm
    return pl.pallas_call(
        functools.partial(_q_up_kernel, rank=MLA_Q_RANK),
        grid=(batch * spt,),
        in_specs=[pl.BlockSpec((tm, Q_PAD), lambda i: (i, OFF_CQ // Q_PAD)),
                  pl.BlockSpec((1, Q_PAD), lambda i: (0, 0)),
                  pl.BlockSpec(w.shape, lambda i: (0, 0)),
                  pl.BlockSpec((tm, LANES), lambda i: (i, 0))],
        out_specs=pl.BlockSpec((1, nh, tm, MLA_NOPE + MLA_ROPE), lambda i: (i // spt, 0, i % spt, 0)),
        out_shape=jax.ShapeDtypeStruct((batch, nh, seq, MLA_NOPE + MLA_ROPE), BF16),
        compiler_params=_cparams("parallel"),
        name="mla_q_up",
    )(proj, gain, w, cs)


def _kv_up_kernel(p_ref, kr_ref, g_ref, w_ref, cs_ref, k_ref, v_ref, *, rank):
    x = p_ref[...]
    ms = jnp.sum(x * x, axis=-1, keepdims=True) * (1.0 / rank)
    xn = (x * lax.rsqrt(ms + RMS_EPS) * g_ref[...]).astype(BF16)
    krope = _rope_half(kr_ref[...], cs_ref[...])[:, 0:MLA_ROPE].astype(BF16)
    hw = MLA_NOPE + MLA_V
    for h in range(k_ref.shape[1]):
        acc = jnp.dot(xn, w_ref[:, h * hw:(h + 1) * hw], preferred_element_type=F32)
        k_ref[0, h, :, 0:MLA_NOPE] = acc[:, 0:MLA_NOPE].astype(BF16)
        k_ref[0, h, :, MLA_NOPE:MLA_NOPE + MLA_ROPE] = krope
        v_ref[0, h] = acc[:, MLA_NOPE:hw].astype(BF16)


def mla_kv_up(proj, gain, w, cs, *, batch, seq, tm):
    nh = MLA_HEADS
    spt = seq // tm
    dk = MLA_NOPE + MLA_ROPE
    return pl.pallas_call(
        functools.partial(_kv_up_kernel, rank=MLA_KV_RANK),
        grid=(batch * spt,),
        in_specs=[pl.BlockSpec((tm, MLA_KV_RANK), lambda i: (i, OFF_CKV // MLA_KV_RANK)),
                  pl.BlockSpec((tm, LANES), lambda i: (i, OFF_KR // LANES)),
                  pl.BlockSpec((1, MLA_KV_RANK), lambda i: (0, 0)),
                  pl.BlockSpec(w.shape, lambda i: (0, 0)),
                  pl.BlockSpec((tm, LANES), lambda i: (i, 0))],
        out_specs=[pl.BlockSpec((1, nh, tm, dk), lambda i: (i // spt, 0, i % spt, 0)),
                   pl.BlockSpec((1, nh, tm, MLA_V), lambda i: (i // spt, 0, i % spt, 0))],
        out_shape=[jax.ShapeDtypeStruct((batch, nh, seq, dk), BF16),
                   jax.ShapeDtypeStruct((batch, nh, seq, MLA_V), BF16)],
        compiler_params=_cparams("parallel"),
        name="mla_kv_up",
    )(proj, proj, gain, w, cs)


def _flash_kernel(q_ref, k_ref, v_ref, o_ref, *, tq, scale):
    qi = pl.program_id(2)
    heads = range(q_ref.shape[1])
    qs = [q_ref[0, h] for h in heads]

    def scores(h, j):
        k = k_ref[0, h, pl.ds(pl.multiple_of(j * tq, tq), tq), :]
        return lax.dot_general(qs[h], k, (((1,), (1,)), ((), ())),
                               preferred_element_type=F32) * (scale * _LOG2_E)

    def update(h, j, s, m, l, acc):
        v = v_ref[0, h, pl.ds(pl.multiple_of(j * tq, tq), tq), :]
        m_new = jnp.maximum(m, jnp.max(s, axis=-1, keepdims=True))
        alpha = jnp.exp2(m - m_new)
        p = jnp.exp2(s - m_new)
        l = alpha * l + jnp.sum(p, axis=-1, keepdims=True)
        acc = alpha * acc + jnp.dot(p.astype(BF16), v, preferred_element_type=F32)
        return m_new, l, acc

    def pair(jj, carry):
        out = []
        for h in heads:
            s_a, s_b = scores(h, 2 * jj), scores(h, 2 * jj + 1)
            out.append(update(h, 2 * jj + 1, s_b, *update(h, 2 * jj, s_a, *carry[h])))
        return tuple(out)

    def single(_, carry):
        return tuple(update(h, qi - 1, scores(h, qi - 1), *carry[h]) for h in heads)

    carry = tuple((jnp.full((tq, 1), -1e30, F32), jnp.zeros((tq, 1), F32),
                   jnp.zeros((tq, MLA_V), F32)) for _ in heads)
    carry = lax.fori_loop(0, qi >> 1, pair, carry)
    carry = lax.fori_loop(0, qi & 1, single, carry)
    shift = CHUNK.bit_length() - 1
    qc = lax.broadcasted_iota(jnp.int32, (tq, tq), 0) >> shift
    kc = lax.broadcasted_iota(jnp.int32, (tq, tq), 1) >> shift
    for h in heads:
        m, l, acc = update(h, qi, jnp.where(kc <= qc, scores(h, qi), -jnp.inf), *carry[h])
        o_ref[0, :, h * MLA_V:(h + 1) * MLA_V] = (acc / l).astype(o_ref.dtype)


def mla_attention(q, k, v, *, tq, hps):
    batch, nh, seq, dk = q.shape
    scale = (MLA_NOPE + MLA_ROPE) ** -0.5
    return pl.pallas_call(
        functools.partial(_flash_kernel, tq=tq, scale=scale),
        grid=(batch, nh // hps, seq // tq),
        in_specs=[pl.BlockSpec((1, hps, tq, dk), lambda b, h, i: (b, h, i, 0)),
                  pl.BlockSpec((1, hps, seq, dk), lambda b, h, i: (b, h, 0, 0)),
                  pl.BlockSpec((1, hps, seq, MLA_V), lambda b, h, i: (b, h, 0, 0))],
        out_specs=pl.BlockSpec((1, tq, hps * MLA_V), lambda b, h, i: (b, i, h)),
        out_shape=jax.ShapeDtypeStruct((batch, seq, nh * MLA_V), BF16),
        compiler_params=_cparams("parallel", "parallel", "arbitrary"),
        name="mla_attention",
    )(q, k, v)


def _lru_kernel(x_ref, g_ref, cw_ref, cb_ref, wa_ref, ba_ref, wx_ref, bx_ref, lam_ref, o_ref,
                tail_ref, h_ref, a_scr, u_scr):
    @pl.when(pl.program_id(1) == 0)
    def _():
        tail_ref[...] = jnp.zeros_like(tail_ref)
        h_ref[...] = jnp.zeros_like(h_ref)

    x = x_ref[...]
    ts = x.shape[0]
    tail = tail_ref[...]
    xc = x * cw_ref[LRU_CONV - 1:LRU_CONV, :] + cb_ref[...]
    for sh in range(1, LRU_CONV):
        xc = xc + _shift_rows(x, sh, tail) * cw_ref[LRU_CONV - 1 - sh:LRU_CONV - sh, :]
    tail_ref[...] = x[ts - SUBLANES:ts, :]

    xb = xc.astype(BF16)
    ra, rx = [], []
    for n in range(LRU_BLOCKS):
        blk = xb[:, n * LRU_BLOCK:(n + 1) * LRU_BLOCK]
        ra.append(jnp.dot(blk, wa_ref[n], preferred_element_type=F32))
        rx.append(jnp.dot(blk, wx_ref[n], preferred_element_type=F32))
    r = jax.nn.sigmoid(jnp.concatenate(ra, axis=1) + ba_ref[...])
    gi = jax.nn.sigmoid(jnp.concatenate(rx, axis=1) + bx_ref[...])
    log_a = -LRU_C * r * _softplus(-lam_ref[...])
    a_scr[...] = jnp.exp(log_a)
    u_scr[...] = jnp.sqrt(1.0 - jnp.exp(2.0 * log_a)) * (gi * xc)

    row8 = lax.broadcasted_iota(jnp.int32, (SUBLANES, x.shape[1]), 0)

    def body(grp, h):
        base = pl.multiple_of(grp * SUBLANES, SUBLANES)
        a8 = a_scr[pl.ds(base, SUBLANES), :]
        u8 = u_scr[pl.ds(base, SUBLANES), :]
        hs = u8
        for s in range(SUBLANES):
            h = a8[s:s + 1, :] * h + u8[s:s + 1, :]
            hs = jnp.where(row8 == s, h, hs)
        u_scr[pl.ds(base, SUBLANES), :] = hs
        return h

    h_ref[...] = lax.fori_loop(0, ts // SUBLANES, body, h_ref[...])
    o_ref[...] = (u_scr[...] * _gelu_tanh(g_ref[...])).astype(o_ref.dtype)


def rglru_branch(proj, conv_w, conv_b, w_a, b_a, w_x, b_x, lam, *, batch, seq, ts):
    w = LRU_WIDTH
    spt = seq // ts
    vec = lambda: pl.BlockSpec((1, w), lambda b, s: (0, 0))
    blockdiag = lambda: pl.BlockSpec((LRU_BLOCKS, LRU_BLOCK, LRU_BLOCK), lambda b, s: (0, 0, 0))
    return pl.pallas_call(
        _lru_kernel,
        grid=(batch, spt),
        in_specs=[pl.BlockSpec((ts, w), lambda b, s: (b * spt + s, OFF_LRU_X // w)),
                  pl.BlockSpec((ts, w), lambda b, s: (b * spt + s, OFF_LRU_G // w)),
                  pl.BlockSpec((LRU_CONV, w), lambda b, s: (0, 0)),
                  vec(), blockdiag(), vec(), blockdiag(), vec(), vec()],
        out_specs=pl.BlockSpec((ts, w), lambda b, s: (b * spt + s, 0)),
        out_shape=jax.ShapeDtypeStruct((batch * seq, w), BF16),
        scratch_shapes=[pltpu.VMEM((SUBLANES, w), F32), pltpu.VMEM((1, w), F32),
                        pltpu.VMEM((ts, w), F32), pltpu.VMEM((ts, w), F32)],
        compiler_params=_cparams("parallel", "arbitrary"),
        name="rglru",
    )(proj, proj, conv_w, conv_b.reshape(1, w), w_a.astype(BF16), b_a.reshape(1, w),
      w_x.astype(BF16), b_x.reshape(1, w), lam.reshape(1, w))


def _rwkv_prep_kernel(r_ref, k_ref, v_ref, wa_ref, xg_ref,
                      mur_ref, muk_ref, muv_ref, muwa_ref, mug_ref,
                      w0_ref, w2_ref, a0_ref, a2_ref, g2_ref, kkw_ref, ka_ref,
                      ro_ref, ko_ref, vo_ref, nkk_ref, b_ref, d_ref, g_ref,
                      tr_ref, tk_ref, tv_ref, twa_ref, tg_ref):
    @pl.when(pl.program_id(1) == 0)
    def _():
        for t in (tr_ref, tk_ref, tv_ref, twa_ref, tg_ref):
            t[...] = jnp.zeros_like(t)

    def mix(x_ref, mu_ref, tail_ref):
        x = x_ref[...]
        prev = _shift_rows(x, 1, tail_ref[...])
        tail_ref[...] = x[x.shape[0] - SUBLANES:, :]
        return x + mu_ref[...] * (prev - x)

    r = mix(r_ref, mur_ref, tr_ref)
    k = mix(k_ref, muk_ref, tk_ref)
    v = mix(v_ref, muv_ref, tv_ref)
    xwa = mix(wa_ref, muwa_ref, twa_ref)
    xg = mix(xg_ref, mug_ref, tg_ref)
    w = -_softplus(-(w0_ref[...] + jnp.dot(jnp.tanh(xwa).astype(BF16), w2_ref[...],
                                           preferred_element_type=F32))) - 0.5
    a = jax.nn.sigmoid(a0_ref[...] + jnp.dot(xwa.astype(BF16), a2_ref[...], preferred_element_type=F32))
    g = jnp.dot(jax.nn.sigmoid(xg).astype(BF16), g2_ref[...], preferred_element_type=F32)

    kk = k * kkw_ref[...]
    ones_hl = _group_ones(2 * LANES, LANES, RW_HEAD)
    sq = kk * kk
    ss = jnp.concatenate([_head_sum(sq[:, c * LANES:(c + 1) * LANES], ones_hl)
                          for c in range(RW_WIDTH // LANES)], axis=1)
    kk = kk / jnp.maximum(jnp.sqrt(ss), 1e-12)

    ro_ref[...] = r
    ko_ref[...] = k * (1.0 + (a - 1.0) * ka_ref[...])
    vo_ref[...] = v
    nkk_ref[...] = -kk
    b_ref[...] = kk * a
    d_ref[...] = jnp.exp(-jnp.exp(w))
    g_ref[...] = g


def rwkv_prep(proj, mu, w0, w2, a0, a2, g2, k_k, k_a, *, batch, seq, ts):
    w = RW_WIDTH
    spt = seq // ts
    lora = RW_DECAY_LORA + RW_AAA_LORA
    row = lambda width, col: pl.BlockSpec((ts, width), lambda b, s: (b * spt + s, col))
    vec = lambda width: pl.BlockSpec((1, width), lambda b, s: (0, 0))
    mat = lambda rows: pl.BlockSpec((rows, w), lambda b, s: (0, 0))
    mu_r, mu_k, mu_v = (mu[i * w:(i + 1) * w].reshape(1, w) for i in range(3))
    mu_wa = mu[3 * w:3 * w + lora].reshape(1, lora)
    mu_g = jnp.pad(mu[3 * w + lora:], (0, XG_PAD - RW_GATE_LORA)).reshape(1, XG_PAD)
    zeros = jnp.zeros((RW_DECAY_LORA, w), w2.dtype)
    w2p = jnp.concatenate([w2, zeros], axis=0).astype(BF16)
    a2p = jnp.concatenate([zeros, a2], axis=0).astype(BF16)
    g2p = jnp.pad(g2, ((0, XG_PAD - RW_GATE_LORA), (0, 0))).astype(BF16)
    out = jax.ShapeDtypeStruct((batch * seq, w), F32)
    return pl.pallas_call(
        _rwkv_prep_kernel,
        grid=(batch, spt),
        in_specs=[row(w, OFF_RW_R // w), row(w, OFF_RW_K // w), row(w, OFF_RW_V // w),
                  row(lora, OFF_XWA // lora), row(XG_PAD, OFF_XG // XG_PAD),
                  vec(w), vec(w), vec(w), vec(lora), vec(XG_PAD),
                  vec(w), mat(lora), vec(w), mat(lora), mat(XG_PAD), vec(w), vec(w)],
        out_specs=[pl.BlockSpec((ts, w), lambda b, s: (b * spt + s, 0))] * 7,
        out_shape=[out] * 7,
        scratch_shapes=[pltpu.VMEM((SUBLANES, w), F32)] * 3
                       + [pltpu.VMEM((SUBLANES, lora), F32), pltpu.VMEM((SUBLANES, XG_PAD), F32)],
        compiler_params=_cparams("parallel", "arbitrary"),
        name="rwkv_prep",
    )(proj, proj, proj, proj, proj, mu_r, mu_k, mu_v, mu_wa, mu_g,
      w0.reshape(1, w), w2p, a0.reshape(1, w), a2p, g2p, k_k.reshape(1, w), k_a.reshape(1, w))


def _rwkv_rec_kernel(*refs, sides):
    ns = len(sides)
    r_ref, k_ref, v_ref, nkk_ref, b_ref, d_ref, g_ref, rk_ref, gng_ref, gnb_ref = refs[:10]
    side_src = refs[10:10 + ns]
    y_ref = refs[10 + ns]
    side_dst = refs[11 + ns:11 + 2 * ns]
    st_ref, q_ref, ot_ref = refs[11 + 2 * ns:]
    for side, src, dst in zip(sides, side_src, side_dst):
        side.emit(pl.program_id(0), src, dst)

    nb, tau, width = r_ref.shape
    npair = width // LANES
    hd = RW_HEAD
    pairs = [(bi, p) for bi in range(nb) for p in range(npair)]
    flat = [(i, bi, p) for i, (bi, p) in enumerate(pairs)]
    groups = [flat[j:j + RW_GROUP_PAIRS] for j in range(0, len(flat), RW_GROUP_PAIRS)]

    @pl.when(pl.program_id(0) == 0)
    def _():
        st_ref[...] = jnp.zeros_like(st_ref)

    q_ref[...] = jnp.zeros_like(q_ref)
    ot_ref[...] = jnp.zeros_like(ot_ref)

    row = lax.broadcasted_iota(jnp.int32, (hd, LANES), 0)
    lane = lax.broadcasted_iota(jnp.int32, (hd, LANES), 1)
    lane_in_head = lane & (hd - 1)
    diag = lane_in_head == row
    ones2 = _group_ones(2 * LANES, 2 * LANES, hd)

    def group(grp, carry):
        base = pl.multiple_of(grp * SUBLANES, SUBLANES)
        tile = lambda ref, bi, p: ref[bi, pl.ds(base, SUBLANES), pl.ds(p * LANES, LANES)]
        vcb = [None] * len(groups)
        for s in range(SUBLANES):
            sel = lane_in_head == (base + s - 1)
            for gi, grp_pairs in enumerate(groups):
                lhs = []
                for i, bi, p in grp_pairs:
                    pm = (st_ref[bi, p] * tile(nkk_ref, bi, p)[s:s + 1, :]).astype(BF16)
                    lhs.append(jnp.concatenate([pm, q_ref[pl.ds(i * hd, hd), :]], axis=1))
                if s % 2 == 0:
                    for i, bi, p in grp_pairs:
                        v8 = tile(v_ref, bi, p)
                        lhs.append(jnp.concatenate(
                            [jnp.where(diag, v8[s:s + 1, :], 0.0).astype(BF16),
                             jnp.where(diag, v8[s + 1:s + 2, :], 0.0).astype(BF16)], axis=1))
                out = jnp.dot(jnp.concatenate(lhs, axis=0), ones2, preferred_element_type=F32)
                if s % 2 == 0:
                    vcb[gi] = out[len(grp_pairs) * hd:, :]
                for j, (i, bi, p) in enumerate(grp_pairs):
                    rows = slice(j * hd, (j + 1) * hd)
                    ot_ref[bi, p] = jnp.where(sel, out[rows, LANES:], ot_ref[bi, p])
                    vc = vcb[gi][rows, 0:LANES] if s % 2 == 0 else vcb[gi][rows, LANES:]
                    new = (st_ref[bi, p] * tile(d_ref, bi, p)[s:s + 1, :]
                           + out[rows, 0:LANES] * tile(b_ref, bi, p)[s:s + 1, :]
                           + vc * tile(k_ref, bi, p)[s:s + 1, :])
                    st_ref[bi, p] = new
                    q_ref[pl.ds(i * hd, hd), :] = (new * tile(r_ref, bi, p)[s:s + 1, :]).astype(BF16)
        return carry

    lax.fori_loop(0, tau // SUBLANES, group, 0)

    o_last = jnp.dot(q_ref[...], ones2[0:LANES, 0:LANES], preferred_element_type=F32)
    sel = lane_in_head == (tau - 1)
    for i, (bi, p) in enumerate(pairs):
        ot_ref[bi, p] = jnp.where(sel, o_last[i * hd:(i + 1) * hd, :], ot_ref[bi, p])

    ones_hl = _group_ones(2 * LANES, LANES, hd)
    first_head = lane < hd
    inv_hd = 1.0 / hd
    for bi in range(nb):
        nat = []
        for p0 in range(0, npair, 2):
            tr = jnp.concatenate([ot_ref[bi, p0], ot_ref[bi, p0 + 1]], axis=0).T
            top, bot = tr[0:hd, :], tr[hd:2 * hd, :]
            nat.append(jnp.where(first_head, top, pltpu.roll(bot, hd, axis=1)))
            nat.append(jnp.where(first_head, pltpu.roll(top, hd, axis=1), bot))
        cols = [pl.ds(p * LANES, LANES) for p in range(npair)]
        o = jnp.concatenate(nat, axis=0)
        mean = _head_sum(o, ones_hl) * inv_hd
        oc = o - mean
        var = _head_sum(oc * oc, ones_hl) * inv_hd
        rkr = jnp.concatenate([r_ref[bi, :, c] * k_ref[bi, :, c] * rk_ref[:, c] for c in cols], axis=0)
        bonus = _head_sum(rkr, ones_hl)
        on = oc * lax.rsqrt(var + GN_EPS)
        for p, c in enumerate(cols):
            rows = slice(p * tau, (p + 1) * tau)
            y = on[rows] * gng_ref[:, c] + gnb_ref[:, c] + bonus[rows] * v_ref[bi, :, c]
            y_ref[bi, :, c] = (y * g_ref[bi, :, c]).astype(y_ref.dtype)


def rwkv_recurrence(r, k, v, nkk, b, d, g, r_k, gn_g, gn_b, sides, *, batch, seq):
    width = RW_WIDTH
    tau = RW_HEAD
    npair = width // LANES
    nsteps = seq // tau
    assert all(s.nblocks <= nsteps for s in sides)
    spec = pl.BlockSpec((batch, tau, width), lambda i: (0, i, 0))
    vec = pl.BlockSpec((1, width), lambda i: (0, 0))
    acts = [a.reshape(batch, seq, width) for a in (r, k, v, nkk, b, d, g)]
    step_of = lambda i: i
    return pl.pallas_call(
        functools.partial(_rwkv_rec_kernel, sides=sides),
        grid=(nsteps,),
        in_specs=[spec] * 7 + [vec] * 3 + [s.in_spec(step_of) for s in sides],
        out_specs=[spec] + [s.out_spec(step_of) for s in sides],
        out_shape=[jax.ShapeDtypeStruct((batch, seq, width), BF16)] + [s.out_shape() for s in sides],
        scratch_shapes=[pltpu.VMEM((batch, npair, RW_HEAD, LANES), F32),
                        pltpu.VMEM((batch * npair * RW_HEAD, LANES), BF16),
                        pltpu.VMEM((batch, npair, RW_HEAD, LANES), F32)],
        compiler_params=_cparams("arbitrary"),
        name="rwkv_recurrence",
    )(*acts, r_k.reshape(1, width), gn_g.reshape(1, width), gn_b.reshape(1, width),
      *[s.w for s in sides])


def _merge_kernel(a_ref, l_ref, r_ref, wa_ref, wl_ref, wr_ref, ga_ref, gl_ref, gr_ref, o_ref):
    acc = jax.nn.sigmoid(ga_ref[...]) * jnp.dot(a_ref[...], wa_ref[...], preferred_element_type=F32)
    acc += jax.nn.sigmoid(gl_ref[...]) * jnp.dot(l_ref[...], wl_ref[...], preferred_element_type=F32)
    acc += jax.nn.sigmoid(gr_ref[...]) * jnp.dot(r_ref[...], wr_ref[...], preferred_element_type=F32)
    o_ref[...] = acc.astype(o_ref.dtype)


def merge_branches(y_a, y_l, y_r, w_a, w_l, w_r, proj, *, tm, tn):
    m = y_a.shape[0]
    n = w_a.shape[1]
    nj = n // tn
    act = lambda width: pl.BlockSpec((tm, width), lambda i, j: (i, 0))
    wgt = lambda width: pl.BlockSpec((width, tn), lambda i, j: (0, j))
    gate = lambda g: pl.BlockSpec((tm, tn), lambda i, j: (i, g * nj + j))
    return pl.pallas_call(
        _merge_kernel,
        grid=(m // tm, nj),
        in_specs=[act(y_a.shape[1]), act(y_l.shape[1]), act(y_r.shape[1]),
                  wgt(w_a.shape[0]), wgt(w_l.shape[0]), wgt(w_r.shape[0]),
                  gate(0), gate(1), gate(2)],
        out_specs=pl.BlockSpec((tm, tn), lambda i, j: (i, j)),
        out_shape=jax.ShapeDtypeStruct((m, n), BF16),
        compiler_params=_cparams("parallel", "parallel"),
        name="merge_branches",
    )(y_a, y_l, y_r, w_a, w_l, w_r, proj, proj, proj)


def _ffn_up_kernel(x_ref, wg_ref, wv_ref, cw_ref, cb_ref, o_ref, wgb_ref, wvb_ref, halo_ref,
                   *, tiles_per_seq, valid_tiles):
    j = pl.program_id(0)
    i = pl.program_id(1)

    @pl.when(i == 0)
    def _():
        wgb_ref[...] = wg_ref[...].astype(BF16)
        wvb_ref[...] = wv_ref[...].astype(BF16)

    @pl.when(i % tiles_per_seq == 0)
    def _():
        halo_ref[...] = jnp.zeros_like(halo_ref)

    @pl.when(j < valid_tiles)
    def _():
        x = x_ref[...]
        g = jnp.dot(x, wgb_ref[...], preferred_element_type=F32)
        tm = g.shape[0]
        tail = halo_ref[...]
        c = g * cw_ref[FFN_CONV - 1:FFN_CONV, :] + cb_ref[...]
        for sh in range(1, FFN_CONV):
            c = c + _shift_rows(g, sh, tail) * cw_ref[FFN_CONV - 1 - sh:FFN_CONV - sh, :]
        halo_ref[...] = g[tm - SUBLANES:tm, :]
        v = jnp.dot(x, wvb_ref[...], preferred_element_type=F32)
        o_ref[...] = (_gelu_tanh(c) * v).astype(o_ref.dtype)

    @pl.when(j >= valid_tiles)
    def _():
        o_ref[...] = jnp.zeros_like(o_ref)


def ffn_up(x, w, conv_w, conv_b, layer, *, seq, tm, tn, dff_pad):
    m, k = x.shape
    dff = conv_w.shape[-1]
    valid = dff // tn
    last = valid - 1
    return pl.pallas_call(
        functools.partial(_ffn_up_kernel, tiles_per_seq=seq // tm, valid_tiles=valid),
        grid=(dff_pad // tn, m // tm),
        in_specs=[pl.BlockSpec((tm, k), lambda j, i: (i, 0)),
                  pl.BlockSpec((None, k, tn), lambda j, i: (layer, 0, jnp.minimum(j, last))),
                  pl.BlockSpec((None, k, tn), lambda j, i: (layer, 0, valid + jnp.minimum(j, last))),
                  pl.BlockSpec((None, FFN_CONV, tn), lambda j, i: (layer, 0, jnp.minimum(j, last))),
                  pl.BlockSpec((None, 1, tn), lambda j, i: (layer, 0, jnp.minimum(j, last)))],
        out_specs=pl.BlockSpec((tm, tn), lambda j, i: (i, j)),
        out_shape=jax.ShapeDtypeStruct((m, dff_pad), BF16),
        scratch_shapes=[pltpu.VMEM((k, tn), BF16), pltpu.VMEM((k, tn), BF16),
                        pltpu.VMEM((SUBLANES, tn), F32)],
        compiler_params=_cparams("arbitrary", "arbitrary"),
        name="ffn_up",
    )(x, w, w, conv_w, conv_b)


def _rotate_half_cols(w):
    half = MLA_ROPE // 2
    return jnp.concatenate([-w[..., half:], w[..., :half]], axis=-1)


SRC_CKV = MLA_Q_RANK
SRC_KR = SRC_CKV + MLA_KV_RANK
SRC_LRU = SRC_KR + MLA_ROPE
SRC_RW = SRC_LRU + 2 * LRU_WIDTH
SRC_XWA = SRC_RW + 3 * RW_WIDTH
SRC_XG = SRC_XWA + RW_DECAY_LORA + RW_AAA_LORA
SRC_GATES = SRC_XG + RW_GATE_LORA


W_IN_TILE = 256
_T_LRU = (3 * D_MODEL) // W_IN_TILE
_T_RW = _T_LRU + (2 * LRU_WIDTH) // W_IN_TILE
_T_CQ = _T_RW + (3 * RW_WIDTH) // W_IN_TILE
_T_CKV = _T_CQ + Q_PAD // W_IN_TILE
_T_MIX = _T_CKV + MLA_KV_RANK // W_IN_TILE
_T_XG = _T_MIX + 1


def _w_in_src_row(t):
    tile = W_IN_TILE
    return jnp.where(
        t < _T_LRU, SRC_GATES + tile * t, jnp.where(
            t < _T_RW, SRC_LRU + tile * (t - _T_LRU), jnp.where(
                t < _T_CQ, SRC_RW + tile * (t - _T_RW), jnp.where(
                    t < _T_CKV, tile * (t - _T_CQ), jnp.where(
                        t < _T_MIX, SRC_CKV + tile * (t - _T_CKV), jnp.where(
                            t == _T_MIX, SRC_KR, SRC_XG))))))


def _w_in_kernel(a_ref, b_ref, o_ref):
    t = pl.program_id(0)
    a = a_ref[...]

    @pl.when(t != _T_MIX)
    def _():
        valid = jnp.where(t == _T_CKV - 1, MLA_Q_RANK - (Q_PAD - W_IN_TILE),
                          jnp.where(t == _T_XG, RW_GATE_LORA, W_IN_TILE))
        row = lax.broadcasted_iota(jnp.int32, a.shape, 0)
        o_ref[...] = jnp.where(row < valid, a, 0.0).astype(BF16)

    @pl.when(t == _T_MIX)
    def _():
        half = MLA_ROPE // 2
        lora = RW_DECAY_LORA + RW_AAA_LORA
        o_ref[...] = jnp.concatenate(
            [a[0:MLA_ROPE], -a[half:MLA_ROPE], a[0:half], b_ref[0:lora]], axis=0).astype(BF16)


def relayout_w_in(w_t, layer, *, k):
    window = lambda index: pl.BlockSpec((None, pl.Element(W_IN_TILE), pl.Element(k)), index)
    return pl.pallas_call(
        _w_in_kernel,
        grid=(IN_COLS_PAD // W_IN_TILE,),
        in_specs=[window(lambda t: (layer, pl.multiple_of(_w_in_src_row(t), SUBLANES), 0)),
                  window(lambda t: (layer, SRC_XWA, 0))],
        out_specs=pl.BlockSpec((W_IN_TILE, k), lambda t: (t, 0)),
        out_shape=jax.ShapeDtypeStruct((IN_COLS_PAD, k), BF16),
        compiler_params=_cparams("arbitrary"),
        name="w_in_relayout",
    )(w_t, w_t)


def _cast_kernel(s_ref, o_ref, *, valid_blocks):
    i = pl.program_id(0)

    @pl.when(i < valid_blocks)
    def _():
        o_ref[...] = s_ref[...].astype(o_ref.dtype)

    @pl.when(i >= valid_blocks)
    def _():
        o_ref[...] = jnp.zeros_like(o_ref)


def cast_bf16(w, layer=None, *, tr, rows_out=None):
    r, c = w.shape[-2:]
    rows_out = r if rows_out is None else rows_out
    valid = r // tr
    if layer is None:
        src = pl.BlockSpec((tr, c), lambda i: (jnp.minimum(i, valid - 1), 0))
    else:
        src = pl.BlockSpec((None, tr, c), lambda i: (layer, jnp.minimum(i, valid - 1), 0))
    return pl.pallas_call(
        functools.partial(_cast_kernel, valid_blocks=valid),
        grid=(rows_out // tr,),
        in_specs=[src],
        out_specs=pl.BlockSpec((tr, c), lambda i: (i, 0)),
        out_shape=jax.ShapeDtypeStruct((rows_out, c), BF16),
        compiler_params=_cparams("parallel"),
        name="cast_bf16",
    )(w)


def _prep_w_uq(w):
    rope = w[..., MLA_NOPE:]
    w = jnp.concatenate([w, _rotate_half_cols(rope)], axis=-1)
    w = w.reshape(MLA_Q_RANK, MLA_HEADS * 2 * LANES)
    return jnp.pad(w, ((0, Q_PAD - MLA_Q_RANK), (0, 0))).astype(BF16)


def _rope_table(positions):
    inv_freq = ROPE_THETA ** (-jnp.arange(0, MLA_ROPE, 2, dtype=F32) / MLA_ROPE)
    ang = positions.astype(F32)[..., None] * inv_freq
    cos, sin = jnp.cos(ang), jnp.sin(ang)
    return jnp.concatenate([cos, cos, sin, sin], axis=-1).reshape(-1, 2 * MLA_ROPE)


class _Tiles:
    in_proj = (1024, 1024)
    mla_up_rows = 512
    attn_q = 512
    attn_heads = 4
    lru_rows = 512
    rwkv_prep_rows = 256
    merge = (512, 1024)
    out_mm = (1024, 2048, 1024)
    ln_rows = 256
    ffn_up = (1024, 256)
    cast_rows = 256
    side_rows = 128


def kernel(x, positions, w_in, mla_q_norm, mla_w_uq, mla_kv_norm, mla_w_ukv, lru_conv_w, lru_conv_b, lru_w_a, lru_b_a, lru_w_x, lru_b_x, lru_lambda, rw_mu, rw_w0, rw_w2, rw_a0, rw_a2, rw_g2, rw_k_k, rw_k_a, rw_r_k, rw_gn_g, rw_gn_b, w_o_mla, w_o_lru, w_o_rwkv, w_out, ln1_g, ln1_b, ffn_w_up, ffn_conv_w, ffn_conv_b, ffn_w_down, ln2_g, ln2_b):
    batch, seq, d = x.shape
    m = batch * seq
    t = _Tiles
    cs = _rope_table(positions)
    xf = x.reshape(m, d)
    cast = functools.partial(cast_bf16, tr=t.cast_rows)
    xb = cast(xf)
    w_ukv_all = mla_w_ukv.reshape(DEPTH, MLA_KV_RANK, -1)
    conv_b_all = ffn_conv_b.reshape(DEPTH, 1, -1)
    w_in_t = jnp.transpose(w_in, (0, 2, 1))
    for l in range(DEPTH):
        proj, w_down = matmul_nt(xb, relayout_w_in(w_in_t, l, k=d),
                                 _SideCast(ffn_w_down, l, rows=t.side_rows, rows_out=D_FF_PAD),
                                 tm=t.in_proj[0], tn=t.in_proj[1])

        q_gain = jnp.pad(mla_q_norm[l], (0, Q_PAD - MLA_Q_RANK)).reshape(1, Q_PAD)
        q = mla_q_up(proj, q_gain, _prep_w_uq(mla_w_uq[l]), cs, batch=batch, seq=seq, tm=t.mla_up_rows)
        k, v = mla_kv_up(proj, mla_kv_norm[l].reshape(1, -1), cast(w_ukv_all, l), cs, batch=batch,
                         seq=seq, tm=t.mla_up_rows)
        y_a = mla_attention(q, k, v, tq=t.attn_q, hps=t.attn_heads).reshape(m, -1)

        y_l = rglru_branch(proj, lru_conv_w[l], lru_conv_b[l], lru_w_a[l], lru_b_a[l],
                           lru_w_x[l], lru_b_x[l], lru_lambda[l], batch=batch, seq=seq, ts=t.lru_rows)

        rw = rwkv_prep(proj, rw_mu[l], rw_w0[l], rw_w2[l], rw_a0[l], rw_a2[l], rw_g2[l],
                       rw_k_k[l], rw_k_a[l], batch=batch, seq=seq, ts=t.rwkv_prep_rows)
        sides = [_SideCast(w, l, rows=t.side_rows) for w in (w_o_mla, w_o_lru, w_o_rwkv, w_out)]
        y_r, wo_a, wo_l, wo_r, wo = rwkv_recurrence(*rw, rw_r_k[l], rw_gn_g[l], rw_gn_b[l], sides,
                                                    batch=batch, seq=seq)
        y_r = y_r.reshape(m, RW_WIDTH)

        merged = merge_branches(y_a, y_l, y_r, wo_a, wo_l, wo_r, proj, tm=t.merge[0], tn=t.merge[1])
        mm = dict(tm=t.out_mm[0], tn=t.out_mm[1], tk=t.out_mm[2], ln_rows=t.ln_rows)
        xf, xb = matmul_residual_layernorm(merged, wo, xf, ln1_g[l], ln1_b[l], name="mixer_out", **mm)

        h = ffn_up(xb, ffn_w_up, ffn_conv_w, conv_b_all, l, seq=seq, tm=t.ffn_up[0],
                   tn=t.ffn_up[1], dff_pad=D_FF_PAD)
        xf, xb = matmul_residual_layernorm(h, w_down, xf, ln2_g[l], ln2_b[l], name="ffn_down", **mm)
    return xf.reshape(batch, seq, d)
```

```python
import functools

import jax
import jax.numpy as jnp
from jax import lax
from jax.experimental import pallas as pl
from jax.experimental.pallas import tpu as pltpu

F32 = jnp.float32
BF16 = jnp.bfloat16

D_MODEL = 4096
DEPTH = 2
CHUNK = 64
MLA_HEADS = 16
MLA_Q_RANK = 896
MLA_KV_RANK = 512
MLA_NOPE = 128
MLA_ROPE = 64
MLA_V = 128
ROPE_THETA = 10000.0
LRU_WIDTH = 1024
LRU_BLOCKS = 8
LRU_BLOCK = LRU_WIDTH // LRU_BLOCKS
LRU_CONV = 4
LRU_C = 8.0
RW_WIDTH = 1024
RW_HEAD = 64
RW_HEADS = RW_WIDTH // RW_HEAD
RW_DECAY_LORA = 64
RW_AAA_LORA = 64
RW_GATE_LORA = 160
D_FF = 11008
FFN_CONV = 3
ALPHA = (2 * DEPTH) ** 0.25
LN_EPS = 1e-5
RMS_EPS = 1e-6
GN_EPS = 64e-5

LANES = 128
SUBLANES = 8
VMEM_LIMIT_BYTES = 56 * 1024 * 1024

Q_PAD = 1024
D_FF_PAD = 11264
XG_PAD = 256
OFF_GATES = 0
OFF_LRU_X = 3 * D_MODEL
OFF_LRU_G = OFF_LRU_X + LRU_WIDTH
OFF_RW_R = OFF_LRU_G + LRU_WIDTH
OFF_RW_K = OFF_RW_R + RW_WIDTH
OFF_RW_V = OFF_RW_K + RW_WIDTH
OFF_CQ = OFF_RW_V + RW_WIDTH
OFF_CKV = OFF_CQ + Q_PAD
OFF_KR = OFF_CKV + MLA_KV_RANK
OFF_XWA = OFF_KR + 2 * MLA_ROPE
OFF_XG = OFF_XWA + RW_DECAY_LORA + RW_AAA_LORA
IN_COLS_PAD = OFF_XG + XG_PAD

RW_GROUP_PAIRS = 4
_SQRT_2_OVER_PI = 0.7978845608028654
_LOG2_E = 1.4426950408889634


def _cparams(*sem):
    return pltpu.CompilerParams(dimension_semantics=sem, vmem_limit_bytes=VMEM_LIMIT_BYTES)


def _gelu_tanh(x):
    return 0.5 * x * (1.0 + jnp.tanh(_SQRT_2_OVER_PI * (x + 0.044715 * (x * x * x))))


def _softplus(z):
    return jnp.maximum(z, 0.0) + jnp.log1p(jnp.exp(-jnp.abs(z)))


def _shift_rows(x, shift, tail):
    row = lax.broadcasted_iota(jnp.int32, x.shape, 0)
    xs = pltpu.roll(x, shift, axis=0)
    for j in range(shift):
        src = SUBLANES - shift + j
        xs = jnp.where(row == j, tail[src:src + 1, :], xs)
    return xs


def _group_ones(rows, cols, group):
    shift = group.bit_length() - 1
    r = lax.broadcasted_iota(jnp.int32, (rows, cols), 0)
    c = lax.broadcasted_iota(jnp.int32, (rows, cols), 1)
    return jnp.where(((r & (cols - 1)) >> shift) == (c >> shift), 1.0, 0.0).astype(BF16)


def _head_sum(x, ones_hl):
    hi = x.astype(BF16)
    lo = (x - hi.astype(F32)).astype(BF16)
    return jnp.dot(jnp.concatenate([hi, lo], axis=1), ones_hl, preferred_element_type=F32)


class _SideCast:
    def __init__(self, w, layer, *, rows, rows_out=None):
        r, self.cols = w.shape[-2:]
        self.w, self.layer, self.rows = w, layer, rows
        self.valid = r // rows
        self.nblocks = (r if rows_out is None else rows_out) // rows

    def in_spec(self, step_of):
        last = self.valid - 1
        return pl.BlockSpec((None, self.rows, self.cols),
                            lambda *g: (self.layer, jnp.minimum(step_of(*g), last), 0))

    def out_spec(self, step_of):
        last = self.nblocks - 1
        return pl.BlockSpec((self.rows, self.cols), lambda *g: (jnp.minimum(step_of(*g), last), 0))

    def out_shape(self):
        return jax.ShapeDtypeStruct((self.nblocks * self.rows, self.cols), BF16)

    def emit(self, step, src_ref, dst_ref):
        if self.valid == self.nblocks:
            dst_ref[...] = src_ref[...].astype(BF16)
            return
        blk = jnp.minimum(step, self.nblocks - 1)

        @pl.when(blk < self.valid)
        def _():
            dst_ref[...] = src_ref[...].astype(BF16)

        @pl.when(blk >= self.valid)
        def _():
            dst_ref[...] = jnp.zeros_like(dst_ref)


def _mm_kernel(a_ref, b_ref, s_ref, o_ref, so_ref, *, side, nj):
    side.emit(pl.program_id(0) * nj + pl.program_id(1), s_ref, so_ref)
    o_ref[...] = lax.dot_general(a_ref[...], b_ref[...], (((1,), (1,)), ((), ())),
                                 preferred_element_type=F32).astype(o_ref.dtype)


def matmul_nt(a, b, side, *, tm, tn, out_dtype=F32):
    m, k = a.shape
    n, _ = b.shape
    nj = n // tn
    assert (m // tm) * nj >= side.nblocks
    step_of = lambda i, j: i * nj + j
    return pl.pallas_call(
        functools.partial(_mm_kernel, side=side, nj=nj),
        grid=(m // tm, nj),
        in_specs=[pl.BlockSpec((tm, k), lambda i, j: (i, 0)),
                  pl.BlockSpec((tn, k), lambda i, j: (j, 0)),
                  side.in_spec(step_of)],
        out_specs=[pl.BlockSpec((tm, tn), lambda i, j: (i, j)), side.out_spec(step_of)],
        out_shape=[jax.ShapeDtypeStruct((m, n), out_dtype), side.out_shape()],
        compiler_params=_cparams("arbitrary", "arbitrary"),
        name="in_proj",
    )(a, b, side.w)


def _mm_res_kernel(a_ref, w_ref, x_ref, o_ref, acc_ref):
    k = pl.program_id(2)
    last = pl.num_programs(2) - 1

    @pl.when(k == 0)
    def _():
        acc_ref[...] = ALPHA * x_ref[...]

    @pl.when(k < last)
    def _():
        acc_ref[...] += jnp.dot(a_ref[...], w_ref[...], preferred_element_type=F32)

    @pl.when(k == last)
    def _():
        o_ref[...] = acc_ref[...] + jnp.dot(a_ref[...], w_ref[...], preferred_element_type=F32)


def _ln_kernel(y_ref, g_ref, b_ref, o_ref, ob_ref):
    y = y_ref[...]
    mu = jnp.mean(y, axis=-1, keepdims=True)
    yc = y - mu
    var = jnp.mean(yc * yc, axis=-1, keepdims=True)
    o = yc * lax.rsqrt(var + LN_EPS) * g_ref[...] + b_ref[...]
    o_ref[...] = o
    ob_ref[...] = o.astype(BF16)


def matmul_residual_layernorm(a, w, x, g, b, *, tm, tn, tk, ln_rows, name):
    m, k = a.shape
    n = w.shape[1]
    y = pl.pallas_call(
        _mm_res_kernel,
        grid=(m // tm, n // tn, k // tk),
        in_specs=[pl.BlockSpec((tm, tk), lambda i, j, kk: (i, kk)),
                  pl.BlockSpec((tk, tn), lambda i, j, kk: (kk, j)),
                  pl.BlockSpec((tm, tn), lambda i, j, kk: (i, j))],
        out_specs=pl.BlockSpec((tm, tn), lambda i, j, kk: (i, j)),
        out_shape=jax.ShapeDtypeStruct((m, n), F32),
        scratch_shapes=[pltpu.VMEM((tm, tn), F32)],
        compiler_params=_cparams("parallel", "parallel", "arbitrary"),
        name=name,
    )(a, w, x)
    return pl.pallas_call(
        _ln_kernel,
        grid=(m // ln_rows,),
        in_specs=[pl.BlockSpec((ln_rows, n), lambda i: (i, 0)),
                  pl.BlockSpec((1, n), lambda i: (0, 0)),
                  pl.BlockSpec((1, n), lambda i: (0, 0))],
        out_specs=[pl.BlockSpec((ln_rows, n), lambda i: (i, 0)),
                   pl.BlockSpec((ln_rows, n), lambda i: (i, 0))],
        out_shape=[jax.ShapeDtypeStruct((m, n), F32), jax.ShapeDtypeStruct((m, n), BF16)],
        compiler_params=_cparams("parallel"),
        name="layernorm",
    )(y, g.reshape(1, n), b.reshape(1, n))


def _rope_half(block, cs):
    p = block * cs
    return p + pltpu.roll(p, MLA_ROPE, axis=1)


def _q_up_kernel(p_ref, g_ref, w_ref, cs_ref, q_ref, *, rank):
    x = p_ref[...]
    ms = jnp.sum(x * x, axis=-1, keepdims=True) * (1.0 / rank)
    xn = (x * lax.rsqrt(ms + RMS_EPS) * g_ref[...]).astype(BF16)
    cs = cs_ref[...]
    hw = 2 * LANES
    for h in range(q_ref.shape[1]):
        acc = jnp.dot(xn, w_ref[:, h * hw:(h + 1) * hw], preferred_element_type=F32)
        rot = _rope_half(acc[:, LANES:hw], cs)
        q_ref[0, h, :, 0:MLA_NOPE] = acc[:, 0:MLA_NOPE].astype(BF16)
        q_ref[0, h, :, MLA_NOPE:MLA_NOPE + MLA_ROPE] = rot[:, 0:MLA_ROPE].astype(BF16)


def mla_q_up(proj, gain, w, cs, *, batch, seq, tm):
    nh = MLA_HEADS
    spt = seq // tm
    return pl.pallas_call(
        functools.partial(_q_up_kernel, rank=MLA_Q_RANK),
        grid=(batch * spt,),
        in_specs=[pl.BlockSpec((tm, Q_PAD), lambda i: (i, OFF_CQ // Q_PAD)),
                  pl.BlockSpec((1, Q_PAD), lambda i: (0, 0)),
                  pl.BlockSpec(w.shape, lambda i: (0, 0)),
                  pl.BlockSpec((tm, LANES), lambda i: (i, 0))],
        out_specs=pl.BlockSpec((1, nh, tm, MLA_NOPE + MLA_ROPE), lambda i: (i // spt, 0, i % spt, 0)),
        out_shape=jax.ShapeDtypeStruct((batch, nh, seq, MLA_NOPE + MLA_ROPE), BF16),
        compiler_params=_cparams("parallel"),
        name="mla_q_up",
    )(proj, gain, w, cs)


def _kv_up_kernel(p_ref, kr_ref, g_ref, w_ref, cs_ref, k_ref, v_ref, *, rank):
    x = p_ref[...]
    ms = jnp.sum(x * x, axis=-1, keepdims=True) * (1.0 / rank)
    xn = (x * lax.rsqrt(ms + RMS_EPS) * g_ref[...]).astype(BF16)
    krope = _rope_half(kr_ref[...], cs_ref[...])[:, 0:MLA_ROPE].astype(BF16)
    hw = MLA_NOPE + MLA_V
    for h in range(k_ref.shape[1]):
        acc = jnp.dot(xn, w_ref[:, h * hw:(h + 1) * hw], preferred_element_type=F32)
        k_ref[0, h, :, 0:MLA_NOPE] = acc[:, 0:MLA_NOPE].astype(BF16)
        k_ref[0, h, :, MLA_NOPE:MLA_NOPE + MLA_ROPE] = krope
        v_ref[0, h] = acc[:, MLA_NOPE:hw].astype(BF16)


def mla_kv_up(proj, gain, w, cs, *, batch, seq, tm):
    nh = MLA_HEADS
    spt = seq // tm
    dk = MLA_NOPE + MLA_ROPE
    return pl.pallas_call(
        functools.partial(_kv_up_kernel, rank=MLA_KV_RANK),
        grid=(batch * spt,),
        in_specs=[pl.BlockSpec((tm, MLA_KV_RANK), lambda i: (i, OFF_CKV // MLA_KV_RANK)),
                  pl.BlockSpec((tm, LANES), lambda i: (i, OFF_KR // LANES)),
                  pl.BlockSpec((1, MLA_KV_RANK), lambda i: (0, 0)),
                  pl.BlockSpec(w.shape, lambda i: (0, 0)),
                  pl.BlockSpec((tm, LANES), lambda i: (i, 0))],
        out_specs=[pl.BlockSpec((1, nh, tm, dk), lambda i: (i // spt, 0, i % spt, 0)),
                   pl.BlockSpec((1, nh, tm, MLA_V), lambda i: (i // spt, 0, i % spt, 0))],
        out_shape=[jax.ShapeDtypeStruct((batch, nh, seq, dk), BF16),
                   jax.ShapeDtypeStruct((batch, nh, seq, MLA_V), BF16)],
        compiler_params=_cparams("parallel"),
        name="mla_kv_up",
    )(proj, proj, gain, w, cs)


def _flash_kernel(q_ref, k_ref, v_ref, s_ref, o_ref, so_ref, *, tq, scale, side):
    qi = pl.program_id(2)
    side.emit((pl.program_id(0) * pl.num_programs(1) + pl.program_id(1)) * pl.num_programs(2) + qi,
              s_ref, so_ref)
    heads = range(q_ref.shape[1])
    qs = [q_ref[0, h] for h in heads]

    def scores(h, j):
        k = k_ref[0, h, pl.ds(pl.multiple_of(j * tq, tq), tq), :]
        return lax.dot_general(qs[h], k, (((1,), (1,)), ((), ())),
                               preferred_element_type=F32) * (scale * _LOG2_E)

    def update(h, j, s, m, l, acc):
        v = v_ref[0, h, pl.ds(pl.multiple_of(j * tq, tq), tq), :]
        m_new = jnp.maximum(m, jnp.max(s, axis=-1, keepdims=True))
        alpha = jnp.exp2(m - m_new)
        p = jnp.exp2(s - m_new)
        l = alpha * l + jnp.sum(p, axis=-1, keepdims=True)
        acc = alpha * acc + jnp.dot(p.astype(BF16), v, preferred_element_type=F32)
        return m_new, l, acc

    def pair(jj, carry):
        out = []
        for h in heads:
            s_a, s_b = scores(h, 2 * jj), scores(h, 2 * jj + 1)
            out.append(update(h, 2 * jj + 1, s_b, *update(h, 2 * jj, s_a, *carry[h])))
        return tuple(out)

    def single(_, carry):
        return tuple(update(h, qi - 1, scores(h, qi - 1), *carry[h]) for h in heads)

    carry = tuple((jnp.full((tq, 1), -1e30, F32), jnp.zeros((tq, 1), F32),
                   jnp.zeros((tq, MLA_V), F32)) for _ in heads)
    carry = lax.fori_loop(0, qi >> 1, pair, carry)
    carry = lax.fori_loop(0, qi & 1, single, carry)
    shift = CHUNK.bit_length() - 1
    qc = lax.broadcasted_iota(jnp.int32, (tq, tq), 0) >> shift
    kc = lax.broadcasted_iota(jnp.int32, (tq, tq), 1) >> shift
    for h in heads:
        m, l, acc = update(h, qi, jnp.where(kc <= qc, scores(h, qi), -jnp.inf), *carry[h])
        o_ref[0, :, h * MLA_V:(h + 1) * MLA_V] = (acc / l).astype(o_ref.dtype)


def mla_attention(q, k, v, side, *, tq, hps):
    batch, nh, seq, dk = q.shape
    scale = (MLA_NOPE + MLA_ROPE) ** -0.5
    nhs, nq = nh // hps, seq // tq
    assert batch * nhs * nq >= side.nblocks
    step_of = lambda b, h, i: (b * nhs + h) * nq + i
    return pl.pallas_call(
        functools.partial(_flash_kernel, tq=tq, scale=scale, side=side),
        grid=(batch, nhs, nq),
        in_specs=[pl.BlockSpec((1, hps, tq, dk), lambda b, h, i: (b, h, i, 0)),
                  pl.BlockSpec((1, hps, seq, dk), lambda b, h, i: (b, h, 0, 0),
                               pipeline_mode=pl.Buffered(1)),
                  pl.BlockSpec((1, hps, seq, MLA_V), lambda b, h, i: (b, h, 0, 0),
                               pipeline_mode=pl.Buffered(1)),
                  side.in_spec(step_of)],
        out_specs=[pl.BlockSpec((1, tq, hps * MLA_V), lambda b, h, i: (b, i, h)),
                   side.out_spec(step_of)],
        out_shape=[jax.ShapeDtypeStruct((batch, seq, nh * MLA_V), BF16), side.out_shape()],
        compiler_params=_cparams("arbitrary", "arbitrary", "arbitrary"),
        name="mla_attention",
    )(q, k, v, side.w)


def _lru_kernel(x_ref, g_ref, cw_ref, cb_ref, wa_ref, ba_ref, wx_ref, bx_ref, lam_ref, o_ref,
                tail_ref, h_ref, a_scr, u_scr):
    @pl.when(pl.program_id(1) == 0)
    def _():
        tail_ref[...] = jnp.zeros_like(tail_ref)
        h_ref[...] = jnp.zeros_like(h_ref)

    x = x_ref[...]
    ts = x.shape[0]
    tail = tail_ref[...]
    xc = x * cw_ref[LRU_CONV - 1:LRU_CONV, :] + cb_ref[...]
    for sh in range(1, LRU_CONV):
        xc = xc + _shift_rows(x, sh, tail) * cw_ref[LRU_CONV - 1 - sh:LRU_CONV - sh, :]
    tail_ref[...] = x[ts - SUBLANES:ts, :]

    xb = xc.astype(BF16)
    ra, rx = [], []
    for n in range(LRU_BLOCKS):
        blk = xb[:, n * LRU_BLOCK:(n + 1) * LRU_BLOCK]
        ra.append(jnp.dot(blk, wa_ref[n], preferred_element_type=F32))
        rx.append(jnp.dot(blk, wx_ref[n], preferred_element_type=F32))
    r = jax.nn.sigmoid(jnp.concatenate(ra, axis=1) + ba_ref[...])
    gi = jax.nn.sigmoid(jnp.concatenate(rx, axis=1) + bx_ref[...])
    log_a = -LRU_C * r * _softplus(-lam_ref[...])
    a_scr[...] = jnp.exp(log_a)
    u_scr[...] = jnp.sqrt(1.0 - jnp.exp(2.0 * log_a)) * (gi * xc)

    row8 = lax.broadcasted_iota(jnp.int32, (SUBLANES, x.shape[1]), 0)

    def body(grp, h):
        base = pl.multiple_of(grp * SUBLANES, SUBLANES)
        a8 = a_scr[pl.ds(base, SUBLANES), :]
        u8 = u_scr[pl.ds(base, SUBLANES), :]
        hs = u8
        for s in range(SUBLANES):
            h = a8[s:s + 1, :] * h + u8[s:s + 1, :]
            hs = jnp.where(row8 == s, h, hs)
        u_scr[pl.ds(base, SUBLANES), :] = hs
        return h

    h_ref[...] = lax.fori_loop(0, ts // SUBLANES, body, h_ref[...])
    o_ref[...] = (u_scr[...] * _gelu_tanh(g_ref[...])).astype(o_ref.dtype)


def rglru_branch(proj, conv_w, conv_b, w_a, b_a, w_x, b_x, lam, *, batch, seq, ts):
    w = LRU_WIDTH
    spt = seq // ts
    vec = lambda: pl.BlockSpec((1, w), lambda b, s: (0, 0))
    blockdiag = lambda: pl.BlockSpec((LRU_BLOCKS, LRU_BLOCK, LRU_BLOCK), lambda b, s: (0, 0, 0))
    return pl.pallas_call(
        _lru_kernel,
        grid=(batch, spt),
        in_specs=[pl.BlockSpec((ts, w), lambda b, s: (b * spt + s, OFF_LRU_X // w)),
                  pl.BlockSpec((ts, w), lambda b, s: (b * spt + s, OFF_LRU_G // w)),
                  pl.BlockSpec((LRU_CONV, w), lambda b, s: (0, 0)),
                  vec(), blockdiag(), vec(), blockdiag(), vec(), vec()],
        out_specs=pl.BlockSpec((ts, w), lambda b, s: (b * spt + s, 0)),
        out_shape=jax.ShapeDtypeStruct((batch * seq, w), BF16),
        scratch_shapes=[pltpu.VMEM((SUBLANES, w), F32), pltpu.VMEM((1, w), F32),
                        pltpu.VMEM((ts, w), F32), pltpu.VMEM((ts, w), F32)],
        compiler_params=_cparams("parallel", "arbitrary"),
        name="rglru",
    )(proj, proj, conv_w, conv_b.reshape(1, w), w_a.astype(BF16), b_a.reshape(1, w),
      w_x.astype(BF16), b_x.reshape(1, w), lam.reshape(1, w))


def _rwkv_prep_kernel(r_ref, k_ref, v_ref, wa_ref, xg_ref,
                      mur_ref, muk_ref, muv_ref, muwa_ref, mug_ref,
                      w0_ref, w2_ref, a0_ref, a2_ref, g2_ref, kkw_ref, ka_ref,
                      ro_ref, ko_ref, vo_ref, nkk_ref, b_ref, d_ref, g_ref,
                      tr_ref, tk_ref, tv_ref, twa_ref, tg_ref):
    @pl.when(pl.program_id(1) == 0)
    def _():
        for t in (tr_ref, tk_ref, tv_ref, twa_ref, tg_ref):
            t[...] = jnp.zeros_like(t)

    def mix(x_ref, mu_ref, tail_ref):
        x = x_ref[...]
        prev = _shift_rows(x, 1, tail_ref[...])
        tail_ref[...] = x[x.shape[0] - SUBLANES:, :]
        return x + mu_ref[...] * (prev - x)

    r = mix(r_ref, mur_ref, tr_ref)
    k = mix(k_ref, muk_ref, tk_ref)
    v = mix(v_ref, muv_ref, tv_ref)
    xwa = mix(wa_ref, muwa_ref, twa_ref)
    xg = mix(xg_ref, mug_ref, tg_ref)
    w = -_softplus(-(w0_ref[...] + jnp.dot(jnp.tanh(xwa).astype(BF16), w2_ref[...],
                                           preferred_element_type=F32))) - 0.5
    a = jax.nn.sigmoid(a0_ref[...] + jnp.dot(xwa.astype(BF16), a2_ref[...], preferred_element_type=F32))
    g = jnp.dot(jax.nn.sigmoid(xg).astype(BF16), g2_ref[...], preferred_element_type=F32)

    kk = k * kkw_ref[...]
    ones_hl = _group_ones(2 * LANES, LANES, RW_HEAD)
    sq = kk * kk
    ss = jnp.concatenate([_head_sum(sq[:, c * LANES:(c + 1) * LANES], ones_hl)
                          for c in range(RW_WIDTH // LANES)], axis=1)
    kk = kk / jnp.maximum(jnp.sqrt(ss), 1e-12)

    ro_ref[...] = r
    ko_ref[...] = k * (1.0 + (a - 1.0) * ka_ref[...])
    vo_ref[...] = v
    nkk_ref[...] = -kk
    b_ref[...] = kk * a
    d_ref[...] = jnp.exp(-jnp.exp(w))
    g_ref[...] = g


def rwkv_prep(proj, mu, w0, w2, a0, a2, g2, k_k, k_a, *, batch, seq, ts):
    w = RW_WIDTH
    spt = seq // ts
    lora = RW_DECAY_LORA + RW_AAA_LORA
    row = lambda width, col: pl.BlockSpec((ts, width), lambda b, s: (b * spt + s, col))
    vec = lambda width: pl.BlockSpec((1, width), lambda b, s: (0, 0))
    mat = lambda rows: pl.BlockSpec((rows, w), lambda b, s: (0, 0))
    mu_r, mu_k, mu_v = (mu[i * w:(i + 1) * w].reshape(1, w) for i in range(3))
    mu_wa = mu[3 * w:3 * w + lora].reshape(1, lora)
    mu_g = jnp.pad(mu[3 * w + lora:], (0, XG_PAD - RW_GATE_LORA)).reshape(1, XG_PAD)
    zeros = jnp.zeros((RW_DECAY_LORA, w), w2.dtype)
    w2p = jnp.concatenate([w2, zeros], axis=0).astype(BF16)
    a2p = jnp.concatenate([zeros, a2], axis=0).astype(BF16)
    g2p = jnp.pad(g2, ((0, XG_PAD - RW_GATE_LORA), (0, 0))).astype(BF16)
    out = jax.ShapeDtypeStruct((batch * seq, w), F32)
    return pl.pallas_call(
        _rwkv_prep_kernel,
        grid=(batch, spt),
        in_specs=[row(w, OFF_RW_R // w), row(w, OFF_RW_K // w), row(w, OFF_RW_V // w),
                  row(lora, OFF_XWA // lora), row(XG_PAD, OFF_XG // XG_PAD),
                  vec(w), vec(w), vec(w), vec(lora), vec(XG_PAD),
                  vec(w), mat(lora), vec(w), mat(lora), mat(XG_PAD), vec(w), vec(w)],
        out_specs=[pl.BlockSpec((ts, w), lambda b, s: (b * spt + s, 0))] * 7,
        out_shape=[out] * 7,
        scratch_shapes=[pltpu.VMEM((SUBLANES, w), F32)] * 3
                       + [pltpu.VMEM((SUBLANES, lora), F32), pltpu.VMEM((SUBLANES, XG_PAD), F32)],
        compiler_params=_cparams("parallel", "arbitrary"),
        name="rwkv_prep",
    )(proj, proj, proj, proj, proj, mu_r, mu_k, mu_v, mu_wa, mu_g,
      w0.reshape(1, w), w2p, a0.reshape(1, w), a2p, g2p, k_k.reshape(1, w), k_a.reshape(1, w))


def _rwkv_rec_kernel(*refs, sides):
    ns = len(sides)
    r_ref, k_ref, v_ref, nkk_ref, b_ref, d_ref, g_ref, rk_ref, gng_ref, gnb_ref = refs[:10]
    side_src = refs[10:10 + ns]
    y_ref = refs[10 + ns]
    side_dst = refs[11 + ns:11 + 2 * ns]
    st_ref, q_ref, ot_ref = refs[11 + 2 * ns:]
    for side, src, dst in zip(sides, side_src, side_dst):
        side.emit(pl.program_id(0), src, dst)

    nb, tau, width = r_ref.shape
    npair = width // LANES
    hd = RW_HEAD
    pairs = [(bi, p) for bi in range(nb) for p in range(npair)]
    flat = [(i, bi, p) for i, (bi, p) in enumerate(pairs)]
    groups = [flat[j:j + RW_GROUP_PAIRS] for j in range(0, len(flat), RW_GROUP_PAIRS)]

    @pl.when(pl.program_id(0) == 0)
    def _():
        st_ref[...] = jnp.zeros_like(st_ref)

    q_ref[...] = jnp.zeros_like(q_ref)
    ot_ref[...] = jnp.zeros_like(ot_ref)

    row = lax.broadcasted_iota(jnp.int32, (hd, LANES), 0)
    lane = lax.broadcasted_iota(jnp.int32, (hd, LANES), 1)
    lane_in_head = lane & (hd - 1)
    diag = lane_in_head == row
    ones2 = _group_ones(2 * LANES, 2 * LANES, hd)

    def group(grp, carry):
        base = pl.multiple_of(grp * SUBLANES, SUBLANES)
        tile = lambda ref, bi, p: ref[bi, pl.ds(base, SUBLANES), pl.ds(p * LANES, LANES)]
        vcb = [None] * len(groups)
        for s in range(SUBLANES):
            sel = lane_in_head == (base + s - 1)
            for gi, grp_pairs in enumerate(groups):
                lhs = []
                for i, bi, p in grp_pairs:
                    pm = (st_ref[bi, p] * tile(nkk_ref, bi, p)[s:s + 1, :]).astype(BF16)
                    lhs.append(jnp.concatenate([pm, q_ref[pl.ds(i * hd, hd), :]], axis=1))
                if s % 2 == 0:
                    for i, bi, p in grp_pairs:
                        v8 = tile(v_ref, bi, p)
                        lhs.append(jnp.concatenate(
                            [jnp.where(diag, v8[s:s + 1, :], 0.0).astype(BF16),
                             jnp.where(diag, v8[s + 1:s + 2, :], 0.0).astype(BF16)], axis=1))
                out = jnp.dot(jnp.concatenate(lhs, axis=0), ones2, preferred_element_type=F32)
                if s % 2 == 0:
                    vcb[gi] = out[len(grp_pairs) * hd:, :]
                for j, (i, bi, p) in enumerate(grp_pairs):
                    rows = slice(j * hd, (j + 1) * hd)
                    ot_ref[bi, p] = jnp.where(sel, out[rows, LANES:], ot_ref[bi, p])
                    vc = vcb[gi][rows, 0:LANES] if s % 2 == 0 else vcb[gi][rows, LANES:]
                    new = (st_ref[bi, p] * tile(d_ref, bi, p)[s:s + 1, :]
                           + out[rows, 0:LANES] * tile(b_ref, bi, p)[s:s + 1, :]
                           + vc * tile(k_ref, bi, p)[s:s + 1, :])
                    st_ref[bi, p] = new
                    q_ref[pl.ds(i * hd, hd), :] = (new * tile(r_ref, bi, p)[s:s + 1, :]).astype(BF16)
        return carry

    lax.fori_loop(0, tau // SUBLANES, group, 0)

    o_last = jnp.dot(q_ref[...], ones2[0:LANES, 0:LANES], preferred_element_type=F32)
    sel = lane_in_head == (tau - 1)
    for i, (bi, p) in enumerate(pairs):
        ot_ref[bi, p] = jnp.where(sel, o_last[i * hd:(i + 1) * hd, :], ot_ref[bi, p])

    ones_hl = _group_ones(2 * LANES, LANES, hd)
    first_head = lane < hd
    inv_hd = 1.0 / hd
    for bi in range(nb):
        nat = []
        for p0 in range(0, npair, 2):
            tr = jnp.concatenate([ot_ref[bi, p0], ot_ref[bi, p0 + 1]], axis=0).T
            top, bot = tr[0:hd, :], tr[hd:2 * hd, :]
            nat.append(jnp.where(first_head, top, pltpu.roll(bot, hd, axis=1)))
            nat.append(jnp.where(first_head, pltpu.roll(top, hd, axis=1), bot))
        cols = [pl.ds(p * LANES, LANES) for p in range(npair)]
        o = jnp.concatenate(nat, axis=0)
        mean = _head_sum(o, ones_hl) * inv_hd
        oc = o - mean
        var = _head_sum(oc * oc, ones_hl) * inv_hd
        rkr = jnp.concatenate([r_ref[bi, :, c] * k_ref[bi, :, c] * rk_ref[:, c] for c in cols], axis=0)
        bonus = _head_sum(rkr, ones_hl)
        on = oc * lax.rsqrt(var + GN_EPS)
        for p, c in enumerate(cols):
            rows = slice(p * tau, (p + 1) * tau)
            y = on[rows] * gng_ref[:, c] + gnb_ref[:, c] + bonus[rows] * v_ref[bi, :, c]
            y_ref[bi, :, c] = (y * g_ref[bi, :, c]).astype(y_ref.dtype)


def rwkv_recurrence(r, k, v, nkk, b, d, g, r_k, gn_g, gn_b, sides, *, batch, seq):
    width = RW_WIDTH
    tau = RW_HEAD
    npair = width // LANES
    nsteps = seq // tau
    assert all(s.nblocks <= nsteps for s in sides)
    spec = pl.BlockSpec((batch, tau, width), lambda i: (0, i, 0))
    vec = pl.BlockSpec((1, width), lambda i: (0, 0))
    acts = [a.reshape(batch, seq, width) for a in (r, k, v, nkk, b, d, g)]
    step_of = lambda i: i
    return pl.pallas_call(
        functools.partial(_rwkv_rec_kernel, sides=sides),
        grid=(nsteps,),
        in_specs=[spec] * 7 + [vec] * 3 + [s.in_spec(step_of) for s in sides],
        out_specs=[spec] + [s.out_spec(step_of) for s in sides],
        out_shape=[jax.ShapeDtypeStruct((batch, seq, width), BF16)] + [s.out_shape() for s in sides],
        scratch_shapes=[pltpu.VMEM((batch, npair, RW_HEAD, LANES), F32),
                        pltpu.VMEM((batch * npair * RW_HEAD, LANES), BF16),
                        pltpu.VMEM((batch, npair, RW_HEAD, LANES), F32)],
        compiler_params=_cparams("arbitrary"),
        name="rwkv_recurrence",
    )(*acts, r_k.reshape(1, width), gn_g.reshape(1, width), gn_b.reshape(1, width),
      *[s.w for s in sides])


def _merge_kernel(a_ref, l_ref, r_ref, wa_ref, wl_ref, wr_ref, ga_ref, gl_ref, gr_ref, o_ref):
    acc = jax.nn.sigmoid(ga_ref[...]) * jnp.dot(a_ref[...], wa_ref[...], preferred_element_type=F32)
    acc += jax.nn.sigmoid(gl_ref[...]) * jnp.dot(l_ref[...], wl_ref[...], preferred_element_type=F32)
    acc += jax.nn.sigmoid(gr_ref[...]) * jnp.dot(r_ref[...], wr_ref[...], preferred_element_type=F32)
    o_ref[...] = acc.astype(o_ref.dtype)


def merge_branches(y_a, y_l, y_r, w_a, w_l, w_r, proj, *, tm, tn):
    m = y_a.shape[0]
    n = w_a.shape[1]
    nj = n // tn
    act = lambda width: pl.BlockSpec((tm, width), lambda i, j: (i, 0))
    wgt = lambda width: pl.BlockSpec((width, tn), lambda i, j: (0, j))
    gate = lambda g: pl.BlockSpec((tm, tn), lambda i, j: (i, g * nj + j))
    return pl.pallas_call(
        _merge_kernel,
        grid=(m // tm, nj),
        in_specs=[act(y_a.shape[1]), act(y_l.shape[1]), act(y_r.shape[1]),
                  wgt(w_a.shape[0]), wgt(w_l.shape[0]), wgt(w_r.shape[0]),
                  gate(0), gate(1), gate(2)],
        out_specs=pl.BlockSpec((tm, tn), lambda i, j: (i, j)),
        out_shape=jax.ShapeDtypeStruct((m, n), BF16),
        compiler_params=_cparams("parallel", "parallel"),
        name="merge_branches",
    )(y_a, y_l, y_r, w_a, w_l, w_r, proj, proj, proj)


def _ffn_up_kernel(x_ref, *refs, tiles_per_seq, valid_tiles, nsub):
    wg_refs, wv_refs = refs[0:nsub], refs[nsub:2 * nsub]
    cw_refs, cb_refs = refs[2 * nsub:3 * nsub], refs[3 * nsub:4 * nsub]
    o_ref, halo_ref = refs[4 * nsub], refs[4 * nsub + 1]
    j = pl.program_id(0)
    i = pl.program_id(1)

    @pl.when(i % tiles_per_seq == 0)
    def _():
        halo_ref[...] = jnp.zeros_like(halo_ref)

    x = x_ref[...]
    tm = x.shape[0]
    for c in range(nsub):
        tn = wg_refs[c].shape[1]
        cols = slice(c * tn, (c + 1) * tn)
        g = jnp.dot(x, wg_refs[c][...], preferred_element_type=F32)
        tail = halo_ref[:, cols]
        acc = g * cw_refs[c][FFN_CONV - 1:FFN_CONV, :] + cb_refs[c][...]
        for sh in range(1, FFN_CONV):
            acc = acc + _shift_rows(g, sh, tail) * cw_refs[c][FFN_CONV - 1 - sh:FFN_CONV - sh, :]
        halo_ref[:, cols] = g[tm - SUBLANES:tm, :]
        v = jnp.dot(x, wv_refs[c][...], preferred_element_type=F32)
        h = _gelu_tanh(acc) * v
        o_ref[:, cols] = jnp.where(j * nsub + c < valid_tiles, h, 0.0).astype(o_ref.dtype)


def ffn_up(x, w, conv_w, conv_b, layer, *, seq, tm, tn, nsub, dff_pad):
    m, k = x.shape
    dff = conv_w.shape[-1]
    valid = dff // tn
    last = valid - 1
    tile = lambda c: (lambda j, i: jnp.minimum(j * nsub + c, last))
    wspec = lambda c, off: pl.BlockSpec((k, tn), lambda j, i: (0, off + tile(c)(j, i)))
    pspec = lambda c, rows: pl.BlockSpec((None, rows, tn), lambda j, i: (layer, 0, tile(c)(j, i)))
    subs = range(nsub)
    return pl.pallas_call(
        functools.partial(_ffn_up_kernel, tiles_per_seq=seq // tm, valid_tiles=valid, nsub=nsub),
        grid=(dff_pad // (tn * nsub), m // tm),
        in_specs=[pl.BlockSpec((tm, k), lambda j, i: (i, 0))]
                 + [wspec(c, 0) for c in subs] + [wspec(c, valid) for c in subs]
                 + [pspec(c, FFN_CONV) for c in subs] + [pspec(c, 1) for c in subs],
        out_specs=pl.BlockSpec((tm, tn * nsub), lambda j, i: (i, j)),
        out_shape=jax.ShapeDtypeStruct((m, dff_pad), BF16),
        scratch_shapes=[pltpu.VMEM((SUBLANES, tn * nsub), F32)],
        compiler_params=_cparams("arbitrary", "arbitrary"),
        name="ffn_up",
    )(x, *([w] * (2 * nsub)), *([conv_w] * nsub), *([conv_b] * nsub))


def _rotate_half_cols(w):
    half = MLA_ROPE // 2
    return jnp.concatenate([-w[..., half:], w[..., :half]], axis=-1)


SRC_CKV = MLA_Q_RANK
SRC_KR = SRC_CKV + MLA_KV_RANK
SRC_LRU = SRC_KR + MLA_ROPE
SRC_RW = SRC_LRU + 2 * LRU_WIDTH
SRC_XWA = SRC_RW + 3 * RW_WIDTH
SRC_XG = SRC_XWA + RW_DECAY_LORA + RW_AAA_LORA
SRC_GATES = SRC_XG + RW_GATE_LORA


W_IN_TILE = 256
_T_LRU = (3 * D_MODEL) // W_IN_TILE
_T_RW = _T_LRU + (2 * LRU_WIDTH) // W_IN_TILE
_T_CQ = _T_RW + (3 * RW_WIDTH) // W_IN_TILE
_T_CKV = _T_CQ + Q_PAD // W_IN_TILE
_T_MIX = _T_CKV + MLA_KV_RANK // W_IN_TILE
_T_XG = _T_MIX + 1


def _w_in_src_row(t):
    tile = W_IN_TILE
    return jnp.where(
        t < _T_LRU, SRC_GATES + tile * t, jnp.where(
            t < _T_RW, SRC_LRU + tile * (t - _T_LRU), jnp.where(
                t < _T_CQ, SRC_RW + tile * (t - _T_RW), jnp.where(
                    t < _T_CKV, tile * (t - _T_CQ), jnp.where(
                        t < _T_MIX, SRC_CKV + tile * (t - _T_CKV), jnp.where(
                            t == _T_MIX, SRC_KR, SRC_XG))))))


def _w_in_kernel(a_ref, b_ref, o_ref):
    t = pl.program_id(0)
    a = a_ref[...]

    @pl.when(t != _T_MIX)
    def _():
        valid = jnp.where(t == _T_CKV - 1, MLA_Q_RANK - (Q_PAD - W_IN_TILE),
                          jnp.where(t == _T_XG, RW_GATE_LORA, W_IN_TILE))
        row = lax.broadcasted_iota(jnp.int32, a.shape, 0)
        o_ref[...] = jnp.where(row < valid, a, 0.0).astype(BF16)

    @pl.when(t == _T_MIX)
    def _():
        half = MLA_ROPE // 2
        lora = RW_DECAY_LORA + RW_AAA_LORA
        o_ref[...] = jnp.concatenate(
            [a[0:MLA_ROPE], -a[half:MLA_ROPE], a[0:half], b_ref[0:lora]], axis=0).astype(BF16)


def relayout_w_in(w_t, layer, *, k):
    window = lambda index: pl.BlockSpec((None, pl.Element(W_IN_TILE), pl.Element(k)), index)
    return pl.pallas_call(
        _w_in_kernel,
        grid=(IN_COLS_PAD // W_IN_TILE,),
        in_specs=[window(lambda t: (layer, pl.multiple_of(_w_in_src_row(t), SUBLANES), 0)),
                  window(lambda t: (layer, SRC_XWA, 0))],
        out_specs=pl.BlockSpec((W_IN_TILE, k), lambda t: (t, 0)),
        out_shape=jax.ShapeDtypeStruct((IN_COLS_PAD, k), BF16),
        compiler_params=_cparams("arbitrary"),
        name="w_in_relayout",
    )(w_t, w_t)


def _cast_kernel(s_ref, o_ref, *, valid_blocks):
    i = pl.program_id(0)

    @pl.when(i < valid_blocks)
    def _():
        o_ref[...] = s_ref[...].astype(o_ref.dtype)

    @pl.when(i >= valid_blocks)
    def _():
        o_ref[...] = jnp.zeros_like(o_ref)


def cast_bf16(w, layer=None, *, tr, rows_out=None):
    r, c = w.shape[-2:]
    rows_out = r if rows_out is None else rows_out
    valid = r // tr
    if layer is None:
        src = pl.BlockSpec((tr, c), lambda i: (jnp.minimum(i, valid - 1), 0))
    else:
        src = pl.BlockSpec((None, tr, c), lambda i: (layer, jnp.minimum(i, valid - 1), 0))
    return pl.pallas_call(
        functools.partial(_cast_kernel, valid_blocks=valid),
        grid=(rows_out // tr,),
        in_specs=[src],
        out_specs=pl.BlockSpec((tr, c), lambda i: (i, 0)),
        out_shape=jax.ShapeDtypeStruct((rows_out, c), BF16),
        compiler_params=_cparams("parallel"),
        name="cast_bf16",
    )(w)


def _prep_w_uq(w):
    rope = w[..., MLA_NOPE:]
    w = jnp.concatenate([w, _rotate_half_cols(rope)], axis=-1)
    w = w.reshape(MLA_Q_RANK, MLA_HEADS * 2 * LANES)
    return jnp.pad(w, ((0, Q_PAD - MLA_Q_RANK), (0, 0))).astype(BF16)


def _rope_table(positions):
    inv_freq = ROPE_THETA ** (-jnp.arange(0, MLA_ROPE, 2, dtype=F32) / MLA_ROPE)
    ang = positions.astype(F32)[..., None] * inv_freq
    cos, sin = jnp.cos(ang), jnp.sin(ang)
    return jnp.concatenate([cos, cos, sin, sin], axis=-1).reshape(-1, 2 * MLA_ROPE)


class _Tiles:
    in_proj = (1024, 1024)
    mla_up_rows = 512
    attn_q = 512
    attn_heads = 4
    lru_rows = 512
    rwkv_prep_rows = 256
    merge = (512, 1024)
    out_mm = (1024, 2048, 1024)
    ln_rows = 256
    ffn_up = (1024, 256, 2)
    attn_side_rows = 64
    cast_rows = 256
    side_rows = 128


def kernel(x, positions, w_in, mla_q_norm, mla_w_uq, mla_kv_norm, mla_w_ukv, lru_conv_w, lru_conv_b, lru_w_a, lru_b_a, lru_w_x, lru_b_x, lru_lambda, rw_mu, rw_w0, rw_w2, rw_a0, rw_a2, rw_g2, rw_k_k, rw_k_a, rw_r_k, rw_gn_g, rw_gn_b, w_o_mla, w_o_lru, w_o_rwkv, w_out, ln1_g, ln1_b, ffn_w_up, ffn_conv_w, ffn_conv_b, ffn_w_down, ln2_g, ln2_b):
    batch, seq, d = x.shape
    m = batch * seq
    t = _Tiles
    cs = _rope_table(positions)
    xf = x.reshape(m, d)
    cast = functools.partial(cast_bf16, tr=t.cast_rows)
    xb = cast(xf)
    w_ukv_all = mla_w_ukv.reshape(DEPTH, MLA_KV_RANK, -1)
    conv_b_all = ffn_conv_b.reshape(DEPTH, 1, -1)
    w_in_t = jnp.transpose(w_in, (0, 2, 1))
    for l in range(DEPTH):
        proj, w_down = matmul_nt(xb, relayout_w_in(w_in_t, l, k=d),
                                 _SideCast(ffn_w_down, l, rows=t.side_rows, rows_out=D_FF_PAD),
                                 tm=t.in_proj[0], tn=t.in_proj[1])

        q_gain = jnp.pad(mla_q_norm[l], (0, Q_PAD - MLA_Q_RANK)).reshape(1, Q_PAD)
        q = mla_q_up(proj, q_gain, _prep_w_uq(mla_w_uq[l]), cs, batch=batch, seq=seq, tm=t.mla_up_rows)
        k, v = mla_kv_up(proj, mla_kv_norm[l].reshape(1, -1), cast(w_ukv_all, l), cs, batch=batch,
                         seq=seq, tm=t.mla_up_rows)
        y_a, w_up = mla_attention(q, k, v, _SideCast(ffn_w_up, l, rows=t.attn_side_rows),
                                  tq=t.attn_q, hps=t.attn_heads)
        y_a = y_a.reshape(m, -1)

        y_l = rglru_branch(proj, lru_conv_w[l], lru_conv_b[l], lru_w_a[l], lru_b_a[l],
                           lru_w_x[l], lru_b_x[l], lru_lambda[l], batch=batch, seq=seq, ts=t.lru_rows)

        rw = rwkv_prep(proj, rw_mu[l], rw_w0[l], rw_w2[l], rw_a0[l], rw_a2[l], rw_g2[l],
                       rw_k_k[l], rw_k_a[l], batch=batch, seq=seq, ts=t.rwkv_prep_rows)
        sides = [_SideCast(w, l, rows=t.side_rows) for w in (w_o_mla, w_o_lru, w_o_rwkv, w_out)]
        y_r, wo_a, wo_l, wo_r, wo = rwkv_recurrence(*rw, rw_r_k[l], rw_gn_g[l], rw_gn_b[l], sides,
                                                    batch=batch, seq=seq)
        y_r = y_r.reshape(m, RW_WIDTH)

        merged = merge_branches(y_a, y_l, y_r, wo_a, wo_l, wo_r, proj, tm=t.merge[0], tn=t.merge[1])
        mm = dict(tm=t.out_mm[0], tn=t.out_mm[1], tk=t.out_mm[2], ln_rows=t.ln_rows)
        xf, xb = matmul_residual_layernorm(merged, wo, xf, ln1_g[l], ln1_b[l], name="mixer_out", **mm)

        h = ffn_up(xb, w_up, ffn_conv_w, conv_b_all, l, seq=seq, tm=t.ffn_up[0], tn=t.ffn_up[1],
                   nsub=t.ffn_up[2], dff_pad=D_FF_PAD)
        xf, xb = matmul_residual_layernorm(h, w_down, xf, ln2_g[l], ln2_b[l], name="ffn_down", **mm)
    return xf.reshape(batch, seq, d)
```

```python
import functools

import jax
import jax.numpy as jnp
from jax import lax
from jax.experimental import pallas as pl
from jax.experimental.pallas import tpu as pltpu

F32 = jnp.float32
BF16 = jnp.bfloat16

D_MODEL = 4096
DEPTH = 2
CHUNK = 64
MLA_HEADS = 16
MLA_Q_RANK = 896
MLA_KV_RANK = 512
MLA_NOPE = 128
MLA_ROPE = 64
MLA_V = 128
ROPE_THETA = 10000.0
LRU_WIDTH = 1024
LRU_BLOCKS = 8
LRU_BLOCK = LRU_WIDTH // LRU_BLOCKS
LRU_CONV = 4
LRU_C = 8.0
RW_WIDTH = 1024
RW_HEAD = 64
RW_HEADS = RW_WIDTH // RW_HEAD
RW_DECAY_LORA = 64
RW_AAA_LORA = 64
RW_GATE_LORA = 160
D_FF = 11008
FFN_CONV = 3
ALPHA = (2 * DEPTH) ** 0.25
LN_EPS = 1e-5
RMS_EPS = 1e-6
GN_EPS = 64e-5

LANES = 128
SUBLANES = 8
VMEM_LIMIT_BYTES = 56 * 1024 * 1024

Q_PAD = 1024
D_FF_PAD = 11264
XG_PAD = 256
OFF_GATES = 0
OFF_LRU_X = 3 * D_MODEL
OFF_LRU_G = OFF_LRU_X + LRU_WIDTH
OFF_RW_R = OFF_LRU_G + LRU_WIDTH
OFF_RW_K = OFF_RW_R + RW_WIDTH
OFF_RW_V = OFF_RW_K + RW_WIDTH
OFF_CQ = OFF_RW_V + RW_WIDTH
OFF_CKV = OFF_CQ + Q_PAD
OFF_KR = OFF_CKV + MLA_KV_RANK
OFF_XWA = OFF_KR + 2 * MLA_ROPE
OFF_XG = OFF_XWA + RW_DECAY_LORA + RW_AAA_LORA
IN_COLS_PAD = OFF_XG + XG_PAD

RW_GROUP_PAIRS = 4
_SQRT_2_OVER_PI = 0.7978845608028654
_LOG2_E = 1.4426950408889634


def _cparams(*sem):
    return pltpu.CompilerParams(dimension_semantics=sem, vmem_limit_bytes=VMEM_LIMIT_BYTES)


def _gelu_tanh(x):
    return 0.5 * x * (1.0 + jnp.tanh(_SQRT_2_OVER_PI * (x + 0.044715 * (x * x * x))))


def _softplus(z):
    return jnp.maximum(z, 0.0) + jnp.log1p(jnp.exp(-jnp.abs(z)))


def _shift_rows(x, shift, tail):
    row = lax.broadcasted_iota(jnp.int32, x.shape, 0)
    xs = pltpu.roll(x, shift, axis=0)
    for j in range(shift):
        src = SUBLANES - shift + j
        xs = jnp.where(row == j, tail[src:src + 1, :], xs)
    return xs


def _group_ones(rows, cols, group):
    shift = group.bit_length() - 1
    r = lax.broadcasted_iota(jnp.int32, (rows, cols), 0)
    c = lax.broadcasted_iota(jnp.int32, (rows, cols), 1)
    return jnp.where(((r & (cols - 1)) >> shift) == (c >> shift), 1.0, 0.0).astype(BF16)


def _head_sum(x, ones_hl):
    hi = x.astype(BF16)
    lo = (x - hi.astype(F32)).astype(BF16)
    return jnp.dot(jnp.concatenate([hi, lo], axis=1), ones_hl, preferred_element_type=F32)


class _SideCast:
    def __init__(self, w, layer, *, rows, rows_out=None):
        r, self.cols = w.shape[-2:]
        self.w, self.layer, self.rows = w, layer, rows
        self.valid = r // rows
        self.nblocks = (r if rows_out is None else rows_out) // rows

    def in_spec(self, step_of):
        last = self.valid - 1
        return pl.BlockSpec((None, self.rows, self.cols),
                            lambda *g: (self.layer, jnp.minimum(step_of(*g), last), 0))

    def out_spec(self, step_of):
        last = self.nblocks - 1
        return pl.BlockSpec((self.rows, self.cols), lambda *g: (jnp.minimum(step_of(*g), last), 0))

    def out_shape(self):
        return jax.ShapeDtypeStruct((self.nblocks * self.rows, self.cols), BF16)

    def emit(self, step, src_ref, dst_ref):
        if self.valid == self.nblocks:
            dst_ref[...] = src_ref[...].astype(BF16)
            return
        blk = jnp.minimum(step, self.nblocks - 1)

        @pl.when(blk < self.valid)
        def _():
            dst_ref[...] = src_ref[...].astype(BF16)

        @pl.when(blk >= self.valid)
        def _():
            dst_ref[...] = jnp.zeros_like(dst_ref)


def _mm_kernel(a_ref, b_ref, s_ref, o_ref, so_ref, *, side, nj):
    side.emit(pl.program_id(0) * nj + pl.program_id(1), s_ref, so_ref)
    o_ref[...] = lax.dot_general(a_ref[...], b_ref[...], (((1,), (1,)), ((), ())),
                                 preferred_element_type=F32).astype(o_ref.dtype)


def matmul_nt(a, b, side, *, tm, tn, out_dtype=F32):
    m, k = a.shape
    n, _ = b.shape
    nj = n // tn
    assert (m // tm) * nj >= side.nblocks
    step_of = lambda i, j: i * nj + j
    return pl.pallas_call(
        functools.partial(_mm_kernel, side=side, nj=nj),
        grid=(m // tm, nj),
        in_specs=[pl.BlockSpec((tm, k), lambda i, j: (i, 0)),
                  pl.BlockSpec((tn, k), lambda i, j: (j, 0)),
                  side.in_spec(step_of)],
        out_specs=[pl.BlockSpec((tm, tn), lambda i, j: (i, j)), side.out_spec(step_of)],
        out_shape=[jax.ShapeDtypeStruct((m, n), out_dtype), side.out_shape()],
        compiler_params=_cparams("arbitrary", "arbitrary"),
        name="in_proj",
    )(a, b, side.w)


def _mm_res_kernel(a_ref, w_ref, x_ref, o_ref, acc_ref):
    k = pl.program_id(2)
    last = pl.num_programs(2) - 1

    @pl.when(k == 0)
    def _():
        acc_ref[...] = ALPHA * x_ref[...]

    @pl.when(k < last)
    def _():
        acc_ref[...] += jnp.dot(a_ref[...], w_ref[...], preferred_element_type=F32)

    @pl.when(k == last)
    def _():
        o_ref[...] = acc_ref[...] + jnp.dot(a_ref[...], w_ref[...], preferred_element_type=F32)


def _ln_kernel(y_ref, g_ref, b_ref, o_ref, ob_ref):
    y = y_ref[...]
    mu = jnp.mean(y, axis=-1, keepdims=True)
    yc = y - mu
    var = jnp.mean(yc * yc, axis=-1, keepdims=True)
    o = yc * lax.rsqrt(var + LN_EPS) * g_ref[...] + b_ref[...]
    o_ref[...] = o
    ob_ref[...] = o.astype(BF16)


def matmul_residual_layernorm(a, w, x, g, b, *, tm, tn, tk, ln_rows, name):
    m, k = a.shape
    n = w.shape[1]
    y = pl.pallas_call(
        _mm_res_kernel,
        grid=(m // tm, n // tn, k // tk),
        in_specs=[pl.BlockSpec((tm, tk), lambda i, j, kk: (i, kk)),
                  pl.BlockSpec((tk, tn), lambda i, j, kk: (kk, j)),
                  pl.BlockSpec((tm, tn), lambda i, j, kk: (i, j))],
        out_specs=pl.BlockSpec((tm, tn), lambda i, j, kk: (i, j)),
        out_shape=jax.ShapeDtypeStruct((m, n), F32),
        scratch_shapes=[pltpu.VMEM((tm, tn), F32)],
        compiler_params=_cparams("parallel", "parallel", "arbitrary"),
        name=name,
    )(a, w, x)
    return pl.pallas_call(
        _ln_kernel,
        grid=(m // ln_rows,),
        in_specs=[pl.BlockSpec((ln_rows, n), lambda i: (i, 0)),
                  pl.BlockSpec((1, n), lambda i: (0, 0)),
                  pl.BlockSpec((1, n), lambda i: (0, 0))],
        out_specs=[pl.BlockSpec((ln_rows, n), lambda i: (i, 0)),
                   pl.BlockSpec((ln_rows, n), lambda i: (i, 0))],
        out_shape=[jax.ShapeDtypeStruct((m, n), F32), jax.ShapeDtypeStruct((m, n), BF16)],
        compiler_params=_cparams("parallel"),
        name="layernorm",
    )(y, g.reshape(1, n), b.reshape(1, n))


def _rope_half(block, cs):
    p = block * cs
    return p + pltpu.roll(p, MLA_ROPE, axis=1)


def _q_up_kernel(p_ref, g_ref, w_ref, cs_ref, q_ref, *, rank):
    x = p_ref[...]
    ms = jnp.sum(x * x, axis=-1, keepdims=True) * (1.0 / rank)
    xn = (x * lax.rsqrt(ms + RMS_EPS) * g_ref[...]).astype(BF16)
    cs = cs_ref[...]
    hw = 2 * LANES
    for h in range(q_ref.shape[1]):
        acc = jnp.dot(xn, w_ref[:, h * hw:(h + 1) * hw], preferred_element_type=F32)
        rot = _rope_half(acc[:, LANES:hw], cs)
        q_ref[0, h, :, 0:MLA_NOPE] = acc[:, 0:MLA_NOPE].astype(BF16)
        q_ref[0, h, :, MLA_NOPE:MLA_NOPE + MLA_ROPE] = rot[:, 0:MLA_ROPE].astype(BF16)


def mla_q_up(proj, gain, w, cs, *, batch, seq, tm):
    nh = MLA_HEADS
    spt = seq // tm
    return pl.pallas_call(
        functools.partial(_q_up_kernel, rank=MLA_Q_RANK),
        grid=(batch * spt,),
        in_specs=[pl.BlockSpec((tm, Q_PAD), lambda i: (i, OFF_CQ // Q_PAD)),
                  pl.BlockSpec((1, Q_PAD), lambda i: (0, 0)),
                  pl.BlockSpec(w.shape, lambda i: (0, 0)),
                  pl.BlockSpec((tm, LANES), lambda i: (i, 0))],
        out_specs=pl.BlockSpec((1, nh, tm, MLA_NOPE + MLA_ROPE), lambda i: (i // spt, 0, i % spt, 0)),
        out_shape=jax.ShapeDtypeStruct((batch, nh, seq, MLA_NOPE + MLA_ROPE), BF16),
        compiler_params=_cparams("parallel"),
        name="mla_q_up",
    )(proj, gain, w, cs)


def _kv_up_kernel(p_ref, kr_ref, g_ref, w_ref, cs_ref, k_ref, v_ref, *, rank):
    x = p_ref[...]
    ms = jnp.sum(x * x, axis=-1, keepdims=True) * (1.0 / rank)
    xn = (x * lax.rsqrt(ms + RMS_EPS) * g_ref[...]).astype(BF16)
    krope = _rope_half(kr_ref[...], cs_ref[...])[:, 0:MLA_ROPE].astype(BF16)
    hw = MLA_NOPE + MLA_V
    for h in range(k_ref.shape[1]):
        acc = jnp.dot(xn, w_ref[:, h * hw:(h + 1) * hw], preferred_element_type=F32)
        k_ref[0, h, :, 0:MLA_NOPE] = acc[:, 0:MLA_NOPE].astype(BF16)
        k_ref[0, h, :, MLA_NOPE:MLA_NOPE + MLA_ROPE] = krope
        v_ref[0, h] = acc[:, MLA_NOPE:hw].astype(BF16)


def mla_kv_up(proj, gain, w, cs, *, batch, seq, tm):
    nh = MLA_HEADS
    spt = seq // tm
    dk = MLA_NOPE + MLA_ROPE
    return pl.pallas_call(
        functools.partial(_kv_up_kernel, rank=MLA_KV_RANK),
        grid=(batch * spt,),
        in_specs=[pl.BlockSpec((tm, MLA_KV_RANK), lambda i: (i, OFF_CKV // MLA_KV_RANK)),
                  pl.BlockSpec((tm, LANES), lambda i: (i, OFF_KR // LANES)),
                  pl.BlockSpec((1, MLA_KV_RANK), lambda i: (0, 0)),
                  pl.BlockSpec(w.shape, lambda i: (0, 0)),
                  pl.BlockSpec((tm, LANES), lambda i: (i, 0))],
        out_specs=[pl.BlockSpec((1, nh, tm, dk), lambda i: (i // spt, 0, i % spt, 0)),
                   pl.BlockSpec((1, nh, tm, MLA_V), lambda i: (i // spt, 0, i % spt, 0))],
        out_shape=[jax.ShapeDtypeStruct((batch, nh, seq, dk), BF16),
                   jax.ShapeDtypeStruct((batch, nh, seq, MLA_V), BF16)],
        compiler_params=_cparams("parallel"),
        name="mla_kv_up",
    )(proj, proj, gain, w, cs)


def _flash_kernel(q_ref, k_ref, v_ref, s_ref, o_ref, so_ref, *, tq, scale, side):
    qi = pl.program_id(2)
    side.emit((pl.program_id(0) * pl.num_programs(1) + pl.program_id(1)) * pl.num_programs(2) + qi,
              s_ref, so_ref)
    heads = range(q_ref.shape[1])
    qs = [q_ref[0, h] for h in heads]

    def scores(h, j):
        k = k_ref[0, h, pl.ds(pl.multiple_of(j * tq, tq), tq), :]
        return lax.dot_general(qs[h], k, (((1,), (1,)), ((), ())),
                               preferred_element_type=F32) * (scale * _LOG2_E)

    def update(h, j, s, m, l, acc):
        v = v_ref[0, h, pl.ds(pl.multiple_of(j * tq, tq), tq), :]
        m_new = jnp.maximum(m, jnp.max(s, axis=-1, keepdims=True))
        alpha = jnp.exp2(m - m_new)
        p = jnp.exp2(s - m_new)
        l = alpha * l + jnp.sum(p, axis=-1, keepdims=True)
        acc = alpha * acc + jnp.dot(p.astype(BF16), v, preferred_element_type=F32)
        return m_new, l, acc

    def pair(jj, carry):
        out = []
        for h in heads:
            s_a, s_b = scores(h, 2 * jj), scores(h, 2 * jj + 1)
            out.append(update(h, 2 * jj + 1, s_b, *update(h, 2 * jj, s_a, *carry[h])))
        return tuple(out)

    def single(_, carry):
        return tuple(update(h, qi - 1, scores(h, qi - 1), *carry[h]) for h in heads)

    carry = tuple((jnp.full((tq, 1), -1e30, F32), jnp.zeros((tq, 1), F32),
                   jnp.zeros((tq, MLA_V), F32)) for _ in heads)
    carry = lax.fori_loop(0, qi >> 1, pair, carry)
    carry = lax.fori_loop(0, qi & 1, single, carry)
    shift = CHUNK.bit_length() - 1
    qc = lax.broadcasted_iota(jnp.int32, (tq, tq), 0) >> shift
    kc = lax.broadcasted_iota(jnp.int32, (tq, tq), 1) >> shift
    for h in heads:
        m, l, acc = update(h, qi, jnp.where(kc <= qc, scores(h, qi), -jnp.inf), *carry[h])
        o_ref[0, :, h * MLA_V:(h + 1) * MLA_V] = (acc / l).astype(o_ref.dtype)


def mla_attention(q, k, v, side, *, tq, hps):
    batch, nh, seq, dk = q.shape
    scale = (MLA_NOPE + MLA_ROPE) ** -0.5
    nhs, nq = nh // hps, seq // tq
    assert batch * nhs * nq >= side.nblocks
    step_of = lambda b, h, i: (b * nhs + h) * nq + i
    return pl.pallas_call(
        functools.partial(_flash_kernel, tq=tq, scale=scale, side=side),
        grid=(batch, nhs, nq),
        in_specs=[pl.BlockSpec((1, hps, tq, dk), lambda b, h, i: (b, h, i, 0)),
                  pl.BlockSpec((1, hps, seq, dk), lambda b, h, i: (b, h, 0, 0),
                               pipeline_mode=pl.Buffered(1)),
                  pl.BlockSpec((1, hps, seq, MLA_V), lambda b, h, i: (b, h, 0, 0),
                               pipeline_mode=pl.Buffered(1)),
                  side.in_spec(step_of)],
        out_specs=[pl.BlockSpec((1, tq, hps * MLA_V), lambda b, h, i: (b, i, h)),
                   side.out_spec(step_of)],
        out_shape=[jax.ShapeDtypeStruct((batch, seq, nh * MLA_V), BF16), side.out_shape()],
        compiler_params=_cparams("arbitrary", "arbitrary", "arbitrary"),
        name="mla_attention",
    )(q, k, v, side.w)


def _lru_kernel(x_ref, g_ref, cw_ref, cb_ref, wa_ref, ba_ref, wx_ref, bx_ref, lam_ref, o_ref,
                tail_ref, h_ref, a_scr, u_scr):
    @pl.when(pl.program_id(1) == 0)
    def _():
        tail_ref[...] = jnp.zeros_like(tail_ref)
        h_ref[...] = jnp.zeros_like(h_ref)

    x = x_ref[...]
    ts = x.shape[0]
    tail = tail_ref[...]
    xc = x * cw_ref[LRU_CONV - 1:LRU_CONV, :] + cb_ref[...]
    for sh in range(1, LRU_CONV):
        xc = xc + _shift_rows(x, sh, tail) * cw_ref[LRU_CONV - 1 - sh:LRU_CONV - sh, :]
    tail_ref[...] = x[ts - SUBLANES:ts, :]

    xb = xc.astype(BF16)
    ra, rx = [], []
    for n in range(LRU_BLOCKS):
        blk = xb[:, n * LRU_BLOCK:(n + 1) * LRU_BLOCK]
        ra.append(jnp.dot(blk, wa_ref[n], preferred_element_type=F32))
        rx.append(jnp.dot(blk, wx_ref[n], preferred_element_type=F32))
    r = jax.nn.sigmoid(jnp.concatenate(ra, axis=1) + ba_ref[...])
    gi = jax.nn.sigmoid(jnp.concatenate(rx, axis=1) + bx_ref[...])
    log_a = -LRU_C * r * _softplus(-lam_ref[...])
    a_scr[...] = jnp.exp(log_a)
    u_scr[...] = jnp.sqrt(1.0 - jnp.exp(2.0 * log_a)) * (gi * xc)

    row8 = lax.broadcasted_iota(jnp.int32, (SUBLANES, x.shape[1]), 0)

    def body(grp, h):
        base = pl.multiple_of(grp * SUBLANES, SUBLANES)
        a8 = a_scr[pl.ds(base, SUBLANES), :]
        u8 = u_scr[pl.ds(base, SUBLANES), :]
        hs = u8
        for s in range(SUBLANES):
            h = a8[s:s + 1, :] * h + u8[s:s + 1, :]
            hs = jnp.where(row8 == s, h, hs)
        u_scr[pl.ds(base, SUBLANES), :] = hs
        return h

    h_ref[...] = lax.fori_loop(0, ts // SUBLANES, body, h_ref[...])
    o_ref[...] = (u_scr[...] * _gelu_tanh(g_ref[...])).astype(o_ref.dtype)


def rglru_branch(proj, conv_w, conv_b, w_a, b_a, w_x, b_x, lam, *, batch, seq, ts):
    w = LRU_WIDTH
    spt = seq // ts
    vec = lambda: pl.BlockSpec((1, w), lambda b, s: (0, 0))
    blockdiag = lambda: pl.BlockSpec((LRU_BLOCKS, LRU_BLOCK, LRU_BLOCK), lambda b, s: (0, 0, 0))
    return pl.pallas_call(
        _lru_kernel,
        grid=(batch, spt),
        in_specs=[pl.BlockSpec((ts, w), lambda b, s: (b * spt + s, OFF_LRU_X // w)),
                  pl.BlockSpec((ts, w), lambda b, s: (b * spt + s, OFF_LRU_G // w)),
                  pl.BlockSpec((LRU_CONV, w), lambda b, s: (0, 0)),
                  vec(), blockdiag(), vec(), blockdiag(), vec(), vec()],
        out_specs=pl.BlockSpec((ts, w), lambda b, s: (b * spt + s, 0)),
        out_shape=jax.ShapeDtypeStruct((batch * seq, w), BF16),
        scratch_shapes=[pltpu.VMEM((SUBLANES, w), F32), pltpu.VMEM((1, w), F32),
                        pltpu.VMEM((ts, w), F32), pltpu.VMEM((ts, w), F32)],
        compiler_params=_cparams("parallel", "arbitrary"),
        name="rglru",
    )(proj, proj, conv_w, conv_b.reshape(1, w), w_a.astype(BF16), b_a.reshape(1, w),
      w_x.astype(BF16), b_x.reshape(1, w), lam.reshape(1, w))


def _rwkv_prep_kernel(r_ref, k_ref, v_ref, wa_ref, xg_ref,
                      mur_ref, muk_ref, muv_ref, muwa_ref, mug_ref,
                      w0_ref, w2_ref, a0_ref, a2_ref, g2_ref, kkw_ref, ka_ref,
                      ro_ref, ko_ref, vo_ref, nkk_ref, b_ref, d_ref, g_ref,
                      tr_ref, tk_ref, tv_ref, twa_ref, tg_ref):
    @pl.when(pl.program_id(1) == 0)
    def _():
        for t in (tr_ref, tk_ref, tv_ref, twa_ref, tg_ref):
            t[...] = jnp.zeros_like(t)

    def mix(x_ref, mu_ref, tail_ref):
        x = x_ref[...]
        prev = _shift_rows(x, 1, tail_ref[...])
        tail_ref[...] = x[x.shape[0] - SUBLANES:, :]
        return x + mu_ref[...] * (prev - x)

    r = mix(r_ref, mur_ref, tr_ref)
    k = mix(k_ref, muk_ref, tk_ref)
    v = mix(v_ref, muv_ref, tv_ref)
    xwa = mix(wa_ref, muwa_ref, twa_ref)
    xg = mix(xg_ref, mug_ref, tg_ref)
    w = -_softplus(-(w0_ref[...] + jnp.dot(jnp.tanh(xwa).astype(BF16), w2_ref[...],
                                           preferred_element_type=F32))) - 0.5
    a = jax.nn.sigmoid(a0_ref[...] + jnp.dot(xwa.astype(BF16), a2_ref[...], preferred_element_type=F32))
    g = jnp.dot(jax.nn.sigmoid(xg).astype(BF16), g2_ref[...], preferred_element_type=F32)

    kk = k * kkw_ref[...]
    ones_hl = _group_ones(2 * LANES, LANES, RW_HEAD)
    sq = kk * kk
    ss = jnp.concatenate([_head_sum(sq[:, c * LANES:(c + 1) * LANES], ones_hl)
                          for c in range(RW_WIDTH // LANES)], axis=1)
    kk = kk / jnp.maximum(jnp.sqrt(ss), 1e-12)

    ro_ref[...] = r
    ko_ref[...] = k * (1.0 + (a - 1.0) * ka_ref[...])
    vo_ref[...] = v
    nkk_ref[...] = -kk
    b_ref[...] = kk * a
    d_ref[...] = jnp.exp(-jnp.exp(w))
    g_ref[...] = g


def rwkv_prep(proj, mu, w0, w2, a0, a2, g2, k_k, k_a, *, batch, seq, ts):
    w = RW_WIDTH
    spt = seq // ts
    lora = RW_DECAY_LORA + RW_AAA_LORA
    row = lambda width, col: pl.BlockSpec((ts, width), lambda b, s: (b * spt + s, col))
    vec = lambda width: pl.BlockSpec((1, width), lambda b, s: (0, 0))
    mat = lambda rows: pl.BlockSpec((rows, w), lambda b, s: (0, 0))
    mu_r, mu_k, mu_v = (mu[i * w:(i + 1) * w].reshape(1, w) for i in range(3))
    mu_wa = mu[3 * w:3 * w + lora].reshape(1, lora)
    mu_g = jnp.pad(mu[3 * w + lora:], (0, XG_PAD - RW_GATE_LORA)).reshape(1, XG_PAD)
    zeros = jnp.zeros((RW_DECAY_LORA, w), w2.dtype)
    w2p = jnp.concatenate([w2, zeros], axis=0).astype(BF16)
    a2p = jnp.concatenate([zeros, a2], axis=0).astype(BF16)
    g2p = jnp.pad(g2, ((0, XG_PAD - RW_GATE_LORA), (0, 0))).astype(BF16)
    out = jax.ShapeDtypeStruct((batch * seq, w), F32)
    return pl.pallas_call(
        _rwkv_prep_kernel,
        grid=(batch, spt),
        in_specs=[row(w, OFF_RW_R // w), row(w, OFF_RW_K // w), row(w, OFF_RW_V // w),
                  row(lora, OFF_XWA // lora), row(XG_PAD, OFF_XG // XG_PAD),
                  vec(w), vec(w), vec(w), vec(lora), vec(XG_PAD),
                  vec(w), mat(lora), vec(w), mat(lora), mat(XG_PAD), vec(w), vec(w)],
        out_specs=[pl.BlockSpec((ts, w), lambda b, s: (b * spt + s, 0))] * 7,
        out_shape=[out] * 7,
        scratch_shapes=[pltpu.VMEM((SUBLANES, w), F32)] * 3
                       + [pltpu.VMEM((SUBLANES, lora), F32), pltpu.VMEM((SUBLANES, XG_PAD), F32)],
        compiler_params=_cparams("parallel", "arbitrary"),
        name="rwkv_prep",
    )(proj, proj, proj, proj, proj, mu_r, mu_k, mu_v, mu_wa, mu_g,
      w0.reshape(1, w), w2p, a0.reshape(1, w), a2p, g2p, k_k.reshape(1, w), k_a.reshape(1, w))


def _rwkv_rec_kernel(*refs, sides):
    ns = len(sides)
    r_ref, k_ref, v_ref, nkk_ref, b_ref, d_ref, g_ref, rk_ref, gng_ref, gnb_ref = refs[:10]
    side_src = refs[10:10 + ns]
    y_ref = refs[10 + ns]
    side_dst = refs[11 + ns:11 + 2 * ns]
    st_ref, q_ref, ot_ref = refs[11 + 2 * ns:]
    for side, src, dst in zip(sides, side_src, side_dst):
        side.emit(pl.program_id(0), src, dst)

    nb, tau, width = r_ref.shape
    npair = width // LANES
    hd = RW_HEAD
    pairs = [(bi, p) for bi in range(nb) for p in range(npair)]
    flat = [(i, bi, p) for i, (bi, p) in enumerate(pairs)]
    groups = [flat[j:j + RW_GROUP_PAIRS] for j in range(0, len(flat), RW_GROUP_PAIRS)]

    @pl.when(pl.program_id(0) == 0)
    def _():
        st_ref[...] = jnp.zeros_like(st_ref)

    q_ref[...] = jnp.zeros_like(q_ref)
    ot_ref[...] = jnp.zeros_like(ot_ref)

    row = lax.broadcasted_iota(jnp.int32, (hd, LANES), 0)
    lane = lax.broadcasted_iota(jnp.int32, (hd, LANES), 1)
    lane_in_head = lane & (hd - 1)
    diag = lane_in_head == row
    ones2 = _group_ones(2 * LANES, 2 * LANES, hd)

    def group(grp, carry):
        base = pl.multiple_of(grp * SUBLANES, SUBLANES)
        tile = lambda ref, bi, p: ref[bi, pl.ds(base, SUBLANES), pl.ds(p * LANES, LANES)]
        vcb = [None] * len(groups)
        for s in range(SUBLANES):
            sel = lane_in_head == (base + s - 1)
            for gi, grp_pairs in enumerate(groups):
                lhs = []
                for i, bi, p in grp_pairs:
                    pm = (st_ref[bi, p] * tile(nkk_ref, bi, p)[s:s + 1, :]).astype(BF16)
                    lhs.append(jnp.concatenate([pm, q_ref[pl.ds(i * hd, hd), :]], axis=1))
                if s % 2 == 0:
                    for i, bi, p in grp_pairs:
                        v8 = tile(v_ref, bi, p)
                        lhs.append(jnp.concatenate(
                            [jnp.where(diag, v8[s:s + 1, :], 0.0).astype(BF16),
                             jnp.where(diag, v8[s + 1:s + 2, :], 0.0).astype(BF16)], axis=1))
                out = jnp.dot(jnp.concatenate(lhs, axis=0), ones2, preferred_element_type=F32)
                if s % 2 == 0:
                    vcb[gi] = out[len(grp_pairs) * hd:, :]
                for j, (i, bi, p) in enumerate(grp_pairs):
                    rows = slice(j * hd, (j + 1) * hd)
                    ot_ref[bi, p] = jnp.where(sel, out[rows, LANES:], ot_ref[bi, p])
                    vc = vcb[gi][rows, 0:LANES] if s % 2 == 0 else vcb[gi][rows, LANES:]
                    new = (st_ref[bi, p] * tile(d_ref, bi, p)[s:s + 1, :]
                           + out[rows, 0:LANES] * tile(b_ref, bi, p)[s:s + 1, :]
                           + vc * tile(k_ref, bi, p)[s:s + 1, :])
                    st_ref[bi, p] = new
                    q_ref[pl.ds(i * hd, hd), :] = (new * tile(r_ref, bi, p)[s:s + 1, :]).astype(BF16)
        return carry

    lax.fori_loop(0, tau // SUBLANES, group, 0)

    o_last = jnp.dot(q_ref[...], ones2[0:LANES, 0:LANES], preferred_element_type=F32)
    sel = lane_in_head == (tau - 1)
    for i, (bi, p) in enumerate(pairs):
        ot_ref[bi, p] = jnp.where(sel, o_last[i * hd:(i + 1) * hd, :], ot_ref[bi, p])

    ones_hl = _group_ones(2 * LANES, LANES, hd)
    first_head = lane < hd
    inv_hd = 1.0 / hd
    for bi in range(nb):
        nat = []
        for p0 in range(0, npair, 2):
            tr = jnp.concatenate([ot_ref[bi, p0], ot_ref[bi, p0 + 1]], axis=0).T
            top, bot = tr[0:hd, :], tr[hd:2 * hd, :]
            nat.append(jnp.where(first_head, top, pltpu.roll(bot, hd, axis=1)))
            nat.append(jnp.where(first_head, pltpu.roll(top, hd, axis=1), bot))
        cols = [pl.ds(p * LANES, LANES) for p in range(npair)]
        o = jnp.concatenate(nat, axis=0)
        mean = _head_sum(o, ones_hl) * inv_hd
        oc = o - mean
        var = _head_sum(oc * oc, ones_hl) * inv_hd
        rkr = jnp.concatenate([r_ref[bi, :, c] * k_ref[bi, :, c] * rk_ref[:, c] for c in cols], axis=0)
        bonus = _head_sum(rkr, ones_hl)
        on = oc * lax.rsqrt(var + GN_EPS)
        for p, c in enumerate(cols):
            rows = slice(p * tau, (p + 1) * tau)
            y = on[rows] * gng_ref[:, c] + gnb_ref[:, c] + bonus[rows] * v_ref[bi, :, c]
            y_ref[bi, :, c] = (y * g_ref[bi, :, c]).astype(y_ref.dtype)


def rwkv_recurrence(r, k, v, nkk, b, d, g, r_k, gn_g, gn_b, sides, *, batch, seq):
    width = RW_WIDTH
    tau = RW_HEAD
    npair = width // LANES
    nsteps = seq // tau
    assert all(s.nblocks <= nsteps for s in sides)
    spec = pl.BlockSpec((batch, tau, width), lambda i: (0, i, 0))
    vec = pl.BlockSpec((1, width), lambda i: (0, 0))
    acts = [a.reshape(batch, seq, width) for a in (r, k, v, nkk, b, d, g)]
    step_of = lambda i: i
    return pl.pallas_call(
        functools.partial(_rwkv_rec_kernel, sides=sides),
        grid=(nsteps,),
        in_specs=[spec] * 7 + [vec] * 3 + [s.in_spec(step_of) for s in sides],
        out_specs=[spec] + [s.out_spec(step_of) for s in sides],
        out_shape=[jax.ShapeDtypeStruct((batch, seq, width), BF16)] + [s.out_shape() for s in sides],
        scratch_shapes=[pltpu.VMEM((batch, npair, RW_HEAD, LANES), F32),
                        pltpu.VMEM((batch * npair * RW_HEAD, LANES), BF16),
                        pltpu.VMEM((batch, npair, RW_HEAD, LANES), F32)],
        compiler_params=_cparams("arbitrary"),
        name="rwkv_recurrence",
    )(*acts, r_k.reshape(1, width), gn_g.reshape(1, width), gn_b.reshape(1, width),
      *[s.w for s in sides])


def _merge_kernel(a_ref, l_ref, r_ref, wa_ref, wl_ref, wr_ref, ga_ref, gl_ref, gr_ref, o_ref):
    acc = jax.nn.sigmoid(ga_ref[...]) * jnp.dot(a_ref[...], wa_ref[...], preferred_element_type=F32)
    acc += jax.nn.sigmoid(gl_ref[...]) * jnp.dot(l_ref[...], wl_ref[...], preferred_element_type=F32)
    acc += jax.nn.sigmoid(gr_ref[...]) * jnp.dot(r_ref[...], wr_ref[...], preferred_element_type=F32)
    o_ref[...] = acc.astype(o_ref.dtype)


def merge_branches(y_a, y_l, y_r, w_a, w_l, w_r, proj, *, tm, tn):
    m = y_a.shape[0]
    n = w_a.shape[1]
    nj = n // tn
    act = lambda width: pl.BlockSpec((tm, width), lambda i, j: (i, 0))
    wgt = lambda width: pl.BlockSpec((width, tn), lambda i, j: (0, j))
    gate = lambda g: pl.BlockSpec((tm, tn), lambda i, j: (i, g * nj + j))
    return pl.pallas_call(
        _merge_kernel,
        grid=(m // tm, nj),
        in_specs=[act(y_a.shape[1]), act(y_l.shape[1]), act(y_r.shape[1]),
                  wgt(w_a.shape[0]), wgt(w_l.shape[0]), wgt(w_r.shape[0]),
                  gate(0), gate(1), gate(2)],
        out_specs=pl.BlockSpec((tm, tn), lambda i, j: (i, j)),
        out_shape=jax.ShapeDtypeStruct((m, n), BF16),
        compiler_params=_cparams("parallel", "parallel"),
        name="merge_branches",
    )(y_a, y_l, y_r, w_a, w_l, w_r, proj, proj, proj)


def _ffn_up_kernel(x_ref, *refs, tiles_per_seq, valid_tiles, nsub):
    wg_refs, wv_refs = refs[0:nsub], refs[nsub:2 * nsub]
    cw_refs, cb_refs = refs[2 * nsub:3 * nsub], refs[3 * nsub:4 * nsub]
    o_ref, halo_ref = refs[4 * nsub], refs[4 * nsub + 1]
    j = pl.program_id(0)
    i = pl.program_id(1)

    @pl.when(i % tiles_per_seq == 0)
    def _():
        halo_ref[...] = jnp.zeros_like(halo_ref)

    x = x_ref[...]
    tm = x.shape[0]
    for c in range(nsub):
        tn = wg_refs[c].shape[1]
        cols = slice(c * tn, (c + 1) * tn)
        g = jnp.dot(x, wg_refs[c][...], preferred_element_type=F32)
        tail = halo_ref[:, cols]
        acc = g * cw_refs[c][FFN_CONV - 1:FFN_CONV, :] + cb_refs[c][...]
        for sh in range(1, FFN_CONV):
            acc = acc + _shift_rows(g, sh, tail) * cw_refs[c][FFN_CONV - 1 - sh:FFN_CONV - sh, :]
        halo_ref[:, cols] = g[tm - SUBLANES:tm, :]
        v = jnp.dot(x, wv_refs[c][...], preferred_element_type=F32)
        h = _gelu_tanh(acc) * v
        o_ref[:, cols] = jnp.where(j * nsub + c < valid_tiles, h, 0.0).astype(o_ref.dtype)


def ffn_up(x, w, conv_w, conv_b, layer, *, seq, tm, tn, nsub, dff_pad):
    m, k = x.shape
    dff = conv_w.shape[-1]
    valid = dff // tn
    last = valid - 1
    tile = lambda c: (lambda j, i: jnp.minimum(j * nsub + c, last))
    wspec = lambda c, off: pl.BlockSpec((k, tn), lambda j, i: (0, off + tile(c)(j, i)))
    pspec = lambda c, rows: pl.BlockSpec((None, rows, tn), lambda j, i: (layer, 0, tile(c)(j, i)))
    subs = range(nsub)
    return pl.pallas_call(
        functools.partial(_ffn_up_kernel, tiles_per_seq=seq // tm, valid_tiles=valid, nsub=nsub),
        grid=(dff_pad // (tn * nsub), m // tm),
        in_specs=[pl.BlockSpec((tm, k), lambda j, i: (i, 0))]
                 + [wspec(c, 0) for c in subs] + [wspec(c, valid) for c in subs]
                 + [pspec(c, FFN_CONV) for c in subs] + [pspec(c, 1) for c in subs],
        out_specs=pl.BlockSpec((tm, tn * nsub), lambda j, i: (i, j)),
        out_shape=jax.ShapeDtypeStruct((m, dff_pad), BF16),
        scratch_shapes=[pltpu.VMEM((SUBLANES, tn * nsub), F32)],
        compiler_params=_cparams("arbitrary", "arbitrary"),
        name="ffn_up",
    )(x, *([w] * (2 * nsub)), *([conv_w] * nsub), *([conv_b] * nsub))


def _rotate_half_cols(w):
    half = MLA_ROPE // 2
    return jnp.concatenate([-w[..., half:], w[..., :half]], axis=-1)


SRC_CKV = MLA_Q_RANK
SRC_KR = SRC_CKV + MLA_KV_RANK
SRC_LRU = SRC_KR + MLA_ROPE
SRC_RW = SRC_LRU + 2 * LRU_WIDTH
SRC_XWA = SRC_RW + 3 * RW_WIDTH
SRC_XG = SRC_XWA + RW_DECAY_LORA + RW_AAA_LORA
SRC_GATES = SRC_XG + RW_GATE_LORA


W_IN_TILE = 256
_T_LRU = (3 * D_MODEL) // W_IN_TILE
_T_RW = _T_LRU + (2 * LRU_WIDTH) // W_IN_TILE
_T_CQ = _T_RW + (3 * RW_WIDTH) // W_IN_TILE
_T_CKV = _T_CQ + Q_PAD // W_IN_TILE
_T_MIX = _T_CKV + MLA_KV_RANK // W_IN_TILE
_T_XG = _T_MIX + 1


def _w_in_src_row(t):
    tile = W_IN_TILE
    return jnp.where(
        t < _T_LRU, SRC_GATES + tile * t, jnp.where(
            t < _T_RW, SRC_LRU + tile * (t - _T_LRU), jnp.where(
                t < _T_CQ, SRC_RW + tile * (t - _T_RW), jnp.where(
                    t < _T_CKV, tile * (t - _T_CQ), jnp.where(
                        t < _T_MIX, SRC_CKV + tile * (t - _T_CKV), jnp.where(
                            t == _T_MIX, SRC_KR, SRC_XG))))))


def _w_in_kernel(a_ref, b_ref, o_ref):
    t = pl.program_id(0)
    a = a_ref[...]

    @pl.when(t != _T_MIX)
    def _():
        valid = jnp.where(t == _T_CKV - 1, MLA_Q_RANK - (Q_PAD - W_IN_TILE),
                          jnp.where(t == _T_XG, RW_GATE_LORA, W_IN_TILE))
        row = lax.broadcasted_iota(jnp.int32, a.shape, 0)
        o_ref[...] = jnp.where(row < valid, a, 0.0).astype(BF16)

    @pl.when(t == _T_MIX)
    def _():
        half = MLA_ROPE // 2
        lora = RW_DECAY_LORA + RW_AAA_LORA
        o_ref[...] = jnp.concatenate(
            [a[0:MLA_ROPE], -a[half:MLA_ROPE], a[0:half], b_ref[0:lora]], axis=0).astype(BF16)


def relayout_w_in(w_t, layer, *, k):
    window = lambda index: pl.BlockSpec((None, pl.Element(W_IN_TILE), pl.Element(k)), index)
    return pl.pallas_call(
        _w_in_kernel,
        grid=(IN_COLS_PAD // W_IN_TILE,),
        in_specs=[window(lambda t: (layer, pl.multiple_of(_w_in_src_row(t), SUBLANES), 0)),
                  window(lambda t: (layer, SRC_XWA, 0))],
        out_specs=pl.BlockSpec((W_IN_TILE, k), lambda t: (t, 0)),
        out_shape=jax.ShapeDtypeStruct((IN_COLS_PAD, k), BF16),
        compiler_params=_cparams("arbitrary"),
        name="w_in_relayout",
    )(w_t, w_t)


def _cast_kernel(s_ref, o_ref, *, valid_blocks):
    i = pl.program_id(0)

    @pl.when(i < valid_blocks)
    def _():
        o_ref[...] = s_ref[...].astype(o_ref.dtype)

    @pl.when(i >= valid_blocks)
    def _():
        o_ref[...] = jnp.zeros_like(o_ref)


def cast_bf16(w, layer=None, *, tr, rows_out=None):
    r, c = w.shape[-2:]
    rows_out = r if rows_out is None else rows_out
    valid = r // tr
    if layer is None:
        src = pl.BlockSpec((tr, c), lambda i: (jnp.minimum(i, valid - 1), 0))
    else:
        src = pl.BlockSpec((None, tr, c), lambda i: (layer, jnp.minimum(i, valid - 1), 0))
    return pl.pallas_call(
        functools.partial(_cast_kernel, valid_blocks=valid),
        grid=(rows_out // tr,),
        in_specs=[src],
        out_specs=pl.BlockSpec((tr, c), lambda i: (i, 0)),
        out_shape=jax.ShapeDtypeStruct((rows_out, c), BF16),
        compiler_params=_cparams("parallel"),
        name="cast_bf16",
    )(w)


def _prep_w_uq(w):
    rope = w[..., MLA_NOPE:]
    w = jnp.concatenate([w, _rotate_half_cols(rope)], axis=-1)
    w = w.reshape(MLA_Q_RANK, MLA_HEADS * 2 * LANES)
    return jnp.pad(w, ((0, Q_PAD - MLA_Q_RANK), (0, 0))).astype(BF16)


def _rope_table(positions):
    inv_freq = ROPE_THETA ** (-jnp.arange(0, MLA_ROPE, 2, dtype=F32) / MLA_ROPE)
    ang = positions.astype(F32)[..., None] * inv_freq
    cos, sin = jnp.cos(ang), jnp.sin(ang)
    return jnp.concatenate([cos, cos, sin, sin], axis=-1).reshape(-1, 2 * MLA_ROPE)


class _Tiles:
    in_proj = (1024, 1024)
    mla_up_rows = 512
    attn_q = 512
    attn_heads = 4
    lru_rows = 512
    rwkv_prep_rows = 256
    merge = (512, 1024)
    out_mm = (1024, 2048, 1024)
    ln_rows = 256
    ffn_up = (1024, 256, 2)
    attn_side_rows = 256
    in_proj_side_rows = 32
    cast_rows = 256
    side_rows = 128


def kernel(x, positions, w_in, mla_q_norm, mla_w_uq, mla_kv_norm, mla_w_ukv, lru_conv_w, lru_conv_b, lru_w_a, lru_b_a, lru_w_x, lru_b_x, lru_lambda, rw_mu, rw_w0, rw_w2, rw_a0, rw_a2, rw_g2, rw_k_k, rw_k_a, rw_r_k, rw_gn_g, rw_gn_b, w_o_mla, w_o_lru, w_o_rwkv, w_out, ln1_g, ln1_b, ffn_w_up, ffn_conv_w, ffn_conv_b, ffn_w_down, ln2_g, ln2_b):
    batch, seq, d = x.shape
    m = batch * seq
    t = _Tiles
    cs = _rope_table(positions)
    xf = x.reshape(m, d)
    cast = functools.partial(cast_bf16, tr=t.cast_rows)
    xb = cast(xf)
    w_ukv_all = mla_w_ukv.reshape(DEPTH, MLA_KV_RANK, -1)
    conv_b_all = ffn_conv_b.reshape(DEPTH, 1, -1)
    w_in_t = jnp.transpose(w_in, (0, 2, 1))
    for l in range(DEPTH):
        proj, w_up = matmul_nt(xb, relayout_w_in(w_in_t, l, k=d),
                               _SideCast(ffn_w_up, l, rows=t.in_proj_side_rows),
                               tm=t.in_proj[0], tn=t.in_proj[1])

        q_gain = jnp.pad(mla_q_norm[l], (0, Q_PAD - MLA_Q_RANK)).reshape(1, Q_PAD)
        q = mla_q_up(proj, q_gain, _prep_w_uq(mla_w_uq[l]), cs, batch=batch, seq=seq, tm=t.mla_up_rows)
        k, v = mla_kv_up(proj, mla_kv_norm[l].reshape(1, -1), cast(w_ukv_all, l), cs, batch=batch,
                         seq=seq, tm=t.mla_up_rows)
        y_a, w_down = mla_attention(
            q, k, v, _SideCast(ffn_w_down, l, rows=t.attn_side_rows, rows_out=D_FF_PAD),
            tq=t.attn_q, hps=t.attn_heads)
        y_a = y_a.reshape(m, -1)

        y_l = rglru_branch(proj, lru_conv_w[l], lru_conv_b[l], lru_w_a[l], lru_b_a[l],
                           lru_w_x[l], lru_b_x[l], lru_lambda[l], batch=batch, seq=seq, ts=t.lru_rows)

        rw = rwkv_prep(proj, rw_mu[l], rw_w0[l], rw_w2[l], rw_a0[l], rw_a2[l], rw_g2[l],
                       rw_k_k[l], rw_k_a[l], batch=batch, seq=seq, ts=t.rwkv_prep_rows)
        sides = [_SideCast(w, l, rows=t.side_rows) for w in (w_o_mla, w_o_lru, w_o_rwkv, w_out)]
        y_r, wo_a, wo_l, wo_r, wo = rwkv_recurrence(*rw, rw_r_k[l], rw_gn_g[l], rw_gn_b[l], sides,
                                                    batch=batch, seq=seq)
        y_r = y_r.reshape(m, RW_WIDTH)

        merged = merge_branches(y_a, y_l, y_r, wo_a, wo_l, wo_r, proj, tm=t.merge[0], tn=t.merge[1])
        mm = dict(tm=t.out_mm[0], tn=t.out_mm[1], tk=t.out_mm[2], ln_rows=t.ln_rows)
        xf, xb = matmul_residual_layernorm(merged, wo, xf, ln1_g[l], ln1_b[l], name="mixer_out", **mm)

        h = ffn_up(xb, w_up, ffn_conv_w, conv_b_all, l, seq=seq, tm=t.ffn_up[0], tn=t.ffn_up[1],
                   nsub=t.ffn_up[2], dff_pad=D_FF_PAD)
        xf, xb = matmul_residual_layernorm(h, w_down, xf, ln2_g[l], ln2_b[l], name="ffn_down", **mm)
    return xf.reshape(batch, seq, d)
```
